```python
import jax, jax.numpy as jnp
from jax import lax
import numpy as np

D_MODEL = 2048
BATCH = 2
SEQ = 8192
DEPTH = 1

CHUNK = 64
SB_HEADS = 16
SB_HEAD_DIM = 128
SB_WIDTH = SB_HEADS * SB_HEAD_DIM
SB_BLOCK = 128
SSD_EXPAND = 2
SSD_INNER = SSD_EXPAND * D_MODEL
SSD_HEAD_DIM = 64
SSD_HEADS = SSD_INNER // SSD_HEAD_DIM
SSD_GROUPS = 8
SSD_STATE = 128
SSD_CONV = 4
SSD_XBC = SSD_INNER + 2 * SSD_GROUPS * SSD_STATE
SSD_CHUNK = CHUNK
D_FF = 5632
FFN_CONV = 3
N_GATES = 2
SPLITS = (SB_WIDTH, SB_WIDTH, SB_WIDTH, SSD_INNER, SSD_XBC, SSD_HEADS, D_MODEL, D_MODEL)
N_IN = sum(SPLITS)
EPS = 1e-6

kernel_name = "hybrid_sb_ssd_convffn_block"


def rms_norm(x, g):
    xf = x.astype(jnp.float32)
    y = xf * lax.rsqrt(jnp.mean(xf * xf, axis=-1, keepdims=True) + EPS)
    return (y * g.astype(jnp.float32)).astype(x.dtype)


def causal_dwconv(x, w, b):
    k_len = w.shape[0]
    s = x.shape[1]
    xp = jnp.pad(x, ((0, 0), (k_len - 1, 0), (0, 0)))
    y = b
    for k in range(k_len):
        y = y + xp[:, k:k + s] * w[k]
    return y


def stick_breaking_attention(q, k, v):
    b, h, s, dh = q.shape
    nb = s // SB_BLOCK
    scale = dh ** -0.5
    q_blocks = q.reshape(b, h, nb, SB_BLOCK, dh).transpose(2, 0, 1, 3, 4)
    k_pos = jnp.arange(s)

    def block(args):
        q_blk, i = args
        z = jnp.einsum('bhqd,bhkd->bhqk', q_blk, k).astype(jnp.float32) * scale
        q_pos = i * SB_BLOCK + jnp.arange(SB_BLOCK)
        mask = k_pos[None, :] < q_pos[:, None]
        log_keep = jnp.where(mask, jax.nn.log_sigmoid(-z), 0.0)
        after = lax.cumsum(log_keep, axis=3, reverse=True) - log_keep
        w = jnp.where(mask, jnp.exp(jax.nn.log_sigmoid(z) + after), 0.0)
        return jnp.einsum('bhqk,bhkd->bhqd', w.astype(v.dtype), v)

    out = lax.map(block, (q_blocks, jnp.arange(nb)))
    return out.transpose(1, 2, 0, 3, 4).reshape(b, h, s, dh)


def ssd_scan(xh, dt, a, bm, cm):
    b, s, h, p = xh.shape
    g, n = bm.shape[-2:]
    hg = h // g
    nc, l = s // SSD_CHUNK, SSD_CHUNK
    x = (xh * dt[..., None]).reshape(b, nc, l, g, hg, p)
    la = (dt * a).astype(jnp.float32).reshape(b, nc, l, g, hg).transpose(0, 3, 4, 1, 2)
    cs = jnp.cumsum(la, axis=-1)
    bm = bm.reshape(b, nc, l, g, n)
    cm = cm.reshape(b, nc, l, g, n)
    causal = jnp.tril(jnp.ones((l, l), dtype=bool))
    seg = cs[..., :, None] - cs[..., None, :]
    decay = jnp.exp(jnp.where(causal, seg, -jnp.inf))
    cb = jnp.einsum('bclgn,bcsgn->bcgls', cm, bm)
    y_diag = jnp.einsum('bcgls,bghcls,bcsghp->bclghp', cb, decay, x)
    decay_to_end = jnp.exp(cs[..., -1:] - cs)
    states = jnp.einsum('bclgn,bghcl,bclghp->bcghpn', bm, decay_to_end, x)
    chunk_decay = jnp.exp(cs[..., -1])

    def step(carry, inp):
        st, dec = inp
        return carry * dec[..., None, None] + st, carry

    init = jnp.zeros((b, g, hg, p, n), dtype=states.dtype)
    _, prev = lax.scan(step, init, (states.transpose(1, 0, 2, 3, 4, 5), chunk_decay.transpose(3, 0, 1, 2)))
    prev = prev.transpose(1, 0, 2, 3, 4, 5)
    y_off = jnp.einsum('bclgn,bcghpn,bghcl->bclghp', cm, prev, jnp.exp(cs))
    return (y_diag + y_off).reshape(b, s, h, p).astype(xh.dtype)


def setup_inputs(seed: int = 0) -> dict:
    key = jax.random.key(seed)
    ks = jax.random.split(key, 24)
    nrm = lambda k, shape, scale: jax.random.normal(k, shape, jnp.float32) * scale
    gain = lambda k, dim: 1.0 + 0.05 * jax.random.normal(k, (DEPTH, dim), jnp.float32)
    dt0 = jnp.exp(jax.random.uniform(ks[5], (DEPTH, SSD_HEADS), jnp.float32, np.log(1e-3), np.log(1e-1)))
    return {
        "x": nrm(ks[0], (BATCH, SEQ, D_MODEL), 1.0),
        "norm_mix_pre": gain(ks[1], D_MODEL),
        "w_in": nrm(ks[2], (DEPTH, D_MODEL, N_IN), D_MODEL ** -0.5),
        "ssd_conv_w": nrm(ks[3], (DEPTH, SSD_CONV, SSD_XBC), SSD_CONV ** -0.5),
        "ssd_conv_b": nrm(ks[4], (DEPTH, SSD_XBC), 0.01),
        "dt_bias": dt0 + jnp.log(-jnp.expm1(-dt0)),
        "a_log": jnp.log(jax.random.uniform(ks[6], (DEPTH, SSD_HEADS), jnp.float32, 1.0, 16.0)),
        "d_skip": gain(ks[7], SSD_HEADS),
        "ssd_norm": gain(ks[8], SSD_INNER),
        "w_sb_proj": nrm(ks[9], (DEPTH, SB_WIDTH, D_MODEL), SB_WIDTH ** -0.5),
        "w_ssd_proj": nrm(ks[10], (DEPTH, SSD_INNER, D_MODEL), SSD_INNER ** -0.5),
        "w_out": nrm(ks[11], (DEPTH, D_MODEL, D_MODEL), D_MODEL ** -0.5),
        "norm_mix_post": gain(ks[12], D_MODEL),
        "norm_ffn_pre": gain(ks[13], D_MODEL),
        "w_up": nrm(ks[14], (DEPTH, D_MODEL, 2 * D_FF), D_MODEL ** -0.5),
        "ffn_conv_w": nrm(ks[15], (DEPTH, FFN_CONV, 2 * D_FF), FFN_CONV ** -0.5),
        "ffn_conv_b": nrm(ks[16], (DEPTH, 2 * D_FF), 0.01),
        "w_down": nrm(ks[17], (DEPTH, D_FF, D_MODEL), D_FF ** -0.5),
        "norm_ffn_post": gain(ks[18], D_MODEL),
    }


def reference(x, norm_mix_pre, w_in, ssd_conv_w, ssd_conv_b, dt_bias, a_log, d_skip, ssd_norm,
              w_sb_proj, w_ssd_proj, w_out, norm_mix_post, norm_ffn_pre, w_up, ffn_conv_w,
              ffn_conv_b, w_down, norm_ffn_post):
    b, s, _ = x.shape
    cuts = list(np.cumsum(SPLITS)[:-1])
    for i in range(DEPTH):
        h = rms_norm(x, norm_mix_pre[i])
        proj = h @ w_in[i]
        q, k, v, z, xbc, dt, g_sb, g_ssd = jnp.split(proj, cuts, axis=-1)
        to_heads = lambda t: t.reshape(b, s, SB_HEADS, SB_HEAD_DIM).transpose(0, 2, 1, 3)
        o_sb = stick_breaking_attention(to_heads(q), to_heads(k), to_heads(v))
        o_sb = o_sb.transpose(0, 2, 1, 3).reshape(b, s, SB_WIDTH)
        xbc = jax.nn.silu(causal_dwconv(xbc, ssd_conv_w[i], ssd_conv_b[i]))
        xs, bm, cm = jnp.split(xbc, [SSD_INNER, SSD_INNER + SSD_GROUPS * SSD_STATE], axis=-1)
        dt = jax.nn.softplus(dt + dt_bias[i])
        a = -jnp.exp(a_log[i])
        xh = xs.reshape(b, s, SSD_HEADS, SSD_HEAD_DIM)
        y = ssd_scan(xh, dt, a,
                     bm.reshape(b, s, SSD_GROUPS, SSD_STATE),
                     cm.reshape(b, s, SSD_GROUPS, SSD_STATE))
        y = (y + xh * d_skip[i][:, None]).reshape(b, s, SSD_INNER) * jax.nn.silu(z)
        y = rms_norm(y.reshape(b, s, SSD_GROUPS, SSD_INNER // SSD_GROUPS),
                     ssd_norm[i].reshape(SSD_GROUPS, SSD_INNER // SSD_GROUPS)).reshape(b, s, SSD_INNER)
        merged = (jax.nn.sigmoid(g_sb) * (o_sb @ w_sb_proj[i])
                  + jax.nn.sigmoid(g_ssd) * (y @ w_ssd_proj[i]))
        x = x + rms_norm(merged @ w_out[i], norm_mix_post[i])
        h = rms_norm(x, norm_ffn_pre[i])
        u = causal_dwconv(h @ w_up[i], ffn_conv_w[i], ffn_conv_b[i])
        gate, val = jnp.split(u, [D_FF], axis=-1)
        f = (jax.nn.gelu(gate, approximate=True) * val) @ w_down[i]
        x = x + rms_norm(f, norm_ffn_post[i])
    return x
```

```python
import dataclasses
import functools

import jax
import jax.numpy as jnp
from jax import lax
from jax.experimental import pallas as pl
from jax.experimental.pallas import tpu as pltpu

F32 = jnp.float32
BF16 = jnp.bfloat16
LANE = 128
EPS = 1e-6


@dataclasses.dataclass(frozen=True)
class Dims:
    d_model: int = 2048
    sb_heads: int = 16
    sb_head_dim: int = 128
    ssd_inner: int = 4096
    ssd_head_dim: int = 64
    ssd_groups: int = 8
    ssd_state: int = 128
    ssd_conv: int = 4
    d_ff: int = 5632
    ffn_conv: int = 3

    @property
    def sb_width(self):
        return self.sb_heads * self.sb_head_dim

    @property
    def ssd_heads(self):
        return self.ssd_inner // self.ssd_head_dim

    @property
    def ssd_xbc(self):
        return self.ssd_inner + 2 * self.ssd_groups * self.ssd_state


SB_LOG_WEIGHT_FLOOR = -110.0
VMEM_LIMIT = 56 * 1024 * 1024


def _softplus(x):
    return jnp.maximum(x, 0.0) + jnp.log1p(jnp.exp(-jnp.abs(x)))


def _params(*sem):
    return pltpu.CompilerParams(dimension_semantics=sem, vmem_limit_bytes=VMEM_LIMIT)


def _in_proj_kernel(x_ref, g_ref, w_ref, wdt_ref, p_ref, dt_ref, h_scr):
    @pl.when(pl.program_id(1) == 0)
    def _():
        x = x_ref[...]
        ms = jnp.mean(x * x, axis=-1, keepdims=True)
        h = (x * lax.rsqrt(ms + EPS) * g_ref[...]).astype(BF16)
        h_scr[...] = h
        dt_ref[...] = lax.dot_general(wdt_ref[...], h, (((1,), (1,)), ((), ())),
                                      preferred_element_type=F32)

    acc = jnp.dot(h_scr[...], w_ref[...], preferred_element_type=F32)
    for c in range(p_ref.shape[0]):
        p_ref[c] = acc[:, c * LANE:(c + 1) * LANE].astype(BF16)


def _in_proj(x2, gain, w_main, w_dt_t, tm, tn):
    m, d = x2.shape
    n = w_main.shape[1]
    hd = w_dt_t.shape[0]
    return pl.pallas_call(
        _in_proj_kernel,
        grid=(m // tm, n // tn),
        in_specs=[
            pl.BlockSpec((tm, d), lambda i, j: (i, 0)),
            pl.BlockSpec((1, d), lambda i, j: (0, 0)),
            pl.BlockSpec((d, tn), lambda i, j: (0, j)),
            pl.BlockSpec((hd, d), lambda i, j: (0, 0)),
        ],
        out_specs=[
            pl.BlockSpec((tn // LANE, tm, LANE), lambda i, j: (j, i, 0)),
            pl.BlockSpec((hd, tm), lambda i, j: (0, i)),
        ],
        out_shape=[
            jax.ShapeDtypeStruct((n // LANE, m, LANE), BF16),
            jax.ShapeDtypeStruct((hd, m), F32),
        ],
        scratch_shapes=[pltpu.VMEM((tm, d), BF16)],
        compiler_params=_params("parallel", "arbitrary"),
        name="in_proj",
    )(x2, gain, w_main, w_dt_t)


def _sb_attn_kernel(q_ref, k_ref, v_ref, o_ref, *, tq, scale):
    s, dh = q_ref.shape
    nq = s // tq
    row = lax.broadcasted_iota(jnp.int32, (tq, tq), 0)
    col = lax.broadcasted_iota(jnp.int32, (tq, tq), 1)
    causal = col < row
    suffix = jnp.where(row > col, 1.0, 0.0).astype(BF16)

    def key_block(qb, j, carry, acc, masked):
        start = pl.multiple_of(j * tq, tq)
        kb = k_ref[pl.ds(start, tq), :]
        vb = v_ref[pl.ds(start, tq), :]
        z = lax.dot_general(qb, kb, (((1,), (1,)), ((), ())), preferred_element_type=F32) * scale
        sp = _softplus(z)
        lk = -sp
        if masked:
            lk = jnp.where(causal, lk, 0.0)
        hi = lk.astype(BF16)
        lo = (lk - hi.astype(F32)).astype(BF16)
        after = (jnp.dot(hi, suffix, preferred_element_type=F32)
                 + jnp.dot(lo, suffix, preferred_element_type=F32) + carry)
        w = jnp.exp(z - sp + after)
        if masked:
            w = jnp.where(causal, w, 0.0)
        acc = acc + jnp.dot(w.astype(BF16), vb, preferred_element_type=F32)
        carry = carry + jnp.sum(lk, axis=1, keepdims=True)
        return carry, acc

    def q_block(i, _):
        qstart = pl.multiple_of(i * tq, tq)
        qb = q_ref[pl.ds(qstart, tq), :]
        carry, acc = key_block(qb, i, jnp.zeros((tq, 1), F32), jnp.zeros((tq, dh), F32), True)

        def cond(st):
            j, _, _, top = st
            return jnp.logical_and(j >= 0, top > SB_LOG_WEIGHT_FLOOR)

        def body(st):
            j, carry, acc, _ = st
            carry, acc = key_block(qb, j, carry, acc, False)
            return j - 1, carry, acc, jnp.max(carry)

        _, _, acc, _ = lax.while_loop(cond, body, (i - 1, carry, acc, jnp.max(carry)))
        o_ref[pl.ds(qstart, tq), :] = acc.astype(o_ref.dtype)
        return 0

    lax.fori_loop(0, nq, q_block, 0)


def _sb_attn(p, dims, batch, seq, tq):
    h, dh = dims.sb_heads, dims.sb_head_dim
    m = batch * seq
    spec = lambda off: pl.BlockSpec((None, seq, dh), lambda b, hh: (off + hh, b, 0))
    return pl.pallas_call(
        functools.partial(_sb_attn_kernel, tq=tq, scale=dh ** -0.5),
        grid=(batch, h),
        in_specs=[spec(0), spec(h), spec(2 * h)],
        out_specs=pl.BlockSpec((None, seq, dh), lambda b, hh: (hh, b, 0)),
        out_shape=jax.ShapeDtypeStruct((h, m, dh), BF16),
        compiler_params=_params("parallel", "parallel"),
        name="sb_attn",
    )(p, p, p)


def _ssd_kernel(x_ref, b_ref, c_ref, z_ref, dt_ref, wx_ref, wb_ref, wc_ref, bx_ref, bb_ref, bc_ref,
                dtb_ref, alog_ref, dskip_ref, norm_ref, y_ref, halo_scr, state_scr, *, hd, kconv):
    nxb, l, _ = x_ref.shape
    hg = dt_ref.shape[0]
    inner = nxb * LANE
    n_state = b_ref.shape[-1]
    per_blk = LANE // hd

    @pl.when(pl.program_id(2) == 0)
    def _():
        halo_scr[...] = jnp.zeros_like(halo_scr)
        state_scr[...] = jnp.zeros_like(state_scr)

    raw = jnp.concatenate([x_ref[c] for c in range(nxb)] + [b_ref[...], c_ref[...]], axis=1).astype(F32)
    xp = jnp.concatenate([halo_scr[...], raw], axis=0)
    halo_scr[...] = raw[l - 8:, :]
    wcat = jnp.concatenate([wx_ref[...], wb_ref[...], wc_ref[...]], axis=1)
    conv = jnp.concatenate([bx_ref[...], bb_ref[...], bc_ref[...]], axis=1)
    for k in range(kconv):
        off = 8 - (kconv - 1) + k
        conv = conv + wcat[k:k + 1, :] * xp[off:off + l, :]
    act = conv * jax.nn.sigmoid(conv)
    xs = act[:, :inner]
    bm = act[:, inner:inner + n_state]
    cm = act[:, inner + n_state:]

    dt = _softplus(dt_ref[...] + dtb_ref[:, :l])
    la = dt * (-jnp.exp(alog_ref[:, :l]))
    lane_t = lax.broadcasted_iota(jnp.int32, (hg, l), 1)
    cs = la
    sh = 1
    while sh < l:
        cs = cs + jnp.where(lane_t >= sh, pltpu.roll(cs, sh, axis=1), 0.0)
        sh *= 2
    pad = jnp.zeros((LANE - hg, l), F32)
    dt_col = jnp.concatenate([dt, pad], axis=0).T
    cs_col = jnp.concatenate([cs, pad], axis=0).T

    lane = lax.broadcasted_iota(jnp.int32, (l, LANE), 1)

    def expand(col_arr):
        blocks = []
        for blk in range(nxb):
            out = jnp.broadcast_to(col_arr[:, blk * per_blk:blk * per_blk + 1], (l, LANE))
            for r in range(1, per_blk):
                hh = blk * per_blk + r
                out = jnp.where(lane >= r * hd, jnp.broadcast_to(col_arr[:, hh:hh + 1], (l, LANE)), out)
            blocks.append(out)
        return jnp.concatenate(blocks, axis=1)

    dt_e = expand(dt_col)
    cs_e = expand(cs_col)
    cs_last = cs_e[l - 1:l, :]
    xdt = xs * dt_e

    bm16 = bm.astype(BF16)
    cm16 = cm.astype(BF16)
    cb = lax.dot_general(cm16, bm16, (((1,), (1,)), ((), ())), preferred_element_type=F32)
    ri = lax.broadcasted_iota(jnp.int32, (l, l), 0)
    ci = lax.broadcasted_iota(jnp.int32, (l, l), 1)
    tril = ci <= ri
    xdt16 = xdt.astype(BF16)
    y_blocks = []
    for blk in range(nxb):
        xblk = xdt16[:, blk * LANE:(blk + 1) * LANE]
        out = None
        for r in range(per_blk):
            hh = blk * per_blk + r
            seg = cs_col[:, hh:hh + 1] - cs[hh:hh + 1, :]
            dec = jnp.exp(jnp.where(tril, seg, -jnp.inf))
            yh = jnp.dot((cb * dec).astype(BF16), xblk, preferred_element_type=F32)
            out = yh if out is None else jnp.where(lane >= r * hd, yh, out)
        y_blocks.append(out)
    y = jnp.concatenate(y_blocks, axis=1)

    state = state_scr[...]
    y = y + jnp.dot(cm16, state.astype(BF16), preferred_element_type=F32) * jnp.exp(cs_e)
    xw = (xdt * jnp.exp(cs_last - cs_e)).astype(BF16)
    state_scr[...] = state * jnp.exp(cs_last) + jnp.dot(bm.T.astype(BF16), xw, preferred_element_type=F32)

    y = y + xs * dskip_ref[...]
    zf = jnp.concatenate([z_ref[c] for c in range(nxb)], axis=1).astype(F32)
    y = y * (zf * jax.nn.sigmoid(zf))
    ms = jnp.mean(y * y, axis=-1, keepdims=True)
    y = y * lax.rsqrt(ms + EPS) * norm_ref[...]
    for c in range(nxb):
        y_ref[c] = y[:, c * LANE:(c + 1) * LANE].astype(y_ref.dtype)


def _ssd(p, dt_t, conv_w, conv_b, dtb, alog, dskip_e, norm_g, dims, batch, seq, chunk, blk_off):
    g = dims.ssd_groups
    inner_g = dims.ssd_inner // g
    nxb = inner_g // LANE
    hg = dims.ssd_heads // g
    n_state = dims.ssd_state
    nsb = n_state // LANE
    assert nsb == 1
    m = batch * seq
    nt = seq // chunk
    kconv = conv_w.shape[0]
    z_off, x_off, b_off, c_off = blk_off
    tok = lambda b, gg, t: b * nt + t
    in_specs = [
        pl.BlockSpec((nxb, chunk, LANE), lambda b, gg, t: (x_off // nxb + gg, tok(b, gg, t), 0)),
        pl.BlockSpec((None, chunk, LANE), lambda b, gg, t: (b_off + gg, tok(b, gg, t), 0)),
        pl.BlockSpec((None, chunk, LANE), lambda b, gg, t: (c_off + gg, tok(b, gg, t), 0)),
        pl.BlockSpec((nxb, chunk, LANE), lambda b, gg, t: (z_off // nxb + gg, tok(b, gg, t), 0)),
        pl.BlockSpec((hg, chunk), lambda b, gg, t: (gg, tok(b, gg, t))),
        pl.BlockSpec((kconv, inner_g), lambda b, gg, t: (0, gg)),
        pl.BlockSpec((kconv, n_state), lambda b, gg, t: (0, dims.ssd_inner // n_state + gg)),
        pl.BlockSpec((kconv, n_state), lambda b, gg, t: (0, dims.ssd_inner // n_state + g + gg)),
        pl.BlockSpec((1, inner_g), lambda b, gg, t: (0, gg)),
        pl.BlockSpec((1, n_state), lambda b, gg, t: (0, dims.ssd_inner // n_state + gg)),
        pl.BlockSpec((1, n_state), lambda b, gg, t: (0, dims.ssd_inner // n_state + g + gg)),
        pl.BlockSpec((hg, LANE), lambda b, gg, t: (gg, 0)),
        pl.BlockSpec((hg, LANE), lambda b, gg, t: (gg, 0)),
        pl.BlockSpec((1, inner_g), lambda b, gg, t: (0, gg)),
        pl.BlockSpec((1, inner_g), lambda b, gg, t: (0, gg)),
    ]
    return pl.pallas_call(
        functools.partial(_ssd_kernel, hd=dims.ssd_head_dim, kconv=kconv),
        grid=(batch, g, nt),
        in_specs=in_specs,
        out_specs=pl.BlockSpec((nxb, chunk, LANE), lambda b, gg, t: (gg, tok(b, gg, t), 0)),
        out_shape=jax.ShapeDtypeStruct((dims.ssd_inner // LANE, m, LANE), BF16),
        scratch_shapes=[pltpu.VMEM((8, inner_g + 2 * n_state), F32),
                        pltpu.VMEM((n_state, inner_g), F32)],
        compiler_params=_params("parallel", "parallel", "arbitrary"),
        name="ssd",
    )(p, p, p, p, dt_t, conv_w, conv_w, conv_w, conv_b, conv_b, conv_b, dtb, alog, dskip_e, norm_g)


def _merge_kernel(o_ref, y_ref, gsb_ref, gssd_ref, wsb_ref, wssd_ref, out_ref, o_scr, y_scr):
    @pl.when(pl.program_id(1) == 0)
    def _():
        for c in range(o_ref.shape[0]):
            o_scr[:, c * LANE:(c + 1) * LANE] = o_ref[c]
        for c in range(y_ref.shape[0]):
            y_scr[:, c * LANE:(c + 1) * LANE] = y_ref[c]

    a = jnp.dot(o_scr[...], wsb_ref[...], preferred_element_type=F32)
    b = jnp.dot(y_scr[...], wssd_ref[...], preferred_element_type=F32)
    gsb = jnp.concatenate([gsb_ref[c] for c in range(gsb_ref.shape[0])], axis=1).astype(F32)
    gssd = jnp.concatenate([gssd_ref[c] for c in range(gssd_ref.shape[0])], axis=1).astype(F32)
    out_ref[...] = (jax.nn.sigmoid(gsb) * a + jax.nn.sigmoid(gssd) * b).astype(out_ref.dtype)


def _merge(o_sb, y, p, w_sb, w_ssd, gsb_off, gssd_off, tm, tn):
    nob, m, _ = o_sb.shape
    nyb = y.shape[0]
    d = w_sb.shape[1]
    nb = tn // LANE
    return pl.pallas_call(
        _merge_kernel,
        grid=(m // tm, d // tn),
        in_specs=[
            pl.BlockSpec((nob, tm, LANE), lambda i, j: (0, i, 0)),
            pl.BlockSpec((nyb, tm, LANE), lambda i, j: (0, i, 0)),
            pl.BlockSpec((nb, tm, LANE), lambda i, j: (gsb_off // nb + j, i, 0)),
            pl.BlockSpec((nb, tm, LANE), lambda i, j: (gssd_off // nb + j, i, 0)),
            pl.BlockSpec((nob * LANE, tn), lambda i, j: (0, j)),
            pl.BlockSpec((nyb * LANE, tn), lambda i, j: (0, j)),
        ],
        out_specs=pl.BlockSpec((tm, tn), lambda i, j: (i, j)),
        out_shape=jax.ShapeDtypeStruct((m, d), BF16),
        scratch_shapes=[pltpu.VMEM((tm, nob * LANE), BF16), pltpu.VMEM((tm, nyb * LANE), BF16)],
        compiler_params=_params("parallel", "arbitrary"),
        name="merge",
    )(o_sb, y, p, p, w_sb, w_ssd)


def _proj_norm_res_kernel(a_ref, w_ref, g_ref, x_ref, out_ref, acc_scr):
    k = pl.program_id(1)
    part = jnp.dot(a_ref[...], w_ref[...], preferred_element_type=F32)

    @pl.when(k == 0)
    def _():
        acc_scr[...] = part

    @pl.when(k > 0)
    def _():
        acc_scr[...] += part

    @pl.when(k == pl.num_programs(1) - 1)
    def _():
        f = acc_scr[...]
        ms = jnp.mean(f * f, axis=-1, keepdims=True)
        out_ref[...] = x_ref[...] + f * lax.rsqrt(ms + EPS) * g_ref[...]


def _proj_norm_res(a, w, gain, x2, tm, tk, name):
    m, kdim = a.shape
    d = w.shape[1]
    return pl.pallas_call(
        _proj_norm_res_kernel,
        grid=(m // tm, kdim // tk),
        in_specs=[
            pl.BlockSpec((tm, tk), lambda i, k: (i, k)),
            pl.BlockSpec((tk, d), lambda i, k: (k, 0)),
            pl.BlockSpec((1, d), lambda i, k: (0, 0)),
            pl.BlockSpec((tm, d), lambda i, k: (i, 0)),
        ],
        out_specs=pl.BlockSpec((tm, d), lambda i, k: (i, 0)),
        out_shape=jax.ShapeDtypeStruct((m, d), F32),
        scratch_shapes=[pltpu.VMEM((tm, d), F32)],
        compiler_params=_params("parallel", "arbitrary"),
        name=name,
    )(a, w, gain, x2)


FFN_HALO = 16


def _ffn_up_kernel(x_ref, halo_ref, g_ref, wg_ref, wv_ref, cwg_ref, cwv_ref, cbg_ref, cbv_ref, out_ref, h_scr,
                   *, tiles_per_seq, kconv):
    tm = x_ref.shape[0]

    def norm(v):
        ms = jnp.mean(v * v, axis=-1, keepdims=True)
        return (v * lax.rsqrt(ms + EPS) * g_ref[...]).astype(BF16)

    @pl.when(pl.program_id(1) == 0)
    def _():
        seq_start = (pl.program_id(0) % tiles_per_seq) == 0
        h_scr[:FFN_HALO, :] = jnp.where(seq_start, jnp.zeros((), BF16), norm(halo_ref[...]))
        h_scr[FFN_HALO:, :] = norm(x_ref[...])

    h = h_scr[...]

    def conv(w_ref, cw_ref, cb_ref):
        up = jnp.dot(h, w_ref[...], preferred_element_type=F32)
        u = cb_ref[...]
        for k in range(kconv):
            off = FFN_HALO - (kconv - 1) + k
            u = u + cw_ref[k:k + 1, :] * up[off:off + tm, :]
        return u

    gate = conv(wg_ref, cwg_ref, cbg_ref)
    val = conv(wv_ref, cwv_ref, cbv_ref)
    out_ref[...] = (jax.nn.gelu(gate, approximate=True) * val).astype(out_ref.dtype)


def _ffn_up(x1, gain, w_up, conv_w, conv_b, seq, tm, tn):
    m, d = x1.shape
    dff = w_up.shape[1] // 2
    nj = dff // tn
    kconv = conv_w.shape[0]
    hb = tm // FFN_HALO
    return pl.pallas_call(
        functools.partial(_ffn_up_kernel, tiles_per_seq=seq // tm, kconv=kconv),
        grid=(m // tm, nj),
        in_specs=[
            pl.BlockSpec((tm, d), lambda i, j: (i, 0)),
            pl.BlockSpec((FFN_HALO, d), lambda i, j: (jnp.maximum(i * hb - 1, 0), 0)),
            pl.BlockSpec((1, d), lambda i, j: (0, 0)),
            pl.BlockSpec((d, tn), lambda i, j: (0, j)),
            pl.BlockSpec((d, tn), lambda i, j: (0, nj + j)),
            pl.BlockSpec((kconv, tn), lambda i, j: (0, j)),
            pl.BlockSpec((kconv, tn), lambda i, j: (0, nj + j)),
            pl.BlockSpec((1, tn), lambda i, j: (0, j)),
            pl.BlockSpec((1, tn), lambda i, j: (0, nj + j)),
        ],
        out_specs=pl.BlockSpec((tm, tn), lambda i, j: (i, j)),
        out_shape=jax.ShapeDtypeStruct((m, dff), BF16),
        scratch_shapes=[pltpu.VMEM((FFN_HALO + tm, d), BF16)],
        compiler_params=_params("parallel", "arbitrary"),
        name="ffn_up",
    )(x1, x1, gain, w_up, w_up, conv_w, conv_w, conv_b, conv_b)


def _largest_tile(total, want, quantum=LANE):
    t = min(want, total)
    while total % t or t % quantum:
        t -= quantum
    return t


def _block(x, norm_mix_pre, w_in, ssd_conv_w, ssd_conv_b, dt_bias, a_log, d_skip, ssd_norm, w_sb_proj,
           w_ssd_proj, w_out, norm_mix_post, norm_ffn_pre, w_up, ffn_conv_w, ffn_conv_b, w_down, norm_ffn_post,
           dims, tiles):
    batch, seq, d = x.shape
    m = batch * seq
    sbw, inner, xbc, heads = dims.sb_width, dims.ssd_inner, dims.ssd_xbc, dims.ssd_heads
    x2 = x.reshape(m, d)
    row = lambda v: v.reshape(1, -1)

    dt_lo = 3 * sbw + inner + xbc
    w_main = jnp.concatenate([w_in[:, :dt_lo], w_in[:, dt_lo + heads:]], axis=1).astype(BF16)
    w_dt_t = w_in[:, dt_lo:dt_lo + heads].T.astype(BF16)
    z_off = 3 * sbw // LANE
    x_off = z_off + inner // LANE
    b_off = x_off + inner // LANE
    c_off = b_off + dims.ssd_groups * dims.ssd_state // LANE
    gsb_off = c_off + dims.ssd_groups * dims.ssd_state // LANE
    gssd_off = gsb_off + d // LANE

    tm = _largest_tile(m, tiles["tm"], 16)
    p, dt_t = _in_proj(x2, row(norm_mix_pre), w_main, w_dt_t, tm, _largest_tile(w_main.shape[1], tiles["tn_in"]))

    o_sb = _sb_attn(p, dims, batch, seq, min(tiles["tq"], seq))

    chunk = min(tiles["chunk"], seq)
    bcast = lambda v: jnp.broadcast_to(v.reshape(-1, 1), (heads, LANE))
    y = _ssd(p, dt_t, ssd_conv_w, row(ssd_conv_b), bcast(dt_bias), bcast(a_log),
             row(jnp.repeat(d_skip, dims.ssd_head_dim)), row(ssd_norm), dims, batch, seq, chunk,
             (z_off, x_off, b_off, c_off))

    merged = _merge(o_sb, y, p, w_sb_proj.astype(BF16), w_ssd_proj.astype(BF16), gsb_off, gssd_off,
                    _largest_tile(m, tiles["tm_merge"], 16), _largest_tile(d, tiles["tn_merge"]))
    tm2 = _largest_tile(m, tiles["tm_res"], 16)
    x1 = _proj_norm_res(merged, w_out.astype(BF16), row(norm_mix_post), x2, tm2, d, "out_proj")

    tm_up = _largest_tile(seq, tiles["tm"], 16)
    act = _ffn_up(x1, row(norm_ffn_pre), w_up.astype(BF16), ffn_conv_w, row(ffn_conv_b), seq, tm_up,
                  _largest_tile(dims.d_ff, tiles["tn_up"]))
    out = _proj_norm_res(act, w_down.astype(BF16), row(norm_ffn_post), x1, tm2,
                         _largest_tile(dims.d_ff, tiles["tk_down"]), "ffn_down")
    return out.reshape(batch, seq, d)


TILES = dict(tm=1024, tn_in=1024, tq=128, chunk=128, tm_merge=512, tn_merge=512, tm_res=512, tn_up=512, tk_down=1408)


def kernel(x, norm_mix_pre, w_in, ssd_conv_w, ssd_conv_b, dt_bias, a_log, d_skip, ssd_norm, w_sb_proj, w_ssd_proj,
           w_out, norm_mix_post, norm_ffn_pre, w_up, ffn_conv_w, ffn_conv_b, w_down, norm_ffn_post):
    dims = Dims()
    args = (norm_mix_pre, w_in, ssd_conv_w, ssd_conv_b, dt_bias, a_log, d_skip, ssd_norm, w_sb_proj, w_ssd_proj,
            w_out, norm_mix_post, norm_ffn_pre, w_up, ffn_conv_w, ffn_conv_b, w_down, norm_ffn_post)
    for layer in range(w_in.shape[0]):
        x = _block(x, *(a[layer] for a in args), dims, TILES)
    return x
```

```python
import dataclasses
import functools

import jax
import jax.numpy as jnp
from jax import lax
from jax.experimental import pallas as pl
from jax.experimental.pallas import tpu as pltpu

F32 = jnp.float32
BF16 = jnp.bfloat16
LANE = 128
EPS = 1e-6


@dataclasses.dataclass(frozen=True)
class Dims:
    d_model: int = 2048
    sb_heads: int = 16
    sb_head_dim: int = 128
    ssd_inner: int = 4096
    ssd_head_dim: int = 64
    ssd_groups: int = 8
    ssd_state: int = 128
    ssd_conv: int = 4
    d_ff: int = 5632
    ffn_conv: int = 3

    @property
    def sb_width(self):
        return self.sb_heads * self.sb_head_dim

    @property
    def ssd_heads(self):
        return self.ssd_inner // self.ssd_head_dim

    @property
    def ssd_xbc(self):
        return self.ssd_inner + 2 * self.ssd_groups * self.ssd_state


SB_LOG_WEIGHT_FLOOR = -110.0
VMEM_LIMIT = 56 * 1024 * 1024


def _softplus(x):
    return jnp.maximum(x, 0.0) + jnp.log1p(jnp.exp(-jnp.abs(x)))


def _params(*sem):
    return pltpu.CompilerParams(dimension_semantics=sem, vmem_limit_bytes=VMEM_LIMIT)


def _in_proj_kernel(x_ref, g_ref, w_ref, wdt_ref, p_ref, dt_ref, h_scr):
    @pl.when(pl.program_id(1) == 0)
    def _():
        x = x_ref[...]
        ms = jnp.mean(x * x, axis=-1, keepdims=True)
        h = (x * lax.rsqrt(ms + EPS) * g_ref[...]).astype(BF16)
        h_scr[...] = h
        dt_ref[...] = lax.dot_general(wdt_ref[...], h, (((1,), (1,)), ((), ())),
                                      preferred_element_type=F32)

    acc = jnp.dot(h_scr[...], w_ref[...], preferred_element_type=F32)
    for c in range(p_ref.shape[0]):
        p_ref[c] = acc[:, c * LANE:(c + 1) * LANE].astype(BF16)


def _in_proj(x2, gain, w_main, w_dt_t, tm, tn):
    m, d = x2.shape
    n = w_main.shape[1]
    hd = w_dt_t.shape[0]
    return pl.pallas_call(
        _in_proj_kernel,
        grid=(m // tm, n // tn),
        in_specs=[
            pl.BlockSpec((tm, d), lambda i, j: (i, 0)),
            pl.BlockSpec((1, d), lambda i, j: (0, 0)),
            pl.BlockSpec((d, tn), lambda i, j: (0, j)),
            pl.BlockSpec((hd, d), lambda i, j: (0, 0)),
        ],
        out_specs=[
            pl.BlockSpec((tn // LANE, tm, LANE), lambda i, j: (j, i, 0)),
            pl.BlockSpec((hd, tm), lambda i, j: (0, i)),
        ],
        out_shape=[
            jax.ShapeDtypeStruct((n // LANE, m, LANE), BF16),
            jax.ShapeDtypeStruct((hd, m), F32),
        ],
        scratch_shapes=[pltpu.VMEM((tm, d), BF16)],
        compiler_params=_params("parallel", "arbitrary"),
        name="in_proj",
    )(x2, gain, w_main, w_dt_t)


def _sb_attn_kernel(q_ref, k_ref, v_ref, o_ref, *, tq, nchain):
    s, dh = q_ref.shape
    ngroups = s // (tq * nchain)
    row = lax.broadcasted_iota(jnp.int32, (tq, tq), 0)
    col = lax.broadcasted_iota(jnp.int32, (tq, tq), 1)
    causal = col < row
    suffix = jnp.where(row > col, 1.0, 0.0).astype(BF16)
    half = jnp.concatenate([suffix, jnp.ones((tq, tq), BF16)], axis=1)
    sum_rhs = jnp.concatenate([half, half], axis=0)

    def key_blocks(qbs, js, carries, accs, masked):
        starts = [pl.multiple_of(j * tq, tq) for j in js]
        zs = [lax.dot_general(qb, k_ref[pl.ds(st, tq), :], (((1,), (1,)), ((), ())), preferred_element_type=F32)
              for qb, st in zip(qbs, starts)]
        sps, parts = [], []
        for z in zs:
            sp = jnp.maximum(z, 0.0) + jnp.log(1.0 + jnp.exp(-jnp.abs(z)))
            if masked:
                sp = jnp.where(causal, sp, 0.0)
            hi = sp.astype(BF16)
            lo = (sp - hi.astype(F32)).astype(BF16)
            sps.append(sp)
            parts.append(jnp.concatenate([hi, lo], axis=1))
        sums = jnp.dot(jnp.concatenate(parts, axis=0), sum_rhs, preferred_element_type=F32)
        ws = []
        for c, (z, sp) in enumerate(zip(zs, sps)):
            w = jnp.exp(z - sp - sums[c * tq:(c + 1) * tq, :tq] - carries[c])
            if masked:
                w = jnp.where(causal, w, 0.0)
            ws.append(w.astype(BF16))
        new_a = [acc + jnp.dot(w, v_ref[pl.ds(st, tq), :], preferred_element_type=F32)
                 for acc, w, st in zip(accs, ws, starts)]
        new_c = [carries[c] + sums[c * tq:(c + 1) * tq, tq:] for c in range(len(js))]
        return new_c, new_a

    def q_group(gi, _):
        qi = [gi * nchain + c for c in range(nchain)]
        qstart = [pl.multiple_of(i * tq, tq) for i in qi]
        qb = [q_ref[pl.ds(st, tq), :] for st in qstart]
        carries, accs = key_blocks(qb, qi, [jnp.zeros((tq, tq), F32)] * nchain,
                                   [jnp.zeros((tq, dh), F32)] * nchain, True)

        def retire(t, carries):
            carries = [jnp.where(qi[c] - t >= 0, carries[c], jnp.inf) for c in range(nchain)]
            return carries, jnp.min(functools.reduce(jnp.minimum, carries))

        def cond(st):
            return st[1] < -SB_LOG_WEIGHT_FLOOR

        def body(st):
            t, _, carries, accs = st
            js = [jnp.maximum(qi[c] - t, 0) for c in range(nchain)]
            carries, accs = key_blocks(qb, js, carries, accs, False)
            carries, low = retire(t + 1, carries)
            return t + 1, low, carries, accs

        carries, low = retire(1, carries)
        _, _, _, accs = lax.while_loop(cond, body, (1, low, carries, accs))
        for c in range(nchain):
            o_ref[pl.ds(qstart[c], tq), :] = accs[c].astype(o_ref.dtype)
        return 0

    lax.fori_loop(0, ngroups, q_group, 0)


def _sb_attn(p, dims, batch, seq, tq, nchain):
    h, dh = dims.sb_heads, dims.sb_head_dim
    m = batch * seq
    spec = lambda off: pl.BlockSpec((None, seq, dh), lambda b, hh: (off + hh, b, 0))
    return pl.pallas_call(
        functools.partial(_sb_attn_kernel, tq=tq, nchain=nchain),
        grid=(batch, h),
        in_specs=[spec(0), spec(h), spec(2 * h)],
        out_specs=pl.BlockSpec((None, seq, dh), lambda b, hh: (hh, b, 0)),
        out_shape=jax.ShapeDtypeStruct((h, m, dh), BF16),
        compiler_params=_params("parallel", "parallel"),
        name="sb_attn",
    )(p, p, p)


def _ssd_kernel(x_ref, b_ref, c_ref, z_ref, dt_ref, wx_ref, wb_ref, wc_ref, bx_ref, bb_ref, bc_ref,
                dtb_ref, alog_ref, dskip_ref, norm_ref, y_ref, halo_scr, state_scr, *, hd, kconv):
    nxb, l, _ = x_ref.shape
    hg = dt_ref.shape[0]
    inner = nxb * LANE
    n_state = b_ref.shape[-1]
    per_blk = LANE // hd

    @pl.when(pl.program_id(2) == 0)
    def _():
        halo_scr[...] = jnp.zeros_like(halo_scr)
        state_scr[...] = jnp.zeros_like(state_scr)

    raw = jnp.concatenate([x_ref[c] for c in range(nxb)] + [b_ref[...], c_ref[...]], axis=1).astype(F32)
    xp = jnp.concatenate([halo_scr[...], raw], axis=0)
    halo_scr[...] = raw[l - 8:, :]
    wcat = jnp.concatenate([wx_ref[...], wb_ref[...], wc_ref[...]], axis=1)
    conv = jnp.concatenate([bx_ref[...], bb_ref[...], bc_ref[...]], axis=1)
    for k in range(kconv):
        off = 8 - (kconv - 1) + k
        conv = conv + wcat[k:k + 1, :] * xp[off:off + l, :]
    act = conv * jax.nn.sigmoid(conv)
    xs = act[:, :inner]
    bm = act[:, inner:inner + n_state]
    cm = act[:, inner + n_state:]

    dt = _softplus(dt_ref[...] + dtb_ref[:, :l])
    la = dt * (-jnp.exp(alog_ref[:, :l]))
    lane_t = lax.broadcasted_iota(jnp.int32, (hg, l), 1)
    cs = la
    sh = 1
    while sh < l:
        cs = cs + jnp.where(lane_t >= sh, pltpu.roll(cs, sh, axis=1), 0.0)
        sh *= 2
    pad = jnp.zeros((LANE - hg, l), F32)
    dt_col = jnp.concatenate([dt, pad], axis=0).T
    cs_col = jnp.concatenate([cs, pad], axis=0).T

    lane = lax.broadcasted_iota(jnp.int32, (l, LANE), 1)

    def expand(col_arr):
        blocks = []
        for blk in range(nxb):
            out = jnp.broadcast_to(col_arr[:, blk * per_blk:blk * per_blk + 1], (l, LANE))
            for r in range(1, per_blk):
                hh = blk * per_blk + r
                out = jnp.where(lane >= r * hd, jnp.broadcast_to(col_arr[:, hh:hh + 1], (l, LANE)), out)
            blocks.append(out)
        return jnp.concatenate(blocks, axis=1)

    dt_e = expand(dt_col)
    cs_e = expand(cs_col)
    cs_last = cs_e[l - 1:l, :]
    xdt = xs * dt_e

    bm16 = bm.astype(BF16)
    cm16 = cm.astype(BF16)
    cb = lax.dot_general(cm16, bm16, (((1,), (1,)), ((), ())), preferred_element_type=F32)
    ri = lax.broadcasted_iota(jnp.int32, (l, l), 0)
    ci = lax.broadcasted_iota(jnp.int32, (l, l), 1)
    tril = ci <= ri
    xdt16 = xdt.astype(BF16)
    y_blocks = []
    for blk in range(nxb):
        xblk = xdt16[:, blk * LANE:(blk + 1) * LANE]
        out = None
        for r in range(per_blk):
            hh = blk * per_blk + r
            seg = cs_col[:, hh:hh + 1] - cs[hh:hh + 1, :]
            dec = jnp.exp(jnp.where(tril, seg, -jnp.inf))
            yh = jnp.dot((cb * dec).astype(BF16), xblk, preferred_element_type=F32)
            out = yh if out is None else jnp.where(lane >= r * hd, yh, out)
        y_blocks.append(out)
    y = jnp.concatenate(y_blocks, axis=1)

    state = state_scr[...]
    y = y + jnp.dot(cm16, state.astype(BF16), preferred_element_type=F32) * jnp.exp(cs_e)
    xw = (xdt * jnp.exp(cs_last - cs_e)).astype(BF16)
    state_scr[...] = state * jnp.exp(cs_last) + jnp.dot(bm.T.astype(BF16), xw, preferred_element_type=F32)

    y = y + xs * dskip_ref[...]
    zf = jnp.concatenate([z_ref[c] for c in range(nxb)], axis=1).astype(F32)
    y = y * (zf * jax.nn.sigmoid(zf))
    ms = jnp.mean(y * y, axis=-1, keepdims=True)
    y = y * lax.rsqrt(ms + EPS) * norm_ref[...]
    for c in range(nxb):
        y_ref[c] = y[:, c * LANE:(c + 1) * LANE].astype(y_ref.dtype)


def _ssd(p, dt_t, conv_w, conv_b, dtb, alog, dskip_e, norm_g, dims, batch, seq, chunk, blk_off):
    g = dims.ssd_groups
    inner_g = dims.ssd_inner // g
    nxb = inner_g // LANE
    hg = dims.ssd_heads // g
    n_state = dims.ssd_state
    nsb = n_state // LANE
    assert nsb == 1
    m = batch * seq
    nt = seq // chunk
    kconv = conv_w.shape[0]
    z_off, x_off, b_off, c_off = blk_off
    tok = lambda b, gg, t: b * nt + t
    in_specs = [
        pl.BlockSpec((nxb, chunk, LANE), lambda b, gg, t: (x_off // nxb + gg, tok(b, gg, t), 0)),
        pl.BlockSpec((None, chunk, LANE), lambda b, gg, t: (b_off + gg, tok(b, gg, t), 0)),
        pl.BlockSpec((None, chunk, LANE), lambda b, gg, t: (c_off + gg, tok(b, gg, t), 0)),
        pl.BlockSpec((nxb, chunk, LANE), lambda b, gg, t: (z_off // nxb + gg, tok(b, gg, t), 0)),
        pl.BlockSpec((hg, chunk), lambda b, gg, t: (gg, tok(b, gg, t))),
        pl.BlockSpec((kconv, inner_g), lambda b, gg, t: (0, gg)),
        pl.BlockSpec((kconv, n_state), lambda b, gg, t: (0, dims.ssd_inner // n_state + gg)),
        pl.BlockSpec((kconv, n_state), lambda b, gg, t: (0, dims.ssd_inner // n_state + g + gg)),
        pl.BlockSpec((1, inner_g), lambda b, gg, t: (0, gg)),
        pl.BlockSpec((1, n_state), lambda b, gg, t: (0, dims.ssd_inner // n_state + gg)),
        pl.BlockSpec((1, n_state), lambda b, gg, t: (0, dims.ssd_inner // n_state + g + gg)),
        pl.BlockSpec((hg, LANE), lambda b, gg, t: (gg, 0)),
        pl.BlockSpec((hg, LANE), lambda b, gg, t: (gg, 0)),
        pl.BlockSpec((1, inner_g), lambda b, gg, t: (0, gg)),
        pl.BlockSpec((1, inner_g), lambda b, gg, t: (0, gg)),
    ]
    return pl.pallas_call(
        functools.partial(_ssd_kernel, hd=dims.ssd_head_dim, kconv=kconv),
        grid=(batch, g, nt),
        in_specs=in_specs,
        out_specs=pl.BlockSpec((nxb, chunk, LANE), lambda b, gg, t: (gg, tok(b, gg, t), 0)),
        out_shape=jax.ShapeDtypeStruct((dims.ssd_inner // LANE, m, LANE), BF16),
        scratch_shapes=[pltpu.VMEM((8, inner_g + 2 * n_state), F32),
                        pltpu.VMEM((n_state, inner_g), F32)],
        compiler_params=_params("parallel", "parallel", "arbitrary"),
        name="ssd",
    )(p, p, p, p, dt_t, conv_w, conv_w, conv_w, conv_b, conv_b, conv_b, dtb, alog, dskip_e, norm_g)


def _merge_kernel(o_ref, y_ref, gsb_ref, gssd_ref, wsb_ref, wssd_ref, out_ref, o_scr, y_scr):
    @pl.when(pl.program_id(1) == 0)
    def _():
        for c in range(o_ref.shape[0]):
            o_scr[:, c * LANE:(c + 1) * LANE] = o_ref[c]
        for c in range(y_ref.shape[0]):
            y_scr[:, c * LANE:(c + 1) * LANE] = y_ref[c]

    a = jnp.dot(o_scr[...], wsb_ref[...], preferred_element_type=F32)
    b = jnp.dot(y_scr[...], wssd_ref[...], preferred_element_type=F32)
    gsb = jnp.concatenate([gsb_ref[c] for c in range(gsb_ref.shape[0])], axis=1).astype(F32)
    gssd = jnp.concatenate([gssd_ref[c] for c in range(gssd_ref.shape[0])], axis=1).astype(F32)
    out_ref[...] = (jax.nn.sigmoid(gsb) * a + jax.nn.sigmoid(gssd) * b).astype(out_ref.dtype)


def _merge(o_sb, y, p, w_sb, w_ssd, gsb_off, gssd_off, tm, tn):
    nob, m, _ = o_sb.shape
    nyb = y.shape[0]
    d = w_sb.shape[1]
    nb = tn // LANE
    return pl.pallas_call(
        _merge_kernel,
        grid=(m // tm, d // tn),
        in_specs=[
            pl.BlockSpec((nob, tm, LANE), lambda i, j: (0, i, 0)),
            pl.BlockSpec((nyb, tm, LANE), lambda i, j: (0, i, 0)),
            pl.BlockSpec((nb, tm, LANE), lambda i, j: (gsb_off // nb + j, i, 0)),
            pl.BlockSpec((nb, tm, LANE), lambda i, j: (gssd_off // nb + j, i, 0)),
            pl.BlockSpec((nob * LANE, tn), lambda i, j: (0, j)),
            pl.BlockSpec((nyb * LANE, tn), lambda i, j: (0, j)),
        ],
        out_specs=pl.BlockSpec((tm, tn), lambda i, j: (i, j)),
        out_shape=jax.ShapeDtypeStruct((m, d), BF16),
        scratch_shapes=[pltpu.VMEM((tm, nob * LANE), BF16), pltpu.VMEM((tm, nyb * LANE), BF16)],
        compiler_params=_params("parallel", "arbitrary"),
        name="merge",
    )(o_sb, y, p, p, w_sb, w_ssd)


def _proj_norm_res_kernel(a_ref, w_ref, g_ref, x_ref, out_ref, acc_scr):
    k = pl.program_id(1)
    part = jnp.dot(a_ref[...], w_ref[...], preferred_element_type=F32)

    @pl.when(k == 0)
    def _():
        acc_scr[...] = part

    @pl.when(k > 0)
    def _():
        acc_scr[...] += part

    @pl.when(k == pl.num_programs(1) - 1)
    def _():
        f = acc_scr[...]
        ms = jnp.mean(f * f, axis=-1, keepdims=True)
        out_ref[...] = x_ref[...] + f * lax.rsqrt(ms + EPS) * g_ref[...]


def _proj_norm_res(a, w, gain, x2, tm, tk, name):
    m, kdim = a.shape
    d = w.shape[1]
    return pl.pallas_call(
        _proj_norm_res_kernel,
        grid=(m // tm, kdim // tk),
        in_specs=[
            pl.BlockSpec((tm, tk), lambda i, k: (i, k)),
            pl.BlockSpec((tk, d), lambda i, k: (k, 0)),
            pl.BlockSpec((1, d), lambda i, k: (0, 0)),
            pl.BlockSpec((tm, d), lambda i, k: (i, 0)),
        ],
        out_specs=pl.BlockSpec((tm, d), lambda i, k: (i, 0)),
        out_shape=jax.ShapeDtypeStruct((m, d), F32),
        scratch_shapes=[pltpu.VMEM((tm, d), F32)],
        compiler_params=_params("parallel", "arbitrary"),
        name=name,
    )(a, w, gain, x2)


FFN_HALO = 16


def _ffn_up_kernel(x_ref, halo_ref, g_ref, wg_ref, wv_ref, cwg_ref, cwv_ref, cbg_ref, cbv_ref, out_ref, h_scr,
                   *, tiles_per_seq, kconv):
    tm = x_ref.shape[0]

    def norm(v):
        ms = jnp.mean(v * v, axis=-1, keepdims=True)
        return (v * lax.rsqrt(ms + EPS) * g_ref[...]).astype(BF16)

    @pl.when(pl.program_id(1) == 0)
    def _():
        seq_start = (pl.program_id(0) % tiles_per_seq) == 0
        h_scr[:FFN_HALO, :] = jnp.where(seq_start, jnp.zeros((), BF16), norm(halo_ref[...]))
        h_scr[FFN_HALO:, :] = norm(x_ref[...])

    h = h_scr[...]

    def conv(w_ref, cw_ref, cb_ref):
        up = jnp.dot(h, w_ref[...], preferred_element_type=F32)
        u = cb_ref[...]
        for k in range(kconv):
            off = FFN_HALO - (kconv - 1) + k
            u = u + cw_ref[k:k + 1, :] * up[off:off + tm, :]
        return u

    gate = conv(wg_ref, cwg_ref, cbg_ref)
    val = conv(wv_ref, cwv_ref, cbv_ref)
    out_ref[...] = (jax.nn.gelu(gate, approximate=True) * val).astype(out_ref.dtype)


def _ffn_up(x1, gain, w_up, conv_w, conv_b, seq, tm, tn):
    m, d = x1.shape
    dff = w_up.shape[1] // 2
    nj = dff // tn
    kconv = conv_w.shape[0]
    hb = tm // FFN_HALO
    return pl.pallas_call(
        functools.partial(_ffn_up_kernel, tiles_per_seq=seq // tm, kconv=kconv),
        grid=(m // tm, nj),
        in_specs=[
            pl.BlockSpec((tm, d), lambda i, j: (i, 0)),
            pl.BlockSpec((FFN_HALO, d), lambda i, j: (jnp.maximum(i * hb - 1, 0), 0)),
            pl.BlockSpec((1, d), lambda i, j: (0, 0)),
            pl.BlockSpec((d, tn), lambda i, j: (0, j)),
            pl.BlockSpec((d, tn), lambda i, j: (0, nj + j)),
            pl.BlockSpec((kconv, tn), lambda i, j: (0, j)),
            pl.BlockSpec((kconv, tn), lambda i, j: (0, nj + j)),
            pl.BlockSpec((1, tn), lambda i, j: (0, j)),
            pl.BlockSpec((1, tn), lambda i, j: (0, nj + j)),
        ],
        out_specs=pl.BlockSpec((tm, tn), lambda i, j: (i, j)),
        out_shape=jax.ShapeDtypeStruct((m, dff), BF16),
        scratch_shapes=[pltpu.VMEM((FFN_HALO + tm, d), BF16)],
        compiler_params=_params("parallel", "arbitrary"),
        name="ffn_up",
    )(x1, x1, gain, w_up, w_up, conv_w, conv_w, conv_b, conv_b)


def _largest_tile(total, want, quantum=LANE):
    t = min(want, total)
    while total % t or t % quantum:
        t -= quantum
    return t


def _block(x, norm_mix_pre, w_in, ssd_conv_w, ssd_conv_b, dt_bias, a_log, d_skip, ssd_norm, w_sb_proj,
           w_ssd_proj, w_out, norm_mix_post, norm_ffn_pre, w_up, ffn_conv_w, ffn_conv_b, w_down, norm_ffn_post,
           dims, tiles):
    batch, seq, d = x.shape
    m = batch * seq
    sbw, inner, xbc, heads = dims.sb_width, dims.ssd_inner, dims.ssd_xbc, dims.ssd_heads
    x2 = x.reshape(m, d)
    row = lambda v: v.reshape(1, -1)

    dt_lo = 3 * sbw + inner + xbc
    w_main = jnp.concatenate([w_in[:, :sbw] * dims.sb_head_dim ** -0.5, w_in[:, sbw:dt_lo],
                              w_in[:, dt_lo + heads:]], axis=1).astype(BF16)
    w_dt_t = w_in[:, dt_lo:dt_lo + heads].T.astype(BF16)
    z_off = 3 * sbw // LANE
    x_off = z_off + inner // LANE
    b_off = x_off + inner // LANE
    c_off = b_off + dims.ssd_groups * dims.ssd_state // LANE
    gsb_off = c_off + dims.ssd_groups * dims.ssd_state // LANE
    gssd_off = gsb_off + d // LANE

    tm = _largest_tile(m, tiles["tm"], 16)
    p, dt_t = _in_proj(x2, row(norm_mix_pre), w_main, w_dt_t, tm, _largest_tile(w_main.shape[1], tiles["tn_in"]))

    o_sb = _sb_attn(p, dims, batch, seq, min(tiles["tq"], seq), tiles["sb_chains"])

    chunk = min(tiles["chunk"], seq)
    bcast = lambda v: jnp.broadcast_to(v.reshape(-1, 1), (heads, LANE))
    y = _ssd(p, dt_t, ssd_conv_w, row(ssd_conv_b), bcast(dt_bias), bcast(a_log),
             row(jnp.repeat(d_skip, dims.ssd_head_dim)), row(ssd_norm), dims, batch, seq, chunk,
             (z_off, x_off, b_off, c_off))

    merged = _merge(o_sb, y, p, w_sb_proj.astype(BF16), w_ssd_proj.astype(BF16), gsb_off, gssd_off,
                    _largest_tile(m, tiles["tm_merge"], 16), _largest_tile(d, tiles["tn_merge"]))
    tm2 = _largest_tile(m, tiles["tm_res"], 16)
    x1 = _proj_norm_res(merged, w_out.astype(BF16), row(norm_mix_post), x2, tm2, d, "out_proj")

    tm_up = _largest_tile(seq, tiles["tm"], 16)
    act = _ffn_up(x1, row(norm_ffn_pre), w_up.astype(BF16), ffn_conv_w, row(ffn_conv_b), seq, tm_up,
                  _largest_tile(dims.d_ff, tiles["tn_up"]))
    out = _proj_norm_res(act, w_down.astype(BF16), row(norm_ffn_post), x1, tm2,
                         _largest_tile(dims.d_ff, tiles["tk_down"]), "ffn_down")
    return out.reshape(batch, seq, d)


TILES = dict(tm=1024, tn_in=1024, tq=128, sb_chains=8, chunk=128, tm_merge=512, tn_merge=512, tm_res=512, tn_up=512, tk_down=1408)


def kernel(x, norm_mix_pre, w_in, ssd_conv_w, ssd_conv_b, dt_bias, a_log, d_skip, ssd_norm, w_sb_proj, w_ssd_proj,
           w_out, norm_mix_post, norm_ffn_pre, w_up, ffn_conv_w, ffn_conv_b, w_down, norm_ffn_post):
    dims = Dims()
    args = (norm_mix_pre, w_in, ssd_conv_w, ssd_conv_b, dt_bias, a_log, d_skip, ssd_norm, w_sb_proj, w_ssd_proj,
            w_out, norm_mix_post, norm_ffn_pre, w_up, ffn_conv_w, ffn_conv_b, w_down, norm_ffn_post)
    for layer in range(w_in.shape[0]):
        x = _block(x, *(a[layer] for a in args), dims, TILES)
    return x
```

```python
import dataclasses
import functools

import jax
import jax.numpy as jnp
from jax import lax
from jax.experimental import pallas as pl
from jax.experimental.pallas import tpu as pltpu

F32 = jnp.float32
BF16 = jnp.bfloat16
LANE = 128
EPS = 1e-6


@dataclasses.dataclass(frozen=True)
class Dims:
    d_model: int = 2048
    sb_heads: int = 16
    sb_head_dim: int = 128
    ssd_inner: int = 4096
    ssd_head_dim: int = 64
    ssd_groups: int = 8
    ssd_state: int = 128
    ssd_conv: int = 4
    d_ff: int = 5632
    ffn_conv: int = 3

    @property
    def sb_width(self):
        return self.sb_heads * self.sb_head_dim

    @property
    def ssd_heads(self):
        return self.ssd_inner // self.ssd_head_dim

    @property
    def ssd_xbc(self):
        return self.ssd_inner + 2 * self.ssd_groups * self.ssd_state


SB_LOG_WEIGHT_FLOOR = -110.0
VMEM_LIMIT = 56 * 1024 * 1024


def _softplus(x):
    return jnp.maximum(x, 0.0) + jnp.log1p(jnp.exp(-jnp.abs(x)))


def _silu(x):
    h = 0.5 * x
    return h + h * jnp.tanh(h)


def _params(*sem):
    return pltpu.CompilerParams(dimension_semantics=sem, vmem_limit_bytes=VMEM_LIMIT)


def _in_proj_kernel(x_ref, g_ref, w_ref, wdt_ref, p_ref, dt_ref, h_scr):
    @pl.when(pl.program_id(1) == 0)
    def _():
        x = x_ref[...]
        ms = jnp.mean(x * x, axis=-1, keepdims=True)
        h = (x * lax.rsqrt(ms + EPS) * g_ref[...]).astype(BF16)
        h_scr[...] = h
        dt_ref[...] = lax.dot_general(wdt_ref[...], h, (((1,), (1,)), ((), ())),
                                      preferred_element_type=F32)

    acc = jnp.dot(h_scr[...], w_ref[...], preferred_element_type=F32)
    for c in range(p_ref.shape[0]):
        p_ref[c] = acc[:, c * LANE:(c + 1) * LANE].astype(BF16)


def _in_proj(x2, gain, w_main, w_dt_t, seq, tm, tn):
    m, d = x2.shape
    n = w_main.shape[1]
    hd = w_dt_t.shape[0]
    tps = seq // tm
    return pl.pallas_call(
        _in_proj_kernel,
        grid=(m // tm, n // tn),
        in_specs=[
            pl.BlockSpec((tm, d), lambda i, j: (i, 0)),
            pl.BlockSpec((1, d), lambda i, j: (0, 0)),
            pl.BlockSpec((d, tn), lambda i, j: (0, j)),
            pl.BlockSpec((hd, d), lambda i, j: (0, 0)),
        ],
        out_specs=[
            pl.BlockSpec((tn // LANE, tm, LANE), lambda i, j: (j, i, 0)),
            pl.BlockSpec((None, hd, tm), lambda i, j: (i // tps, 0, i % tps)),
        ],
        out_shape=[
            jax.ShapeDtypeStruct((n // LANE, m, LANE), BF16),
            jax.ShapeDtypeStruct((m // seq, hd, seq), F32),
        ],
        scratch_shapes=[pltpu.VMEM((tm, d), BF16)],
        compiler_params=_params("parallel", "arbitrary"),
        name="in_proj",
    )(x2, gain, w_main, w_dt_t)


def _sb_attn_kernel(q_ref, k_ref, v_ref, o_ref, *, tq, nchain):
    s, dh = q_ref.shape
    ngroups = s // (tq * nchain)
    row = lax.broadcasted_iota(jnp.int32, (tq, tq), 0)
    col = lax.broadcasted_iota(jnp.int32, (tq, tq), 1)
    causal = col < row
    suffix = jnp.where(row > col, 1.0, 0.0).astype(BF16)
    half = jnp.concatenate([suffix, jnp.ones((tq, tq), BF16)], axis=1)
    sum_rhs = jnp.concatenate([half, half], axis=0)

    def key_blocks(qbs, js, carries, accs, masked):
        starts = [pl.multiple_of(j * tq, tq) for j in js]
        zs = [lax.dot_general(qb, k_ref[pl.ds(st, tq), :], (((1,), (1,)), ((), ())), preferred_element_type=F32)
              for qb, st in zip(qbs, starts)]
        sps, parts = [], []
        for z in zs:
            sp = jnp.maximum(z, 0.0) + jnp.log(1.0 + jnp.exp(-jnp.abs(z)))
            if masked:
                sp = jnp.where(causal, sp, 0.0)
            hi = sp.astype(BF16)
            lo = (sp - hi.astype(F32)).astype(BF16)
            sps.append(sp)
            parts.append(jnp.concatenate([hi, lo], axis=1))
        sums = jnp.dot(jnp.concatenate(parts, axis=0), sum_rhs, preferred_element_type=F32)
        ws = []
        for c, (z, sp) in enumerate(zip(zs, sps)):
            w = jnp.exp(z - sp - sums[c * tq:(c + 1) * tq, :tq] - carries[c])
            if masked:
                w = jnp.where(causal, w, 0.0)
            ws.append(w.astype(BF16))
        new_a = [acc + jnp.dot(w, v_ref[pl.ds(st, tq), :], preferred_element_type=F32)
                 for acc, w, st in zip(accs, ws, starts)]
        new_c = [carries[c] + sums[c * tq:(c + 1) * tq, tq:] for c in range(len(js))]
        return new_c, new_a

    def q_group(gi, _):
        qi = [gi * nchain + c for c in range(nchain)]
        qstart = [pl.multiple_of(i * tq, tq) for i in qi]
        qb = [q_ref[pl.ds(st, tq), :] for st in qstart]
        carries, accs = key_blocks(qb, qi, [jnp.zeros((tq, tq), F32)] * nchain,
                                   [jnp.zeros((tq, dh), F32)] * nchain, True)

        def retire(t, carries):
            carries = [jnp.where(qi[c] - t >= 0, carries[c], jnp.inf) for c in range(nchain)]
            return carries, jnp.min(functools.reduce(jnp.minimum, carries))

        def cond(st):
            return st[1] < -SB_LOG_WEIGHT_FLOOR

        def body(st):
            t, _, carries, accs = st
            js = [jnp.maximum(qi[c] - t, 0) for c in range(nchain)]
            carries, accs = key_blocks(qb, js, carries, accs, False)
            carries, low = retire(t + 1, carries)
            return t + 1, low, carries, accs

        carries, low = retire(1, carries)
        _, _, _, accs = lax.while_loop(cond, body, (1, low, carries, accs))
        for c in range(nchain):
            o_ref[pl.ds(qstart[c], tq), :] = accs[c].astype(o_ref.dtype)
        return 0

    lax.fori_loop(0, ngroups, q_group, 0)


def _sb_attn(p, dims, batch, seq, tq, nchain):
    h, dh = dims.sb_heads, dims.sb_head_dim
    m = batch * seq
    spec = lambda off: pl.BlockSpec((None, seq, dh), lambda b, hh: (off + hh, b, 0))
    return pl.pallas_call(
        functools.partial(_sb_attn_kernel, tq=tq, nchain=nchain),
        grid=(batch, h),
        in_specs=[spec(0), spec(h), spec(2 * h)],
        out_specs=pl.BlockSpec((None, seq, dh), lambda b, hh: (hh, b, 0)),
        out_shape=jax.ShapeDtypeStruct((h, m, dh), BF16),
        compiler_params=_params("parallel", "parallel"),
        name="sb_attn",
    )(p, p, p)


def _ssd_kernel(x_ref, b_ref, c_ref, z_ref, dt_ref, wx_ref, wb_ref, wc_ref, bx_ref, bb_ref, bc_ref,
                dtb_ref, alog_ref, dskip_ref, norm_ref, y_ref, halo_scr, state_scr, *, hd, kconv):
    nxb, nbatch, l, _ = x_ref.shape
    hg = dt_ref.shape[1]
    inner = nxb * LANE
    n_state = b_ref.shape[-1]
    assert l == LANE and 2 * hd == LANE and n_state == LANE

    @pl.when(pl.program_id(1) == 0)
    def _():
        halo_scr[...] = jnp.zeros_like(halo_scr)
        state_scr[...] = jnp.zeros_like(state_scr)

    def rows(v, n=l):
        return jnp.tile(jnp.broadcast_to(v, (8, v.shape[1])), (n // 8, 1))

    wcat = jnp.concatenate([wx_ref[...], wb_ref[...], wc_ref[...]], axis=1)
    taps = [rows(wcat[k:k + 1, :]) for k in range(kconv)]
    bcat = rows(jnp.concatenate([bx_ref[...], bb_ref[...], bc_ref[...]], axis=1))
    dskip = rows(dskip_ref[...])
    norm_g = rows(norm_ref[...])
    neg_a = -jnp.exp(alog_ref[...])
    lane = lax.broadcasted_iota(jnp.int32, (l, LANE), 1)
    lane_t = lax.broadcasted_iota(jnp.int32, (hg, l), 1)
    low_half = lane < hd
    tril = lane <= lax.broadcasted_iota(jnp.int32, (l, l), 0)
    pad = jnp.zeros((LANE - hg, l), F32)

    for b in range(nbatch):
        raw = jnp.concatenate([x_ref[c, b] for c in range(nxb)] + [b_ref[b], c_ref[b]], axis=1).astype(F32)
        xp = jnp.concatenate([halo_scr[b], raw], axis=0)
        halo_scr[b] = raw[l - 8:, :]
        conv = bcat
        for k in range(kconv):
            off = 8 - (kconv - 1) + k
            conv = conv + taps[k] * xp[off:off + l, :]
        act = _silu(conv)
        x16 = act[:, :inner].astype(BF16)
        bm = act[:, inner:inner + n_state]
        cm = act[:, inner + n_state:]

        dt = _softplus(dt_ref[b] + dtb_ref[...])
        cs = dt * neg_a
        sh = 1
        while sh < l:
            cs = cs + jnp.where(lane_t >= sh, pltpu.roll(cs, sh, axis=1), 0.0)
            sh *= 2
        cs_last = jnp.broadcast_to(cs[:, l - 1:l], (hg, l))
        src_row = cs - jnp.log(dt)
        coef_row = dt * jnp.exp(cs_last - cs)
        keep_row = jnp.exp(cs_last)
        cs_col = jnp.concatenate([cs, pad], axis=0).T

        cb = lax.dot_general(cm.astype(BF16), bm.astype(BF16), (((1,), (1,)), ((), ())),
                             preferred_element_type=F32)
        bm_t = bm.T
        state = state_scr[b]
        y_blocks, new_state = [], []
        for blk in range(nxb):
            lhs, scaled_bt = [], []
            for h in (2 * blk, 2 * blk + 1):
                ccol = jnp.broadcast_to(cs_col[:, h:h + 1], (l, LANE))
                seg = ccol - rows(src_row[h:h + 1, :])
                w_in = (cb * jnp.exp(jnp.where(tril, seg, -jnp.inf))).astype(BF16)
                w_prev = (cm * jnp.exp(ccol)).astype(BF16)
                lhs.append(jnp.concatenate([w_in, w_prev], axis=1))
                scaled_bt.append((bm_t * rows(coef_row[h:h + 1, :], n_state)).astype(BF16))
            xblk = x16[:, blk * LANE:(blk + 1) * LANE]
            sblk = state[:, blk * LANE:(blk + 1) * LANE]
            rhs = jnp.concatenate([xblk, sblk.astype(BF16)], axis=0)
            yy = jnp.dot(jnp.concatenate(lhs, axis=0), rhs, preferred_element_type=F32)
            y_blocks.append(jnp.where(low_half, yy[:l], yy[l:]))
            ss = jnp.dot(jnp.concatenate(scaled_bt, axis=0), xblk, preferred_element_type=F32)
            keep = jnp.where(low_half[:1], keep_row[2 * blk:2 * blk + 1, :], keep_row[2 * blk + 1:2 * blk + 2, :])
            new_state.append(sblk * rows(keep, n_state) + jnp.where(low_half, ss[:n_state], ss[n_state:]))
        state_scr[b] = jnp.concatenate(new_state, axis=1)
        y = jnp.concatenate(y_blocks, axis=1)

        y = y + act[:, :inner] * dskip
        zf = jnp.concatenate([z_ref[c, b] for c in range(nxb)], axis=1).astype(F32)
        y = y * _silu(zf)
        ms = jnp.mean(y * y, axis=-1, keepdims=True)
        y = y * lax.rsqrt(ms + EPS) * norm_g
        for c in range(nxb):
            y_ref[c, b] = y[:, c * LANE:(c + 1) * LANE].astype(y_ref.dtype)


def _ssd(p, dt_t, conv_w, conv_b, dtb, alog, dskip_e, norm_g, dims, batch, seq, chunk, blk_off):
    g = dims.ssd_groups
    inner_g = dims.ssd_inner // g
    nxb = inner_g // LANE
    hg = dims.ssd_heads // g
    n_state = dims.ssd_state
    nt = seq // chunk
    kconv = conv_w.shape[0]
    z_off, x_off, b_off, c_off = blk_off
    p4 = p.reshape(p.shape[0], batch, seq, LANE)
    sb_off = dims.ssd_inner // n_state
    in_specs = [
        pl.BlockSpec((nxb, batch, chunk, LANE), lambda gg, t: (x_off // nxb + gg, 0, t, 0)),
        pl.BlockSpec((None, batch, chunk, LANE), lambda gg, t: (b_off + gg, 0, t, 0)),
        pl.BlockSpec((None, batch, chunk, LANE), lambda gg, t: (c_off + gg, 0, t, 0)),
        pl.BlockSpec((nxb, batch, chunk, LANE), lambda gg, t: (z_off // nxb + gg, 0, t, 0)),
        pl.BlockSpec((batch, hg, chunk), lambda gg, t: (0, gg, t)),
        pl.BlockSpec((kconv, inner_g), lambda gg, t: (0, gg)),
        pl.BlockSpec((kconv, n_state), lambda gg, t: (0, sb_off + gg)),
        pl.BlockSpec((kconv, n_state), lambda gg, t: (0, sb_off + g + gg)),
        pl.BlockSpec((1, inner_g), lambda gg, t: (0, gg)),
        pl.BlockSpec((1, n_state), lambda gg, t: (0, sb_off + gg)),
        pl.BlockSpec((1, n_state), lambda gg, t: (0, sb_off + g + gg)),
        pl.BlockSpec((hg, LANE), lambda gg, t: (gg, 0)),
        pl.BlockSpec((hg, LANE), lambda gg, t: (gg, 0)),
        pl.BlockSpec((1, inner_g), lambda gg, t: (0, gg)),
        pl.BlockSpec((1, inner_g), lambda gg, t: (0, gg)),
    ]
    y = pl.pallas_call(
        functools.partial(_ssd_kernel, hd=dims.ssd_head_dim, kconv=kconv),
        grid=(g, nt),
        in_specs=in_specs,
        out_specs=pl.BlockSpec((nxb, batch, chunk, LANE), lambda gg, t: (gg, 0, t, 0)),
        out_shape=jax.ShapeDtypeStruct((dims.ssd_inner // LANE, batch, seq, LANE), BF16),
        scratch_shapes=[pltpu.VMEM((batch, 8, inner_g + 2 * n_state), F32),
                        pltpu.VMEM((batch, n_state, inner_g), F32)],
        compiler_params=_params("parallel", "arbitrary"),
        name="ssd",
    )(p4, p4, p4, p4, dt_t, conv_w, conv_w, conv_w, conv_b, conv_b, conv_b, dtb, alog, dskip_e, norm_g)
    return y.reshape(y.shape[0], batch * seq, LANE)


def _merge_kernel(o_ref, y_ref, gsb_ref, gssd_ref, wsb_ref, wssd_ref, out_ref, o_scr, y_scr):
    @pl.when(pl.program_id(1) == 0)
    def _():
        for c in range(o_ref.shape[0]):
            o_scr[:, c * LANE:(c + 1) * LANE] = o_ref[c]
        for c in range(y_ref.shape[0]):
            y_scr[:, c * LANE:(c + 1) * LANE] = y_ref[c]

    a = jnp.dot(o_scr[...], wsb_ref[...], preferred_element_type=F32)
    b = jnp.dot(y_scr[...], wssd_ref[...], preferred_element_type=F32)
    gsb = jnp.concatenate([gsb_ref[c] for c in range(gsb_ref.shape[0])], axis=1).astype(F32)
    gssd = jnp.concatenate([gssd_ref[c] for c in range(gssd_ref.shape[0])], axis=1).astype(F32)
    out_ref[...] = (jax.nn.sigmoid(gsb) * a + jax.nn.sigmoid(gssd) * b).astype(out_ref.dtype)


def _merge(o_sb, y, p, w_sb, w_ssd, gsb_off, gssd_off, tm, tn):
    nob, m, _ = o_sb.shape
    nyb = y.shape[0]
    d = w_sb.shape[1]
    nb = tn // LANE
    return pl.pallas_call(
        _merge_kernel,
        grid=(m // tm, d // tn),
        in_specs=[
            pl.BlockSpec((nob, tm, LANE), lambda i, j: (0, i, 0)),
            pl.BlockSpec((nyb, tm, LANE), lambda i, j: (0, i, 0)),
            pl.BlockSpec((nb, tm, LANE), lambda i, j: (gsb_off // nb + j, i, 0)),
            pl.BlockSpec((nb, tm, LANE), lambda i, j: (gssd_off // nb + j, i, 0)),
            pl.BlockSpec((nob * LANE, tn), lambda i, j: (0, j)),
            pl.BlockSpec((nyb * LANE, tn), lambda i, j: (0, j)),
        ],
        out_specs=pl.BlockSpec((tm, tn), lambda i, j: (i, j)),
        out_shape=jax.ShapeDtypeStruct((m, d), BF16),
        scratch_shapes=[pltpu.VMEM((tm, nob * LANE), BF16), pltpu.VMEM((tm, nyb * LANE), BF16)],
        compiler_params=_params("parallel", "arbitrary"),
        name="merge",
    )(o_sb, y, p, p, w_sb, w_ssd)


def _proj_norm_res_kernel(a_ref, w_ref, g_ref, x_ref, out_ref, acc_scr):
    k = pl.program_id(1)
    part = jnp.dot(a_ref[...], w_ref[...], preferred_element_type=F32)

    @pl.when(k == 0)
    def _():
        acc_scr[...] = part

    @pl.when(k > 0)
    def _():
        acc_scr[...] += part

    @pl.when(k == pl.num_programs(1) - 1)
    def _():
        f = acc_scr[...]
        ms = jnp.mean(f * f, axis=-1, keepdims=True)
        out_ref[...] = x_ref[...] + f * lax.rsqrt(ms + EPS) * g_ref[...]


def _proj_norm_res(a, w, gain, x2, tm, tk, name):
    m, kdim = a.shape
    d = w.shape[1]
    return pl.pallas_call(
        _proj_norm_res_kernel,
        grid=(m // tm, kdim // tk),
        in_specs=[
            pl.BlockSpec((tm, tk), lambda i, k: (i, k)),
            pl.BlockSpec((tk, d), lambda i, k: (k, 0)),
            pl.BlockSpec((1, d), lambda i, k: (0, 0)),
            pl.BlockSpec((tm, d), lambda i, k: (i, 0)),
        ],
        out_specs=pl.BlockSpec((tm, d), lambda i, k: (i, 0)),
        out_shape=jax.ShapeDtypeStruct((m, d), F32),
        scratch_shapes=[pltpu.VMEM((tm, d), F32)],
        compiler_params=_params("parallel", "arbitrary"),
        name=name,
    )(a, w, gain, x2)


FFN_HALO = 16


def _ffn_up_kernel(x_ref, halo_ref, g_ref, wg_ref, wv_ref, cwg_ref, cwv_ref, cbg_ref, cbv_ref, out_ref, h_scr,
                   *, tiles_per_seq, kconv):
    tm = x_ref.shape[0]

    def norm(v):
        ms = jnp.mean(v * v, axis=-1, keepdims=True)
        return (v * lax.rsqrt(ms + EPS) * g_ref[...]).astype(BF16)

    @pl.when(pl.program_id(1) == 0)
    def _():
        seq_start = (pl.program_id(0) % tiles_per_seq) == 0
        h_scr[:FFN_HALO, :] = jnp.where(seq_start, jnp.zeros((), BF16), norm(halo_ref[...]))
        h_scr[FFN_HALO:, :] = norm(x_ref[...])

    h = h_scr[...]

    def conv(w_ref, cw_ref, cb_ref):
        up = jnp.dot(h, w_ref[...], preferred_element_type=F32)
        u = cb_ref[...]
        for k in range(kconv):
            off = FFN_HALO - (kconv - 1) + k
            u = u + cw_ref[k:k + 1, :] * up[off:off + tm, :]
        return u

    gate = conv(wg_ref, cwg_ref, cbg_ref)
    val = conv(wv_ref, cwv_ref, cbv_ref)
    out_ref[...] = (jax.nn.gelu(gate, approximate=True) * val).astype(out_ref.dtype)


def _ffn_up(x1, gain, w_up, conv_w, conv_b, seq, tm, tn):
    m, d = x1.shape
    dff = w_up.shape[1] // 2
    nj = dff // tn
    kconv = conv_w.shape[0]
    hb = tm // FFN_HALO
    return pl.pallas_call(
        functools.partial(_ffn_up_kernel, tiles_per_seq=seq // tm, kconv=kconv),
        grid=(m // tm, nj),
        in_specs=[
            pl.BlockSpec((tm, d), lambda i, j: (i, 0)),
            pl.BlockSpec((FFN_HALO, d), lambda i, j: (jnp.maximum(i * hb - 1, 0), 0)),
            pl.BlockSpec((1, d), lambda i, j: (0, 0)),
            pl.BlockSpec((d, tn), lambda i, j: (0, j)),
            pl.BlockSpec((d, tn), lambda i, j: (0, nj + j)),
            pl.BlockSpec((kconv, tn), lambda i, j: (0, j)),
            pl.BlockSpec((kconv, tn), lambda i, j: (0, nj + j)),
            pl.BlockSpec((1, tn), lambda i, j: (0, j)),
            pl.BlockSpec((1, tn), lambda i, j: (0, nj + j)),
        ],
        out_specs=pl.BlockSpec((tm, tn), lambda i, j: (i, j)),
        out_shape=jax.ShapeDtypeStruct((m, dff), BF16),
        scratch_shapes=[pltpu.VMEM((FFN_HALO + tm, d), BF16)],
        compiler_params=_params("parallel", "arbitrary"),
        name="ffn_up",
    )(x1, x1, gain, w_up, w_up, conv_w, conv_w, conv_b, conv_b)


def _largest_tile(total, want, quantum=LANE):
    t = min(want, total)
    while total % t or t % quantum:
        t -= quantum
    return t


def _block(x, norm_mix_pre, w_in, ssd_conv_w, ssd_conv_b, dt_bias, a_log, d_skip, ssd_norm, w_sb_proj,
           w_ssd_proj, w_out, norm_mix_post, norm_ffn_pre, w_up, ffn_conv_w, ffn_conv_b, w_down, norm_ffn_post,
           dims, tiles):
    batch, seq, d = x.shape
    m = batch * seq
    sbw, inner, xbc, heads = dims.sb_width, dims.ssd_inner, dims.ssd_xbc, dims.ssd_heads
    x2 = x.reshape(m, d)
    row = lambda v: v.reshape(1, -1)

    dt_lo = 3 * sbw + inner + xbc
    w_main = jnp.concatenate([w_in[:, :sbw] * dims.sb_head_dim ** -0.5, w_in[:, sbw:dt_lo],
                              w_in[:, dt_lo + heads:]], axis=1).astype(BF16)
    w_dt_t = w_in[:, dt_lo:dt_lo + heads].T.astype(BF16)
    z_off = 3 * sbw // LANE
    x_off = z_off + inner // LANE
    b_off = x_off + inner // LANE
    c_off = b_off + dims.ssd_groups * dims.ssd_state // LANE
    gsb_off = c_off + dims.ssd_groups * dims.ssd_state // LANE
    gssd_off = gsb_off + d // LANE

    tm = _largest_tile(seq, tiles["tm"], 16)
    p, dt_t = _in_proj(x2, row(norm_mix_pre), w_main, w_dt_t, seq, tm,
                       _largest_tile(w_main.shape[1], tiles["tn_in"]))

    o_sb = _sb_attn(p, dims, batch, seq, min(tiles["tq"], seq), tiles["sb_chains"])

    chunk = min(tiles["chunk"], seq)
    bcast = lambda v: jnp.broadcast_to(v.reshape(-1, 1), (heads, LANE))
    y = _ssd(p, dt_t, ssd_conv_w, row(ssd_conv_b), bcast(dt_bias), bcast(a_log),
             row(jnp.repeat(d_skip, dims.ssd_head_dim)), row(ssd_norm), dims, batch, seq, chunk,
             (z_off, x_off, b_off, c_off))

    merged = _merge(o_sb, y, p, w_sb_proj.astype(BF16), w_ssd_proj.astype(BF16), gsb_off, gssd_off,
                    _largest_tile(m, tiles["tm_merge"], 16), _largest_tile(d, tiles["tn_merge"]))
    tm2 = _largest_tile(m, tiles["tm_res"], 16)
    x1 = _proj_norm_res(merged, w_out.astype(BF16), row(norm_mix_post), x2, tm2, d, "out_proj")

    tm_up = _largest_tile(seq, tiles["tm"], 16)
    act = _ffn_up(x1, row(norm_ffn_pre), w_up.astype(BF16), ffn_conv_w, row(ffn_conv_b), seq, tm_up,
                  _largest_tile(dims.d_ff, tiles["tn_up"]))
    out = _proj_norm_res(act, w_down.astype(BF16), row(norm_ffn_post), x1, tm2,
                         _largest_tile(dims.d_ff, tiles["tk_down"]), "ffn_down")
    return out.reshape(batch, seq, d)


TILES = dict(tm=1024, tn_in=1024, tq=128, sb_chains=8, chunk=128, tm_merge=512, tn_merge=512, tm_res=512, tn_up=512, tk_down=1408)


def kernel(x, norm_mix_pre, w_in, ssd_conv_w, ssd_conv_b, dt_bias, a_log, d_skip, ssd_norm, w_sb_proj, w_ssd_proj,
           w_out, norm_mix_post, norm_ffn_pre, w_up, ffn_conv_w, ffn_conv_b, w_down, norm_ffn_post):
    dims = Dims()
    args = (norm_mix_pre, w_in, ssd_conv_w, ssd_conv_b, dt_bias, a_log, d_skip, ssd_norm, w_sb_proj, w_ssd_proj,
            w_out, norm_mix_post, norm_ffn_pre, w_up, ffn_conv_w, ffn_conv_b, w_down, norm_ffn_post)
    for layer in range(w_in.shape[0]):
        x = _block(x, *(a[layer] for a in args), dims, TILES)
    return x
```

```python
import dataclasses
import functools
import math

import jax
import jax.numpy as jnp
from jax import lax
from jax.experimental import pallas as pl
from jax.experimental.pallas import tpu as pltpu

F32 = jnp.float32
BF16 = jnp.bfloat16
LANE = 128
EPS = 1e-6


@dataclasses.dataclass(frozen=True)
class Dims:
    d_model: int = 2048
    sb_heads: int = 16
    sb_head_dim: int = 128
    ssd_inner: int = 4096
    ssd_head_dim: int = 64
    ssd_groups: int = 8
    ssd_state: int = 128
    ssd_conv: int = 4
    d_ff: int = 5632
    ffn_conv: int = 3

    @property
    def sb_width(self):
        return self.sb_heads * self.sb_head_dim

    @property
    def ssd_heads(self):
        return self.ssd_inner // self.ssd_head_dim

    @property
    def ssd_xbc(self):
        return self.ssd_inner + 2 * self.ssd_groups * self.ssd_state


SB_LOG_WEIGHT_FLOOR = -110.0
VMEM_LIMIT = 56 * 1024 * 1024


def _softplus(x):
    return jnp.maximum(x, 0.0) + jnp.log1p(jnp.exp(-jnp.abs(x)))


def _rows(v, n):
    return jnp.tile(jnp.broadcast_to(v, (8, v.shape[1])), (n // 8, 1))


def _silu(x):
    h = 0.5 * x
    return h + h * jnp.tanh(h)


def _params(*sem):
    return pltpu.CompilerParams(dimension_semantics=sem, vmem_limit_bytes=VMEM_LIMIT)


def _in_proj_kernel(x_ref, g_ref, wa_ref, wb_ref, wdt_ref, p_ref, dt_ref, h_scr, *, na):
    j = pl.program_id(1)

    @pl.when(j == 0)
    def _():
        x = x_ref[...]
        ms = jnp.mean(x * x, axis=-1, keepdims=True)
        h = (x * lax.rsqrt(ms + EPS) * g_ref[...]).astype(BF16)
        h_scr[...] = h
        wdt_t = wdt_ref[...].T.astype(BF16)
        dt = lax.dot_general(wdt_t, h, (((1,), (1,)), ((), ())), preferred_element_type=F32)
        dt_ref[...] = dt[:dt_ref.shape[0]]

    def emit(w_ref):
        acc = jnp.dot(h_scr[...], w_ref[...], preferred_element_type=F32)
        for c in range(p_ref.shape[0]):
            p_ref[c] = acc[:, c * LANE:(c + 1) * LANE].astype(BF16)

    pl.when(j < na)(lambda: emit(wa_ref))
    pl.when(j >= na)(lambda: emit(wb_ref))


def _in_proj(x2, gain, w_a, w_b, w_dt, hd, seq, tm, tn):
    m, d = x2.shape
    na, nb = w_a.shape[1] // tn, w_b.shape[1] // tn
    n = w_a.shape[1] + w_b.shape[1]
    tps = seq // tm
    return pl.pallas_call(
        functools.partial(_in_proj_kernel, na=na),
        grid=(m // tm, na + nb),
        in_specs=[
            pl.BlockSpec((tm, d), lambda i, j: (i, 0)),
            pl.BlockSpec((1, d), lambda i, j: (0, 0)),
            pl.BlockSpec((d, tn), lambda i, j: (0, jnp.minimum(j, na - 1))),
            pl.BlockSpec((d, tn), lambda i, j: (0, jnp.maximum(j - na, 0))),
            pl.BlockSpec((d, LANE), lambda i, j: (0, 0)),
        ],
        out_specs=[
            pl.BlockSpec((tn // LANE, tm, LANE), lambda i, j: (j, i, 0)),
            pl.BlockSpec((None, hd, tm), lambda i, j: (i // tps, 0, i % tps)),
        ],
        out_shape=[
            jax.ShapeDtypeStruct((n // LANE, m, LANE), BF16),
            jax.ShapeDtypeStruct((m // seq, hd, seq), F32),
        ],
        scratch_shapes=[pltpu.VMEM((tm, d), BF16)],
        compiler_params=_params("parallel", "arbitrary"),
        name="in_proj",
    )(x2, gain, w_a, w_b, w_dt)


def _sb_attn_kernel(q_ref, k_ref, v_ref, o_ref, *, tq, nchain):
    s, dh = q_ref.shape
    ngroups = s // (tq * nchain)
    row = lax.broadcasted_iota(jnp.int32, (tq, tq), 0)
    col = lax.broadcasted_iota(jnp.int32, (tq, tq), 1)
    causal = col < row
    suffix = jnp.where(row > col, 1.0, 0.0).astype(BF16)
    half = jnp.concatenate([suffix, jnp.ones((tq, tq), BF16)], axis=1)
    sum_rhs = jnp.concatenate([half, half], axis=0)

    def key_blocks(qbs, js, carries, accs, masked):
        starts = [pl.multiple_of(j * tq, tq) for j in js]
        zs = [lax.dot_general(qb, k_ref[pl.ds(st, tq), :], (((1,), (1,)), ((), ())), preferred_element_type=F32)
              for qb, st in zip(qbs, starts)]
        sps, parts = [], []
        for z in zs:
            sp = jnp.maximum(z, 0.0) + jnp.log(1.0 + jnp.exp(-jnp.abs(z)))
            if masked:
                sp = jnp.where(causal, sp, 0.0)
            hi = sp.astype(BF16)
            lo = (sp - hi.astype(F32)).astype(BF16)
            sps.append(sp)
            parts.append(jnp.concatenate([hi, lo], axis=1))
        sums = jnp.dot(jnp.concatenate(parts, axis=0), sum_rhs, preferred_element_type=F32)
        ws = []
        for c, (z, sp) in enumerate(zip(zs, sps)):
            w = jnp.exp(z - sp - sums[c * tq:(c + 1) * tq, :tq] - carries[c])
            if masked:
                w = jnp.where(causal, w, 0.0)
            ws.append(w.astype(BF16))
        new_a = [acc + jnp.dot(w, v_ref[pl.ds(st, tq), :], preferred_element_type=F32)
                 for acc, w, st in zip(accs, ws, starts)]
        new_c = [carries[c] + sums[c * tq:(c + 1) * tq, tq:] for c in range(len(js))]
        return new_c, new_a

    def q_group(gi, _):
        qi = [gi * nchain + c for c in range(nchain)]
        qstart = [pl.multiple_of(i * tq, tq) for i in qi]
        qb = [q_ref[pl.ds(st, tq), :] for st in qstart]
        carries, accs = key_blocks(qb, qi, [jnp.zeros((tq, tq), F32)] * nchain,
                                   [jnp.zeros((tq, dh), F32)] * nchain, True)

        def retire(t, carries):
            carries = [jnp.where(qi[c] - t >= 0, carries[c], jnp.inf) for c in range(nchain)]
            return carries, jnp.min(functools.reduce(jnp.minimum, carries))

        def cond(st):
            return st[1] < -SB_LOG_WEIGHT_FLOOR

        def body(st):
            t, _, carries, accs = st
            js = [jnp.maximum(qi[c] - t, 0) for c in range(nchain)]
            carries, accs = key_blocks(qb, js, carries, accs, False)
            carries, low = retire(t + 1, carries)
            return t + 1, low, carries, accs

        carries, low = retire(1, carries)
        _, _, _, accs = lax.while_loop(cond, body, (1, low, carries, accs))
        for c in range(nchain):
            o_ref[pl.ds(qstart[c], tq), :] = accs[c].astype(o_ref.dtype)
        return 0

    lax.fori_loop(0, ngroups, q_group, 0)


def _sb_attn(p, dims, batch, seq, tq, nchain):
    h, dh = dims.sb_heads, dims.sb_head_dim
    m = batch * seq
    spec = lambda off: pl.BlockSpec((None, seq, dh), lambda b, hh: (off + hh, b, 0))
    return pl.pallas_call(
        functools.partial(_sb_attn_kernel, tq=tq, nchain=nchain),
        grid=(batch, h),
        in_specs=[spec(0), spec(h), spec(2 * h)],
        out_specs=pl.BlockSpec((seq, dh), lambda b, hh: (b, hh)),
        out_shape=jax.ShapeDtypeStruct((m, h * dh), BF16),
        compiler_params=_params("parallel", "parallel"),
        name="sb_attn",
    )(p, p, p)


def _ssd_kernel(x_ref, b_ref, c_ref, z_ref, dt_ref, wx_ref, wb_ref, wc_ref, bx_ref, bb_ref, bc_ref,
                dtb_ref, alog_ref, dskip_ref, norm_ref, y_ref, halo_scr, state_scr, *, hd, kconv):
    nxb, nbatch, l, _ = x_ref.shape
    hg = dt_ref.shape[1]
    inner = nxb * LANE
    n_state = b_ref.shape[-1]
    assert l == LANE and 2 * hd == LANE and n_state == LANE

    @pl.when(pl.program_id(1) == 0)
    def _():
        halo_scr[...] = jnp.zeros_like(halo_scr)
        state_scr[...] = jnp.zeros_like(state_scr)

    rows = functools.partial(_rows, n=l)

    wcat = jnp.concatenate([wx_ref[...], wb_ref[...], wc_ref[...]], axis=1)
    taps = [rows(wcat[k:k + 1, :]) for k in range(kconv)]
    bcat = rows(jnp.concatenate([bx_ref[...], bb_ref[...], bc_ref[...]], axis=1))
    dskip = rows(dskip_ref[...])
    norm_g = rows(norm_ref[...])
    neg_a = -jnp.exp(alog_ref[...])
    lane = lax.broadcasted_iota(jnp.int32, (l, LANE), 1)
    lane_t = lax.broadcasted_iota(jnp.int32, (hg, l), 1)
    low_half = lane < hd
    tril = lane <= lax.broadcasted_iota(jnp.int32, (l, l), 0)
    pad = jnp.zeros((LANE - hg, l), F32)

    for b in range(nbatch):
        raw = jnp.concatenate([x_ref[c, b] for c in range(nxb)] + [b_ref[b], c_ref[b]], axis=1).astype(F32)
        xp = jnp.concatenate([halo_scr[b], raw], axis=0)
        halo_scr[b] = raw[l - 8:, :]
        conv = bcat
        for k in range(kconv):
            off = 8 - (kconv - 1) + k
            conv = conv + taps[k] * xp[off:off + l, :]
        act = _silu(conv)
        x16 = act[:, :inner].astype(BF16)
        bm = act[:, inner:inner + n_state]
        cm = act[:, inner + n_state:]

        dt = _softplus(dt_ref[b] + dtb_ref[...])
        cs = dt * neg_a
        sh = 1
        while sh < l:
            cs = cs + jnp.where(lane_t >= sh, pltpu.roll(cs, sh, axis=1), 0.0)
            sh *= 2
        cs_last = jnp.broadcast_to(cs[:, l - 1:l], (hg, l))
        src_row = cs - jnp.log(dt)
        coef_row = dt * jnp.exp(cs_last - cs)
        keep_row = jnp.exp(cs_last)
        cs_col = jnp.concatenate([cs, pad], axis=0).T

        cb = lax.dot_general(cm.astype(BF16), bm.astype(BF16), (((1,), (1,)), ((), ())),
                             preferred_element_type=F32)
        bm_t = bm.T
        state = state_scr[b]
        y_blocks, new_state = [], []
        for blk in range(nxb):
            lhs, scaled_bt = [], []
            for h in (2 * blk, 2 * blk + 1):
                ccol = jnp.broadcast_to(cs_col[:, h:h + 1], (l, LANE))
                seg = ccol - rows(src_row[h:h + 1, :])
                w_in = (cb * jnp.exp(jnp.where(tril, seg, -jnp.inf))).astype(BF16)
                w_prev = (cm * jnp.exp(ccol)).astype(BF16)
                lhs.append(jnp.concatenate([w_in, w_prev], axis=1))
                scaled_bt.append((bm_t * rows(coef_row[h:h + 1, :])).astype(BF16))
            xblk = x16[:, blk * LANE:(blk + 1) * LANE]
            sblk = state[:, blk * LANE:(blk + 1) * LANE]
            rhs = jnp.concatenate([xblk, sblk.astype(BF16)], axis=0)
            yy = jnp.dot(jnp.concatenate(lhs, axis=0), rhs, preferred_element_type=F32)
            y_blocks.append(jnp.where(low_half, yy[:l], yy[l:]))
            ss = jnp.dot(jnp.concatenate(scaled_bt, axis=0), xblk, preferred_element_type=F32)
            keep = jnp.where(low_half[:1], keep_row[2 * blk:2 * blk + 1, :], keep_row[2 * blk + 1:2 * blk + 2, :])
            new_state.append(sblk * rows(keep) + jnp.where(low_half, ss[:n_state], ss[n_state:]))
        state_scr[b] = jnp.concatenate(new_state, axis=1)
        y = jnp.concatenate(y_blocks, axis=1)

        y = y + act[:, :inner] * dskip
        zf = jnp.concatenate([z_ref[c, b] for c in range(nxb)], axis=1).astype(F32)
        y = y * _silu(zf)
        ms = jnp.mean(y * y, axis=-1, keepdims=True)
        y = y * lax.rsqrt(ms + EPS) * norm_g
        y_ref[b] = y.astype(y_ref.dtype)


def _ssd(p, dt_t, conv_w, conv_b, dtb, alog, dskip_e, norm_g, dims, batch, seq, chunk, blk_off):
    g = dims.ssd_groups
    inner_g = dims.ssd_inner // g
    nxb = inner_g // LANE
    hg = dims.ssd_heads // g
    n_state = dims.ssd_state
    nt = seq // chunk
    kconv = conv_w.shape[0]
    z_off, x_off, b_off, c_off = blk_off
    p4 = p.reshape(p.shape[0], batch, seq, LANE)
    sb_off = dims.ssd_inner // n_state
    in_specs = [
        pl.BlockSpec((nxb, batch, chunk, LANE), lambda gg, t: (x_off // nxb + gg, 0, t, 0)),
        pl.BlockSpec((None, batch, chunk, LANE), lambda gg, t: (b_off + gg, 0, t, 0)),
        pl.BlockSpec((None, batch, chunk, LANE), lambda gg, t: (c_off + gg, 0, t, 0)),
        pl.BlockSpec((nxb, batch, chunk, LANE), lambda gg, t: (z_off // nxb + gg, 0, t, 0)),
        pl.BlockSpec((batch, hg, chunk), lambda gg, t: (0, gg, t)),
        pl.BlockSpec((kconv, inner_g), lambda gg, t: (0, gg)),
        pl.BlockSpec((kconv, n_state), lambda gg, t: (0, sb_off + gg)),
        pl.BlockSpec((kconv, n_state), lambda gg, t: (0, sb_off + g + gg)),
        pl.BlockSpec((1, inner_g), lambda gg, t: (0, gg)),
        pl.BlockSpec((1, n_state), lambda gg, t: (0, sb_off + gg)),
        pl.BlockSpec((1, n_state), lambda gg, t: (0, sb_off + g + gg)),
        pl.BlockSpec((hg, LANE), lambda gg, t: (gg, 0)),
        pl.BlockSpec((hg, LANE), lambda gg, t: (gg, 0)),
        pl.BlockSpec((1, inner_g), lambda gg, t: (0, gg)),
        pl.BlockSpec((1, inner_g), lambda gg, t: (0, gg)),
    ]
    y = pl.pallas_call(
        functools.partial(_ssd_kernel, hd=dims.ssd_head_dim, kconv=kconv),
        grid=(g, nt),
        in_specs=in_specs,
        out_specs=pl.BlockSpec((batch, chunk, inner_g), lambda gg, t: (0, t, gg)),
        out_shape=jax.ShapeDtypeStruct((batch, seq, dims.ssd_inner), BF16),
        scratch_shapes=[pltpu.VMEM((batch, 8, inner_g + 2 * n_state), F32),
                        pltpu.VMEM((batch, n_state, inner_g), F32)],
        compiler_params=_params("parallel", "arbitrary"),
        name="ssd",
    )(p4, p4, p4, p4, dt_t, conv_w, conv_w, conv_w, conv_b, conv_b, conv_b, dtb, alog, dskip_e, norm_g)
    return y.reshape(batch * seq, dims.ssd_inner)


def _merge_kernel(o_ref, y_ref, gsb_ref, gssd_ref, wsb_ref, wssd_ref, out_ref):
    a = jnp.dot(o_ref[...], wsb_ref[...], preferred_element_type=F32)
    b = jnp.dot(y_ref[...], wssd_ref[...], preferred_element_type=F32)
    gsb = jnp.concatenate([gsb_ref[c] for c in range(gsb_ref.shape[0])], axis=1).astype(F32)
    gssd = jnp.concatenate([gssd_ref[c] for c in range(gssd_ref.shape[0])], axis=1).astype(F32)
    out_ref[...] = (jax.nn.sigmoid(gsb) * a + jax.nn.sigmoid(gssd) * b).astype(out_ref.dtype)


def _merge(o_sb, y, p, w_sb, w_ssd, gsb_off, gssd_off, tm, tn):
    m, ko = o_sb.shape
    ky = y.shape[1]
    d = w_sb.shape[1]
    nb = tn // LANE
    return pl.pallas_call(
        _merge_kernel,
        grid=(m // tm, d // tn),
        in_specs=[
            pl.BlockSpec((tm, ko), lambda i, j: (i, 0)),
            pl.BlockSpec((tm, ky), lambda i, j: (i, 0)),
            pl.BlockSpec((nb, tm, LANE), lambda i, j: (gsb_off // nb + j, i, 0)),
            pl.BlockSpec((nb, tm, LANE), lambda i, j: (gssd_off // nb + j, i, 0)),
            pl.BlockSpec((ko, tn), lambda i, j: (0, j)),
            pl.BlockSpec((ky, tn), lambda i, j: (0, j)),
        ],
        out_specs=pl.BlockSpec((tm, tn), lambda i, j: (i, j)),
        out_shape=jax.ShapeDtypeStruct((m, d), BF16),
        compiler_params=_params("parallel", "arbitrary"),
        name="merge",
    )(o_sb, y, p, p, w_sb, w_ssd)


def _proj_norm_res_kernel(a_ref, w_ref, g_ref, x_ref, out_ref, f_scr):
    nj, tm, tn = f_scr.shape
    j = pl.program_id(1)
    f_scr[j] = jnp.dot(a_ref[...], w_ref[...], preferred_element_type=F32)

    @pl.when(j == nj - 1)
    def _():
        ss = jnp.zeros((tm, 1), F32)
        for c in range(nj):
            f = f_scr[c]
            ss = ss + jnp.sum(f * f, axis=-1, keepdims=True)
        inv = lax.rsqrt(ss * (1.0 / (nj * tn)) + EPS)
        for c in range(nj):
            cols = slice(c * tn, (c + 1) * tn)
            out_ref[:, cols] = x_ref[:, cols] + f_scr[c] * inv * _rows(g_ref[:, cols], tm)


def _proj_norm_res(a, w, gain, x2, tm, tn, name):
    m, kdim = a.shape
    d = w.shape[1]
    return pl.pallas_call(
        _proj_norm_res_kernel,
        grid=(m // tm, d // tn),
        in_specs=[
            pl.BlockSpec((tm, kdim), lambda i, j: (i, 0)),
            pl.BlockSpec((kdim, tn), lambda i, j: (0, j)),
            pl.BlockSpec((1, d), lambda i, j: (0, 0)),
            pl.BlockSpec((tm, d), lambda i, j: (i, 0)),
        ],
        out_specs=pl.BlockSpec((tm, d), lambda i, j: (i, 0)),
        out_shape=jax.ShapeDtypeStruct((m, d), F32),
        scratch_shapes=[pltpu.VMEM((d // tn, tm, tn), F32)],
        compiler_params=_params("parallel", "arbitrary"),
        name=name,
    )(a, w, gain, x2)


FFN_HALO = 16


def _ffn_up_kernel(x_ref, halo_ref, g_ref, wg_ref, wv_ref, cwg_ref, cwv_ref, cbg_ref, cbv_ref, out_ref, h_scr,
                   *, tiles_per_seq, kconv):
    tm = x_ref.shape[0]

    def norm(v):
        ms = jnp.mean(v * v, axis=-1, keepdims=True)
        return (v * lax.rsqrt(ms + EPS) * g_ref[...]).astype(BF16)

    @pl.when(pl.program_id(1) == 0)
    def _():
        seq_start = (pl.program_id(0) % tiles_per_seq) == 0
        h_scr[:FFN_HALO, :] = jnp.where(seq_start, jnp.zeros((), BF16), norm(halo_ref[...]))
        h_scr[FFN_HALO:, :] = norm(x_ref[...])

    h = h_scr[...]

    def conv(w_ref, cw_ref, cb_ref):
        up = jnp.dot(h, w_ref[...], preferred_element_type=F32)
        u = _rows(cb_ref[...], tm)
        for k in range(kconv):
            off = FFN_HALO - (kconv - 1) + k
            u = u + _rows(cw_ref[k:k + 1, :], tm) * up[off:off + tm, :]
        return u

    gate = conv(wg_ref, cwg_ref, cbg_ref)
    val = conv(wv_ref, cwv_ref, cbv_ref)
    c = math.sqrt(2.0 / math.pi)
    half = 0.5 * gate
    act = half + half * jnp.tanh(gate * (c + (c * 0.044715) * (gate * gate)))
    out_ref[...] = (act * val).astype(out_ref.dtype)


def _ffn_up(x1, gain, w_up, conv_w, conv_b, seq, tm, tn):
    m, d = x1.shape
    dff = w_up.shape[1] // 2
    nj = dff // tn
    kconv = conv_w.shape[0]
    hb = tm // FFN_HALO
    return pl.pallas_call(
        functools.partial(_ffn_up_kernel, tiles_per_seq=seq // tm, kconv=kconv),
        grid=(m // tm, nj),
        in_specs=[
            pl.BlockSpec((tm, d), lambda i, j: (i, 0)),
            pl.BlockSpec((FFN_HALO, d), lambda i, j: (jnp.maximum(i * hb - 1, 0), 0)),
            pl.BlockSpec((1, d), lambda i, j: (0, 0)),
            pl.BlockSpec((d, tn), lambda i, j: (0, j)),
            pl.BlockSpec((d, tn), lambda i, j: (0, nj + j)),
            pl.BlockSpec((kconv, tn), lambda i, j: (0, j)),
            pl.BlockSpec((kconv, tn), lambda i, j: (0, nj + j)),
            pl.BlockSpec((1, tn), lambda i, j: (0, j)),
            pl.BlockSpec((1, tn), lambda i, j: (0, nj + j)),
        ],
        out_specs=pl.BlockSpec((tm, tn), lambda i, j: (i, j)),
        out_shape=jax.ShapeDtypeStruct((m, dff), BF16),
        scratch_shapes=[pltpu.VMEM((FFN_HALO + tm, d), BF16)],
        compiler_params=_params("parallel", "arbitrary"),
        name="ffn_up",
    )(x1, x1, gain, w_up, w_up, conv_w, conv_w, conv_b, conv_b)


def _largest_tile(total, want, quantum=LANE):
    t = min(want, total)
    while total % t or t % quantum:
        t -= quantum
    return t


def _block(x, norm_mix_pre, w_in, ssd_conv_w, ssd_conv_b, dt_bias, a_log, d_skip, ssd_norm, w_sb_proj,
           w_ssd_proj, w_out, norm_mix_post, norm_ffn_pre, w_up, ffn_conv_w, ffn_conv_b, w_down, norm_ffn_post,
           dims, tiles):
    batch, seq, d = x.shape
    m = batch * seq
    sbw, inner, xbc, heads = dims.sb_width, dims.ssd_inner, dims.ssd_xbc, dims.ssd_heads
    x2 = x.reshape(m, d)
    row = lambda v: v.reshape(1, -1)

    dt_lo = 3 * sbw + inner + xbc
    col_scale = jnp.where(jnp.arange(dt_lo) < sbw, dims.sb_head_dim ** -0.5, 1.0).astype(F32)
    w_a = (w_in[:, :dt_lo] * col_scale).astype(BF16)
    w_b = w_in[:, dt_lo + heads:].astype(BF16)
    w_dt = jnp.pad(w_in[:, dt_lo:dt_lo + heads], ((0, 0), (0, LANE - heads)))
    z_off = 3 * sbw // LANE
    x_off = z_off + inner // LANE
    b_off = x_off + inner // LANE
    c_off = b_off + dims.ssd_groups * dims.ssd_state // LANE
    gsb_off = c_off + dims.ssd_groups * dims.ssd_state // LANE
    gssd_off = gsb_off + d // LANE

    tm = _largest_tile(seq, tiles["tm"], 16)
    p, dt_t = _in_proj(x2, row(norm_mix_pre), w_a, w_b, w_dt, heads, seq, tm,
                       _largest_tile(math.gcd(w_a.shape[1], w_b.shape[1]), tiles["tn_in"]))

    o_sb = _sb_attn(p, dims, batch, seq, min(tiles["tq"], seq), tiles["sb_chains"])

    chunk = min(tiles["chunk"], seq)
    bcast = lambda v: jnp.broadcast_to(v.reshape(-1, 1), (heads, LANE))
    y = _ssd(p, dt_t, ssd_conv_w, row(ssd_conv_b), bcast(dt_bias), bcast(a_log),
             row(jnp.repeat(d_skip, dims.ssd_head_dim)), row(ssd_norm), dims, batch, seq, chunk,
             (z_off, x_off, b_off, c_off))

    merged = _merge(o_sb, y, p, w_sb_proj.astype(BF16), w_ssd_proj.astype(BF16), gsb_off, gssd_off,
                    _largest_tile(m, tiles["tm_merge"], 16), _largest_tile(d, tiles["tn_merge"]))
    tm2 = _largest_tile(m, tiles["tm_res"], 16)
    x1 = _proj_norm_res(merged, w_out.astype(BF16), row(norm_mix_post), x2, tm2, d, "out_proj")

    tm_up = _largest_tile(seq, tiles["tm"], 16)
    act = _ffn_up(x1, row(norm_ffn_pre), w_up.astype(BF16), ffn_conv_w, row(ffn_conv_b), seq, tm_up,
                  _largest_tile(dims.d_ff, tiles["tn_up"]))
    out = _proj_norm_res(act, w_down.astype(BF16), row(norm_ffn_post), x1, tm2,
                         _largest_tile(d, tiles["tn_down"]), "ffn_down")
    return out.reshape(batch, seq, d)


TILES = dict(tm=1024, tn_in=1024, tq=128, sb_chains=8, chunk=128, tm_merge=1024, tn_merge=512, tm_res=512, tn_up=512, tn_down=512)


def kernel(x, norm_mix_pre, w_in, ssd_conv_w, ssd_conv_b, dt_bias, a_log, d_skip, ssd_norm, w_sb_proj, w_ssd_proj,
           w_out, norm_mix_post, norm_ffn_pre, w_up, ffn_conv_w, ffn_conv_b, w_down, norm_ffn_post):
    dims = Dims()
    args = (norm_mix_pre, w_in, ssd_conv_w, ssd_conv_b, dt_bias, a_log, d_skip, ssd_norm, w_sb_proj, w_ssd_proj,
            w_out, norm_mix_post, norm_ffn_pre, w_up, ffn_conv_w, ffn_conv_b, w_down, norm_ffn_post)
    for layer in range(w_in.shape[0]):
        x = _block(x, *(a[layer] for a in args), dims, TILES)
    return x
```

```python
import dataclasses
import functools
import math

import jax
import jax.numpy as jnp
from jax import lax
from jax.experimental import pallas as pl
from jax.experimental.pallas import tpu as pltpu

F32 = jnp.float32
BF16 = jnp.bfloat16
LANE = 128
EPS = 1e-6


@dataclasses.dataclass(frozen=True)
class Dims:
    d_model: int = 2048
    sb_heads: int = 16
    sb_head_dim: int = 128
    ssd_inner: int = 4096
    ssd_head_dim: int = 64
    ssd_groups: int = 8
    ssd_state: int = 128
    ssd_conv: int = 4
    d_ff: int = 5632
    ffn_conv: int = 3

    @property
    def sb_width(self):
        return self.sb_heads * self.sb_head_dim

    @property
    def ssd_heads(self):
        return self.ssd_inner // self.ssd_head_dim

    @property
    def ssd_xbc(self):
        return self.ssd_inner + 2 * self.ssd_groups * self.ssd_state


SB_LOG_WEIGHT_FLOOR = -110.0
SB_EAGER_STEPS = 2
VMEM_LIMIT = 56 * 1024 * 1024


def _softplus(x):
    return jnp.maximum(x, 0.0) + jnp.log1p(jnp.exp(-jnp.abs(x)))


def _rows(v, n):
    return jnp.tile(jnp.broadcast_to(v, (8, v.shape[1])), (n // 8, 1))


def _silu(x):
    h = 0.5 * x
    return h + h * jnp.tanh(h)


def _params(*sem):
    return pltpu.CompilerParams(dimension_semantics=sem, vmem_limit_bytes=VMEM_LIMIT)


def _in_proj_kernel(x_ref, g_ref, wa_ref, wb_ref, wdt_ref, p_ref, dt_ref, h_scr, *, na):
    j = pl.program_id(1)

    @pl.when(j == 0)
    def _():
        x = x_ref[...]
        ms = jnp.mean(x * x, axis=-1, keepdims=True)
        h = (x * lax.rsqrt(ms + EPS) * g_ref[...]).astype(BF16)
        h_scr[...] = h
        dt_ref[...] = lax.dot_general(wdt_ref[...].astype(BF16), h, (((1,), (1,)), ((), ())),
                                      preferred_element_type=F32)

    def emit(w_ref):
        acc = lax.dot_general(h_scr[...], w_ref[...], (((1,), (1,)), ((), ())), preferred_element_type=F32)
        for c in range(p_ref.shape[0]):
            p_ref[c] = acc[:, c * LANE:(c + 1) * LANE].astype(BF16)

    pl.when(j < na)(lambda: emit(wa_ref))
    pl.when(j >= na)(lambda: emit(wb_ref))


def _in_proj(x2, gain, w_a, w_b, w_dt, hd, seq, tm, tn):
    m, d = x2.shape
    na, nb = w_a.shape[0] // tn, w_b.shape[0] // tn
    n = w_a.shape[0] + w_b.shape[0]
    tps = seq // tm
    return pl.pallas_call(
        functools.partial(_in_proj_kernel, na=na),
        grid=(m // tm, na + nb),
        in_specs=[
            pl.BlockSpec((tm, d), lambda i, j: (i, 0)),
            pl.BlockSpec((1, d), lambda i, j: (0, 0)),
            pl.BlockSpec((tn, d), lambda i, j: (jnp.minimum(j, na - 1), 0)),
            pl.BlockSpec((tn, d), lambda i, j: (jnp.maximum(j - na, 0), 0)),
            pl.BlockSpec((hd, d), lambda i, j: (0, 0)),
        ],
        out_specs=[
            pl.BlockSpec((tn // LANE, tm, LANE), lambda i, j: (j, i, 0)),
            pl.BlockSpec((None, hd, tm), lambda i, j: (i // tps, 0, i % tps)),
        ],
        out_shape=[
            jax.ShapeDtypeStruct((n // LANE, m, LANE), BF16),
            jax.ShapeDtypeStruct((m // seq, hd, seq), F32),
        ],
        scratch_shapes=[pltpu.VMEM((tm, d), BF16)],
        compiler_params=_params("parallel", "arbitrary"),
        name="in_proj",
    )(x2, gain, w_a, w_b, w_dt)


def _sb_attn_kernel(q_ref, k_ref, v_ref, o_ref, *, tq, nchain):
    s, dh = q_ref.shape
    ngroups = s // (tq * nchain)
    row = lax.broadcasted_iota(jnp.int32, (tq, tq), 0)
    col = lax.broadcasted_iota(jnp.int32, (tq, tq), 1)
    causal = col < row
    suffix = jnp.where(row > col, 1.0, 0.0).astype(BF16)
    half = jnp.concatenate([suffix, jnp.ones((tq, tq), BF16)], axis=1)
    sum_rhs = jnp.concatenate([half, half], axis=0)

    def scores(qbs, js):
        return [lax.dot_general(qb, k_ref[pl.ds(pl.multiple_of(j * tq, tq), tq), :], (((1,), (1,)), ((), ())),
                                preferred_element_type=F32) for qb, j in zip(qbs, js)]

    def key_blocks(zs, js, carries, accs, masked):
        starts = [pl.multiple_of(j * tq, tq) for j in js]
        sps, parts = [], []
        for z in zs:
            sp = jnp.maximum(z, 0.0) + jnp.log(1.0 + jnp.exp(-jnp.abs(z)))
            if masked:
                sp = jnp.where(causal, sp, 0.0)
            hi = sp.astype(BF16)
            lo = (sp - hi.astype(F32)).astype(BF16)
            sps.append(sp)
            parts.append(jnp.concatenate([hi, lo], axis=1))
        sums = jnp.dot(jnp.concatenate(parts, axis=0), sum_rhs, preferred_element_type=F32)
        ws = []
        for c, (z, sp) in enumerate(zip(zs, sps)):
            w = jnp.exp(z - sp - sums[c * tq:(c + 1) * tq, :tq] - carries[c])
            if masked:
                w = jnp.where(causal, w, 0.0)
            ws.append(w.astype(BF16))
        new_a = [acc + jnp.dot(w, v_ref[pl.ds(st, tq), :], preferred_element_type=F32)
                 for acc, w, st in zip(accs, ws, starts)]
        new_c = [carries[c] + sums[c * tq:(c + 1) * tq, tq:] for c in range(len(js))]
        return new_c, new_a

    def q_group(gi, _):
        qi = [gi * nchain + c for c in range(nchain)]
        qstart = [pl.multiple_of(i * tq, tq) for i in qi]
        qb = [q_ref[pl.ds(st, tq), :] for st in qstart]

        def retire(t, carries):
            carries = [jnp.where(qi[c] - t >= 0, carries[c], jnp.inf) for c in range(nchain)]
            return carries, jnp.min(functools.reduce(jnp.minimum, carries))

        def back(t):
            return [jnp.maximum(qi[c] - t, 0) for c in range(nchain)]

        zs = [scores(qb, back(t)) for t in range(SB_EAGER_STEPS + 1)]
        carries, accs = key_blocks(zs[0], qi, [jnp.zeros((tq, tq), F32)] * nchain,
                                   [jnp.zeros((tq, dh), F32)] * nchain, True)
        carries, low = retire(1, carries)
        for t in range(1, SB_EAGER_STEPS + 1):
            carries, accs = key_blocks(zs[t], back(t), carries, accs, False)
            carries, low = retire(t + 1, carries)

        def cond(st):
            return st[1] < -SB_LOG_WEIGHT_FLOOR

        def body(st):
            t, _, carries, accs = st
            carries, accs = key_blocks(scores(qb, back(t)), back(t), carries, accs, False)
            carries, low = retire(t + 1, carries)
            return t + 1, low, carries, accs

        _, _, _, accs = lax.while_loop(cond, body, (SB_EAGER_STEPS + 1, low, carries, accs))
        for c in range(nchain):
            o_ref[pl.ds(qstart[c], tq), :] = accs[c].astype(o_ref.dtype)
        return 0

    lax.fori_loop(0, ngroups, q_group, 0)


def _sb_attn(p, dims, batch, seq, tq, nchain):
    h, dh = dims.sb_heads, dims.sb_head_dim
    m = batch * seq
    spec = lambda off: pl.BlockSpec((None, seq, dh), lambda b, hh: (off + hh, b, 0))
    return pl.pallas_call(
        functools.partial(_sb_attn_kernel, tq=tq, nchain=nchain),
        grid=(batch, h),
        in_specs=[spec(0), spec(h), spec(2 * h)],
        out_specs=pl.BlockSpec((seq, dh), lambda b, hh: (b, hh)),
        out_shape=jax.ShapeDtypeStruct((m, h * dh), BF16),
        compiler_params=_params("parallel", "parallel"),
        name="sb_attn",
    )(p, p, p)


def _ssd_kernel(x_ref, b_ref, c_ref, z_ref, dt_ref, wx_ref, wb_ref, wc_ref, bx_ref, bb_ref, bc_ref,
                dtb_ref, alog_ref, dskip_ref, norm_ref, y_ref, halo_scr, state_scr, *, hd, kconv):
    nxb, nbatch, l, _ = x_ref.shape
    hg = dt_ref.shape[1]
    inner = nxb * LANE
    n_state = b_ref.shape[-1]
    assert l == LANE and 2 * hd == LANE and n_state == LANE

    @pl.when(pl.program_id(1) == 0)
    def _():
        halo_scr[...] = jnp.zeros_like(halo_scr)
        state_scr[...] = jnp.zeros_like(state_scr)

    rows = functools.partial(_rows, n=l)

    wcat = jnp.concatenate([wx_ref[...], wb_ref[...], wc_ref[...]], axis=1)
    taps = [rows(wcat[k:k + 1, :]) for k in range(kconv)]
    bcat = rows(jnp.concatenate([bx_ref[...], bb_ref[...], bc_ref[...]], axis=1))
    dskip = rows(dskip_ref[...])
    norm_g = rows(norm_ref[...])
    neg_a = -jnp.exp(alog_ref[...])
    lane = lax.broadcasted_iota(jnp.int32, (l, LANE), 1)
    lane_t = lax.broadcasted_iota(jnp.int32, (hg, l), 1)
    low_half = lane < hd
    tril = lane <= lax.broadcasted_iota(jnp.int32, (l, l), 0)
    pad = jnp.zeros((LANE - hg, l), F32)

    for b in range(nbatch):
        raw = jnp.concatenate([x_ref[c, b] for c in range(nxb)] + [b_ref[b], c_ref[b]], axis=1).astype(F32)
        xp = jnp.concatenate([halo_scr[b], raw], axis=0)
        halo_scr[b] = raw[l - 8:, :]
        conv = bcat
        for k in range(kconv):
            off = 8 - (kconv - 1) + k
            conv = conv + taps[k] * xp[off:off + l, :]
        act = _silu(conv)
        x16 = act[:, :inner].astype(BF16)
        bm = act[:, inner:inner + n_state]
        cm = act[:, inner + n_state:]

        dt = _softplus(dt_ref[b] + dtb_ref[...])
        cs = dt * neg_a
        sh = 1
        while sh < l:
            cs = cs + jnp.where(lane_t >= sh, pltpu.roll(cs, sh, axis=1), 0.0)
            sh *= 2
        cs_last = jnp.broadcast_to(cs[:, l - 1:l], (hg, l))
        src_row = cs - jnp.log(dt)
        coef_row = dt * jnp.exp(cs_last - cs)
        keep_row = jnp.exp(cs_last)
        cs_col = jnp.concatenate([cs, pad], axis=0).T

        cb = lax.dot_general(cm.astype(BF16), bm.astype(BF16), (((1,), (1,)), ((), ())),
                             preferred_element_type=F32)
        bm_t = bm.T
        state = state_scr[b]
        y_blocks, new_state = [], []
        for blk in range(nxb):
            lhs, scaled_bt = [], []
            for h in (2 * blk, 2 * blk + 1):
                ccol = jnp.broadcast_to(cs_col[:, h:h + 1], (l, LANE))
                seg = ccol - rows(src_row[h:h + 1, :])
                w_in = (cb * jnp.exp(jnp.where(tril, seg, -jnp.inf))).astype(BF16)
                w_prev = (cm * jnp.exp(ccol)).astype(BF16)
                lhs.append(jnp.concatenate([w_in, w_prev], axis=1))
                scaled_bt.append((bm_t * rows(coef_row[h:h + 1, :])).astype(BF16))
            xblk = x16[:, blk * LANE:(blk + 1) * LANE]
            sblk = state[:, blk * LANE:(blk + 1) * LANE]
            rhs = jnp.concatenate([xblk, sblk.astype(BF16)], axis=0)
            yy = jnp.dot(jnp.concatenate(lhs, axis=0), rhs, preferred_element_type=F32)
            y_blocks.append(jnp.where(low_half, yy[:l], yy[l:]))
            ss = jnp.dot(jnp.concatenate(scaled_bt, axis=0), xblk, preferred_element_type=F32)
            keep = jnp.where(low_half[:1], keep_row[2 * blk:2 * blk + 1, :], keep_row[2 * blk + 1:2 * blk + 2, :])
            new_state.append(sblk * rows(keep) + jnp.where(low_half, ss[:n_state], ss[n_state:]))
        state_scr[b] = jnp.concatenate(new_state, axis=1)
        y = jnp.concatenate(y_blocks, axis=1)

        y = y + act[:, :inner] * dskip
        zf = jnp.concatenate([z_ref[c, b] for c in range(nxb)], axis=1).astype(F32)
        y = y * _silu(zf)
        ms = jnp.mean(y * y, axis=-1, keepdims=True)
        y = y * lax.rsqrt(ms + EPS) * norm_g
        y_ref[b] = y.astype(y_ref.dtype)


def _ssd(p, dt_t, conv_w, conv_b, dtb, alog, dskip_e, norm_g, dims, batch, seq, chunk, blk_off):
    g = dims.ssd_groups
    inner_g = dims.ssd_inner // g
    nxb = inner_g // LANE
    hg = dims.ssd_heads // g
    n_state = dims.ssd_state
    nt = seq // chunk
    kconv = conv_w.shape[0]
    z_off, x_off, b_off, c_off = blk_off
    p4 = p.reshape(p.shape[0], batch, seq, LANE)
    sb_off = dims.ssd_inner // n_state
    in_specs = [
        pl.BlockSpec((nxb, batch, chunk, LANE), lambda gg, t: (x_off // nxb + gg, 0, t, 0)),
        pl.BlockSpec((None, batch, chunk, LANE), lambda gg, t: (b_off + gg, 0, t, 0)),
        pl.BlockSpec((None, batch, chunk, LANE), lambda gg, t: (c_off + gg, 0, t, 0)),
        pl.BlockSpec((nxb, batch, chunk, LANE), lambda gg, t: (z_off // nxb + gg, 0, t, 0)),
        pl.BlockSpec((batch, hg, chunk), lambda gg, t: (0, gg, t)),
        pl.BlockSpec((kconv, inner_g), lambda gg, t: (0, gg)),
        pl.BlockSpec((kconv, n_state), lambda gg, t: (0, sb_off + gg)),
        pl.BlockSpec((kconv, n_state), lambda gg, t: (0, sb_off + g + gg)),
        pl.BlockSpec((1, inner_g), lambda gg, t: (0, gg)),
        pl.BlockSpec((1, n_state), lambda gg, t: (0, sb_off + gg)),
        pl.BlockSpec((1, n_state), lambda gg, t: (0, sb_off + g + gg)),
        pl.BlockSpec((hg, LANE), lambda gg, t: (gg, 0)),
        pl.BlockSpec((hg, LANE), lambda gg, t: (gg, 0)),
        pl.BlockSpec((1, inner_g), lambda gg, t: (0, gg)),
        pl.BlockSpec((1, inner_g), lambda gg, t: (0, gg)),
    ]
    y = pl.pallas_call(
        functools.partial(_ssd_kernel, hd=dims.ssd_head_dim, kconv=kconv),
        grid=(g, nt),
        in_specs=in_specs,
        out_specs=pl.BlockSpec((batch, chunk, inner_g), lambda gg, t: (0, t, gg)),
        out_shape=jax.ShapeDtypeStruct((batch, seq, dims.ssd_inner), BF16),
        scratch_shapes=[pltpu.VMEM((batch, 8, inner_g + 2 * n_state), F32),
                        pltpu.VMEM((batch, n_state, inner_g), F32)],
        compiler_params=_params("parallel", "arbitrary"),
        name="ssd",
    )(p4, p4, p4, p4, dt_t, conv_w, conv_w, conv_w, conv_b, conv_b, conv_b, dtb, alog, dskip_e, norm_g)
    return y.reshape(batch * seq, dims.ssd_inner)


def _merge_kernel(o_ref, y_ref, gsb_ref, gssd_ref, wsb_ref, wssd_ref, out_ref):
    a = jnp.dot(o_ref[...], wsb_ref[...], preferred_element_type=F32)
    b = jnp.dot(y_ref[...], wssd_ref[...], preferred_element_type=F32)
    gsb = jnp.concatenate([gsb_ref[c] for c in range(gsb_ref.shape[0])], axis=1).astype(F32)
    gssd = jnp.concatenate([gssd_ref[c] for c in range(gssd_ref.shape[0])], axis=1).astype(F32)
    out_ref[...] = (jax.nn.sigmoid(gsb) * a + jax.nn.sigmoid(gssd) * b).astype(out_ref.dtype)


def _merge(o_sb, y, p, w_sb, w_ssd, gsb_off, gssd_off, tm, tn):
    m, ko = o_sb.shape
    ky = y.shape[1]
    d = w_sb.shape[1]
    nb = tn // LANE
    return pl.pallas_call(
        _merge_kernel,
        grid=(m // tm, d // tn),
        in_specs=[
            pl.BlockSpec((tm, ko), lambda i, j: (i, 0)),
            pl.BlockSpec((tm, ky), lambda i, j: (i, 0)),
            pl.BlockSpec((nb, tm, LANE), lambda i, j: (gsb_off // nb + j, i, 0)),
            pl.BlockSpec((nb, tm, LANE), lambda i, j: (gssd_off // nb + j, i, 0)),
            pl.BlockSpec((ko, tn), lambda i, j: (0, j)),
            pl.BlockSpec((ky, tn), lambda i, j: (0, j)),
        ],
        out_specs=pl.BlockSpec((tm, tn), lambda i, j: (i, j)),
        out_shape=jax.ShapeDtypeStruct((m, d), BF16),
        compiler_params=_params("parallel", "arbitrary"),
        name="merge",
    )(o_sb, y, p, p, w_sb, w_ssd)


def _proj_norm_res_kernel(a_ref, w_ref, g_ref, x_ref, out_ref, f_scr):
    nj, tm, tn = f_scr.shape
    j = pl.program_id(1)
    f_scr[j] = jnp.dot(a_ref[...], w_ref[...], preferred_element_type=F32)

    @pl.when(j == nj - 1)
    def _():
        ss = jnp.zeros((tm, 1), F32)
        for c in range(nj):
            f = f_scr[c]
            ss = ss + jnp.sum(f * f, axis=-1, keepdims=True)
        inv = lax.rsqrt(ss * (1.0 / (nj * tn)) + EPS)
        for c in range(nj):
            cols = slice(c * tn, (c + 1) * tn)
            out_ref[:, cols] = x_ref[:, cols] + f_scr[c] * inv * _rows(g_ref[:, cols], tm)


def _proj_norm_res(a, w, gain, x2, tm, tn, name):
    m, kdim = a.shape
    d = w.shape[1]
    return pl.pallas_call(
        _proj_norm_res_kernel,
        grid=(m // tm, d // tn),
        in_specs=[
            pl.BlockSpec((tm, kdim), lambda i, j: (i, 0)),
            pl.BlockSpec((kdim, tn), lambda i, j: (0, j)),
            pl.BlockSpec((1, d), lambda i, j: (0, 0)),
            pl.BlockSpec((tm, d), lambda i, j: (i, 0)),
        ],
        out_specs=pl.BlockSpec((tm, d), lambda i, j: (i, 0)),
        out_shape=jax.ShapeDtypeStruct((m, d), F32),
        scratch_shapes=[pltpu.VMEM((d // tn, tm, tn), F32)],
        compiler_params=_params("parallel", "arbitrary"),
        name=name,
    )(a, w, gain, x2)


FFN_HALO = 16


def _ffn_up_kernel(x_ref, halo_ref, g_ref, wg_ref, wv_ref, cwg_ref, cwv_ref, cbg_ref, cbv_ref, out_ref, h_scr,
                   *, tiles_per_seq, kconv):
    tm = x_ref.shape[0]

    def norm(v):
        ms = jnp.mean(v * v, axis=-1, keepdims=True)
        return (v * lax.rsqrt(ms + EPS) * g_ref[...]).astype(BF16)

    @pl.when(pl.program_id(1) == 0)
    def _():
        seq_start = (pl.program_id(0) % tiles_per_seq) == 0
        h_scr[:FFN_HALO, :] = jnp.where(seq_start, jnp.zeros((), BF16), norm(halo_ref[...]))
        h_scr[FFN_HALO:, :] = norm(x_ref[...])

    h = h_scr[...]

    def conv(w_ref, cw_ref, cb_ref):
        up = jnp.dot(h, w_ref[...], preferred_element_type=F32)
        u = _rows(cb_ref[...], tm)
        for k in range(kconv):
            off = FFN_HALO - (kconv - 1) + k
            u = u + _rows(cw_ref[k:k + 1, :], tm) * up[off:off + tm, :]
        return u

    gate = conv(wg_ref, cwg_ref, cbg_ref)
    val = conv(wv_ref, cwv_ref, cbv_ref)
    c = math.sqrt(2.0 / math.pi)
    half = 0.5 * gate
    act = half + half * jnp.tanh(gate * (c + (c * 0.044715) * (gate * gate)))
    out_ref[...] = (act * val).astype(out_ref.dtype)


def _ffn_up(x1, gain, w_up, conv_w, conv_b, seq, tm, tn):
    m, d = x1.shape
    dff = w_up.shape[1] // 2
    nj = dff // tn
    kconv = conv_w.shape[0]
    hb = tm // FFN_HALO
    return pl.pallas_call(
        functools.partial(_ffn_up_kernel, tiles_per_seq=seq // tm, kconv=kconv),
        grid=(m // tm, nj),
        in_specs=[
            pl.BlockSpec((tm, d), lambda i, j: (i, 0)),
            pl.BlockSpec((FFN_HALO, d), lambda i, j: (jnp.maximum(i * hb - 1, 0), 0)),
            pl.BlockSpec((1, d), lambda i, j: (0, 0)),
            pl.BlockSpec((d, tn), lambda i, j: (0, j)),
            pl.BlockSpec((d, tn), lambda i, j: (0, nj + j)),
            pl.BlockSpec((kconv, tn), lambda i, j: (0, j)),
            pl.BlockSpec((kconv, tn), lambda i, j: (0, nj + j)),
            pl.BlockSpec((1, tn), lambda i, j: (0, j)),
            pl.BlockSpec((1, tn), lambda i, j: (0, nj + j)),
        ],
        out_specs=pl.BlockSpec((tm, tn), lambda i, j: (i, j)),
        out_shape=jax.ShapeDtypeStruct((m, dff), BF16),
        scratch_shapes=[pltpu.VMEM((FFN_HALO + tm, d), BF16)],
        compiler_params=_params("parallel", "arbitrary"),
        name="ffn_up",
    )(x1, x1, gain, w_up, w_up, conv_w, conv_w, conv_b, conv_b)


def _largest_tile(total, want, quantum=LANE):
    t = min(want, total)
    while total % t or t % quantum:
        t -= quantum
    return t


def _block(x, norm_mix_pre, w_in, ssd_conv_w, ssd_conv_b, dt_bias, a_log, d_skip, ssd_norm, w_sb_proj,
           w_ssd_proj, w_out, norm_mix_post, norm_ffn_pre, w_up, ffn_conv_w, ffn_conv_b, w_down, norm_ffn_post,
           dims, tiles):
    batch, seq, d = x.shape
    m = batch * seq
    sbw, inner, xbc, heads = dims.sb_width, dims.ssd_inner, dims.ssd_xbc, dims.ssd_heads
    x2 = x.reshape(m, d)
    row = lambda v: v.reshape(1, -1)

    dt_lo = 3 * sbw + inner + xbc
    w_in_t = w_in.T
    row_scale = jnp.where(jnp.arange(dt_lo) < sbw, dims.sb_head_dim ** -0.5, 1.0).astype(F32)[:, None]
    w_a = (w_in_t[:dt_lo] * row_scale).astype(BF16)
    w_b = w_in_t[dt_lo + heads:].astype(BF16)
    w_dt = w_in_t[dt_lo:dt_lo + heads]
    z_off = 3 * sbw // LANE
    x_off = z_off + inner // LANE
    b_off = x_off + inner // LANE
    c_off = b_off + dims.ssd_groups * dims.ssd_state // LANE
    gsb_off = c_off + dims.ssd_groups * dims.ssd_state // LANE
    gssd_off = gsb_off + d // LANE

    tm = _largest_tile(seq, tiles["tm"], 16)
    p, dt_t = _in_proj(x2, row(norm_mix_pre), w_a, w_b, w_dt, heads, seq, tm,
                       _largest_tile(math.gcd(w_a.shape[0], w_b.shape[0]), tiles["tn_in"]))

    o_sb = _sb_attn(p, dims, batch, seq, min(tiles["tq"], seq), tiles["sb_chains"])

    chunk = min(tiles["chunk"], seq)
    bcast = lambda v: jnp.broadcast_to(v.reshape(-1, 1), (heads, LANE))
    y = _ssd(p, dt_t, ssd_conv_w, row(ssd_conv_b), bcast(dt_bias), bcast(a_log),
             row(jnp.repeat(d_skip, dims.ssd_head_dim)), row(ssd_norm), dims, batch, seq, chunk,
             (z_off, x_off, b_off, c_off))

    merged = _merge(o_sb, y, p, w_sb_proj.astype(BF16), w_ssd_proj.astype(BF16), gsb_off, gssd_off,
                    _largest_tile(m, tiles["tm_merge"], 16), _largest_tile(d, tiles["tn_merge"]))
    tm2 = _largest_tile(m, tiles["tm_res"], 16)
    x1 = _proj_norm_res(merged, w_out.astype(BF16), row(norm_mix_post), x2, tm2, d, "out_proj")

    tm_up = _largest_tile(seq, tiles["tm"], 16)
    act = _ffn_up(x1, row(norm_ffn_pre), w_up.astype(BF16), ffn_conv_w, row(ffn_conv_b), seq, tm_up,
                  _largest_tile(dims.d_ff, tiles["tn_up"]))
    out = _proj_norm_res(act, w_down.astype(BF16), row(norm_ffn_post), x1, tm2,
                         _largest_tile(d, tiles["tn_down"]), "ffn_down")
    return out.reshape(batch, seq, d)


TILES = dict(tm=1024, tn_in=1024, tq=128, sb_chains=8, chunk=128, tm_merge=1024, tn_merge=512, tm_res=512, tn_up=512, tn_down=512)


def kernel(x, norm_mix_pre, w_in, ssd_conv_w, ssd_conv_b, dt_bias, a_log, d_skip, ssd_norm, w_sb_proj, w_ssd_proj,
           w_out, norm_mix_post, norm_ffn_pre, w_up, ffn_conv_w, ffn_conv_b, w_down, norm_ffn_post):
    dims = Dims()
    args = (norm_mix_pre, w_in, ssd_conv_w, ssd_conv_b, dt_bias, a_log, d_skip, ssd_norm, w_sb_proj, w_ssd_proj,
            w_out, norm_mix_post, norm_ffn_pre, w_up, ffn_conv_w, ffn_conv_b, w_down, norm_ffn_post)
    for layer in range(w_in.shape[0]):
        x = _block(x, *(a[layer] for a in args), dims, TILES)
    return x
```

```python
import dataclasses
import functools
import math

import numpy as np

import jax
import jax.numpy as jnp
from jax import lax
from jax.experimental import pallas as pl
from jax.experimental.pallas import tpu as pltpu

F32 = jnp.float32
BF16 = jnp.bfloat16
LANE = 128
EPS = 1e-6


@dataclasses.dataclass(frozen=True)
class Dims:
    d_model: int = 2048
    sb_heads: int = 16
    sb_head_dim: int = 128
    ssd_inner: int = 4096
    ssd_head_dim: int = 64
    ssd_groups: int = 8
    ssd_state: int = 128
    ssd_conv: int = 4
    d_ff: int = 5632
    ffn_conv: int = 3

    @property
    def sb_width(self):
        return self.sb_heads * self.sb_head_dim

    @property
    def ssd_heads(self):
        return self.ssd_inner // self.ssd_head_dim

    @property
    def ssd_xbc(self):
        return self.ssd_inner + 2 * self.ssd_groups * self.ssd_state


SB_LOG_WEIGHT_FLOOR = -110.0
SB_EAGER_STEPS = 2
VMEM_LIMIT = 56 * 1024 * 1024


def _softplus(x):
    return jnp.maximum(x, 0.0) + jnp.log1p(jnp.exp(-jnp.abs(x)))


def _rows(v, n):
    return jnp.tile(jnp.broadcast_to(v, (8, v.shape[1])), (n // 8, 1))


def _silu(x):
    h = 0.5 * x
    return h + h * jnp.tanh(h)


def _params(*sem):
    return pltpu.CompilerParams(dimension_semantics=sem, vmem_limit_bytes=VMEM_LIMIT)


def _in_proj_kernel(x_ref, g_ref, wa_ref, wb_ref, wdt_ref, p_ref, dt_ref, h_scr, *, na):
    j = pl.program_id(1)

    @pl.when(j == 0)
    def _():
        x = x_ref[...]
        ms = jnp.mean(x * x, axis=-1, keepdims=True)
        h = (x * lax.rsqrt(ms + EPS) * g_ref[...]).astype(BF16)
        h_scr[...] = h
        dt_ref[...] = lax.dot_general(wdt_ref[...].astype(BF16), h, (((1,), (1,)), ((), ())),
                                      preferred_element_type=F32)

    def emit(w_ref):
        acc = lax.dot_general(h_scr[...], w_ref[...], (((1,), (1,)), ((), ())), preferred_element_type=F32)
        for c in range(p_ref.shape[0]):
            p_ref[c] = acc[:, c * LANE:(c + 1) * LANE].astype(BF16)

    pl.when(j < na)(lambda: emit(wa_ref))
    pl.when(j >= na)(lambda: emit(wb_ref))


def _in_proj(x2, gain, w_a, w_b, w_dt, hd, seq, tm, tn):
    m, d = x2.shape
    na, nb = w_a.shape[0] // tn, w_b.shape[0] // tn
    n = w_a.shape[0] + w_b.shape[0]
    tps = seq // tm
    return pl.pallas_call(
        functools.partial(_in_proj_kernel, na=na),
        grid=(m // tm, na + nb),
        in_specs=[
            pl.BlockSpec((tm, d), lambda i, j: (i, 0)),
            pl.BlockSpec((1, d), lambda i, j: (0, 0)),
            pl.BlockSpec((tn, d), lambda i, j: (jnp.minimum(j, na - 1), 0)),
            pl.BlockSpec((tn, d), lambda i, j: (jnp.maximum(j - na, 0), 0)),
            pl.BlockSpec((hd, d), lambda i, j: (0, 0)),
        ],
        out_specs=[
            pl.BlockSpec((tn // LANE, tm, LANE), lambda i, j: (j, i, 0)),
            pl.BlockSpec((None, hd, tm), lambda i, j: (i // tps, 0, i % tps)),
        ],
        out_shape=[
            jax.ShapeDtypeStruct((n // LANE, m, LANE), BF16),
            jax.ShapeDtypeStruct((m // seq, hd, seq), F32),
        ],
        scratch_shapes=[pltpu.VMEM((tm, d), BF16)],
        compiler_params=_params("parallel", "arbitrary"),
        name="in_proj",
    )(x2, gain, w_a, w_b, w_dt)


def _sb_attn_kernel(q_ref, k_ref, v_ref, o_ref, *, tq, nchain):
    s, dh = q_ref.shape
    ngroups = s // (tq * nchain)
    row = lax.broadcasted_iota(jnp.int32, (tq, tq), 0)
    col = lax.broadcasted_iota(jnp.int32, (tq, tq), 1)
    causal = col < row
    suffix = jnp.where(row > col, 1.0, 0.0).astype(BF16)
    half = jnp.concatenate([suffix, jnp.ones((tq, tq), BF16)], axis=1)
    sum_rhs = jnp.concatenate([half, half], axis=0)

    def scores(qbs, js):
        return [lax.dot_general(qb, k_ref[pl.ds(pl.multiple_of(j * tq, tq), tq), :], (((1,), (1,)), ((), ())),
                                preferred_element_type=F32) for qb, j in zip(qbs, js)]

    def key_blocks(zs, js, carries, accs, masked):
        starts = [pl.multiple_of(j * tq, tq) for j in js]
        sps, parts = [], []
        for z in zs:
            sp = jnp.maximum(z, 0.0) + jnp.log(1.0 + jnp.exp(-jnp.abs(z)))
            if masked:
                sp = jnp.where(causal, sp, 0.0)
            hi = sp.astype(BF16)
            lo = (sp - hi.astype(F32)).astype(BF16)
            sps.append(sp)
            parts.append(jnp.concatenate([hi, lo], axis=1))
        sums = jnp.dot(jnp.concatenate(parts, axis=0), sum_rhs, preferred_element_type=F32)
        ws = []
        for c, (z, sp) in enumerate(zip(zs, sps)):
            w = jnp.exp(z - sp - sums[c * tq:(c + 1) * tq, :tq] - carries[c])
            if masked:
                w = jnp.where(causal, w, 0.0)
            ws.append(w.astype(BF16))
        new_a = [acc + jnp.dot(w, v_ref[pl.ds(st, tq), :], preferred_element_type=F32)
                 for acc, w, st in zip(accs, ws, starts)]
        new_c = [carries[c] + sums[c * tq:(c + 1) * tq, tq:] for c in range(len(js))]
        return new_c, new_a

    def q_group(gi, _):
        qi = [gi * nchain + c for c in range(nchain)]
        qstart = [pl.multiple_of(i * tq, tq) for i in qi]
        qb = [q_ref[pl.ds(st, tq), :] for st in qstart]

        def retire(t, carries):
            carries = [jnp.where(qi[c] - t >= 0, carries[c], jnp.inf) for c in range(nchain)]
            return carries, jnp.min(functools.reduce(jnp.minimum, carries))

        def back(t):
            return [jnp.maximum(qi[c] - t, 0) for c in range(nchain)]

        zs = [scores(qb, back(t)) for t in range(SB_EAGER_STEPS + 1)]
        carries, accs = key_blocks(zs[0], qi, [jnp.zeros((tq, tq), F32)] * nchain,
                                   [jnp.zeros((tq, dh), F32)] * nchain, True)
        carries, low = retire(1, carries)
        for t in range(1, SB_EAGER_STEPS + 1):
            carries, accs = key_blocks(zs[t], back(t), carries, accs, False)
            carries, low = retire(t + 1, carries)

        def cond(st):
            return st[1] < -SB_LOG_WEIGHT_FLOOR

        def body(st):
            t, _, carries, accs = st
            carries, accs = key_blocks(scores(qb, back(t)), back(t), carries, accs, False)
            carries, low = retire(t + 1, carries)
            return t + 1, low, carries, accs

        _, _, _, accs = lax.while_loop(cond, body, (SB_EAGER_STEPS + 1, low, carries, accs))
        for c in range(nchain):
            o_ref[pl.ds(qstart[c], tq), :] = accs[c].astype(o_ref.dtype)
        return 0

    lax.fori_loop(0, ngroups, q_group, 0)


def _sb_attn(p, dims, batch, seq, tq, nchain):
    h, dh = dims.sb_heads, dims.sb_head_dim
    m = batch * seq
    spec = lambda off: pl.BlockSpec((None, seq, dh), lambda b, hh: (off + hh, b, 0))
    return pl.pallas_call(
        functools.partial(_sb_attn_kernel, tq=tq, nchain=nchain),
        grid=(batch, h),
        in_specs=[spec(0), spec(h), spec(2 * h)],
        out_specs=pl.BlockSpec((seq, dh), lambda b, hh: (b, hh)),
        out_shape=jax.ShapeDtypeStruct((m, h * dh), BF16),
        compiler_params=_params("parallel", "parallel"),
        name="sb_attn",
    )(p, p, p)


def _ssd_kernel(x_ref, b_ref, c_ref, z_ref, dt_ref, wx_ref, wb_ref, wc_ref, bx_ref, bb_ref, bc_ref,
                dtb_ref, alog_ref, dskip_ref, norm_ref, shift_ref, y_ref, halo_scr, state_scr, *, hd, kconv, gps):
    _, nbatch, l, _ = x_ref.shape
    nxb = x_ref.shape[0] // gps
    hg = dt_ref.shape[1] // gps
    inner = nxb * LANE
    n_state = b_ref.shape[-1]
    width = inner + 2 * n_state
    assert l == LANE and 2 * hd == LANE and n_state == LANE

    @pl.when(pl.program_id(1) == 0)
    def _():
        halo_scr[...] = jnp.zeros_like(halo_scr)
        state_scr[...] = jnp.zeros_like(state_scr)

    rows = functools.partial(_rows, n=l)
    lane = lax.broadcasted_iota(jnp.int32, (l, LANE), 1)
    lane_t = lax.broadcasted_iota(jnp.int32, (hg, l), 1)
    low_half = lane < hd
    tril = lane <= lax.broadcasted_iota(jnp.int32, (l, l), 0)
    pad = jnp.zeros((LANE - hg, l), F32)

    gcols = lambda ref, gi, w: ref[:, gi * w:(gi + 1) * w]
    taps8, taps, bcat, dskip, norm_g, neg_a, dtb = [], [], [], [], [], [], []
    for gi in range(gps):
        wcat = jnp.concatenate([gcols(wx_ref, gi, inner), gcols(wb_ref, gi, n_state), gcols(wc_ref, gi, n_state)],
                               axis=1)
        wcat16 = wcat.astype(BF16)
        taps8.append([jnp.broadcast_to(wcat[k:k + 1, :], (8, width)) for k in range(kconv)])
        taps.append([jnp.tile(jnp.broadcast_to(wcat16[k:k + 1, :], (16, width)), (l // 16, 1))
                     for k in range(kconv)])
        bcat.append(rows(jnp.concatenate([gcols(bx_ref, gi, inner), gcols(bb_ref, gi, n_state),
                                          gcols(bc_ref, gi, n_state)], axis=1)))
        dskip.append(rows(gcols(dskip_ref, gi, inner)))
        norm_g.append(rows(gcols(norm_ref, gi, inner)))
        neg_a.append(-jnp.exp(alog_ref[gi * hg:(gi + 1) * hg, :]))
        dtb.append(dtb_ref[gi * hg:(gi + 1) * hg, :])

    chains = [(gi, b) for gi in range(gps) for b in range(nbatch)]

    acts = []
    for gi, b in chains:
        raw = jnp.concatenate([x_ref[gi * nxb + c, b] for c in range(nxb)] + [b_ref[gi, b], c_ref[gi, b]],
                              axis=1)
        body = jnp.dot(shift_ref[...], jnp.concatenate([raw * taps[gi][k] for k in range(kconv)], axis=0),
                       preferred_element_type=F32)
        head = jnp.concatenate([halo_scr[gi, b], raw[:8].astype(F32)], axis=0)
        halo_scr[gi, b] = raw[l - 8:, :].astype(F32)
        first = jnp.zeros((8, width), F32)
        for k in range(kconv):
            off = 8 - (kconv - 1) + k
            first = first + taps8[gi][k] * head[off:off + 8, :]
        acts.append(_silu(jnp.concatenate([first, body[8:]], axis=0) + bcat[gi]))
    x16 = [a[:, :inner].astype(BF16) for a in acts]
    bms = [a[:, inner:inner + n_state] for a in acts]
    cms = [a[:, inner + n_state:] for a in acts]

    cbs = [lax.dot_general(cm.astype(BF16), bm.astype(BF16), (((1,), (1,)), ((), ())),
                           preferred_element_type=F32) for cm, bm in zip(cms, bms)]
    bm_ts = [bm.T for bm in bms]
    src_rows, coef_rows, keep_rows, cs_cols = [], [], [], []
    for gi, b in chains:
        dt = _softplus(dt_ref[b, gi * hg:(gi + 1) * hg, :] + dtb[gi])
        cs = dt * neg_a[gi]
        sh = 1
        while sh < l:
            cs = cs + jnp.where(lane_t >= sh, pltpu.roll(cs, sh, axis=1), 0.0)
            sh *= 2
        cs_last = jnp.broadcast_to(cs[:, l - 1:l], (hg, l))
        src_rows.append(cs - jnp.log(dt))
        coef_rows.append(dt * jnp.exp(cs_last - cs))
        keep_rows.append(jnp.exp(cs_last))
        cs_cols.append(jnp.concatenate([cs, pad], axis=0).T)

    states = [state_scr[gi, b] for gi, b in chains]
    y_blocks = [[] for _ in chains]
    new_state = [[] for _ in chains]
    for blk in range(nxb):
        for n in range(len(chains)):
            lhs, scaled_bt = [], []
            for h in (2 * blk, 2 * blk + 1):
                ccol = jnp.broadcast_to(cs_cols[n][:, h:h + 1], (l, LANE))
                seg = ccol - rows(src_rows[n][h:h + 1, :])
                w_in = (cbs[n] * jnp.exp(jnp.where(tril, seg, -jnp.inf))).astype(BF16)
                w_prev = (cms[n] * jnp.exp(ccol)).astype(BF16)
                lhs.append(jnp.concatenate([w_in, w_prev], axis=1))
                scaled_bt.append((bm_ts[n] * rows(coef_rows[n][h:h + 1, :])).astype(BF16))
            xblk = x16[n][:, blk * LANE:(blk + 1) * LANE]
            sblk = states[n][:, blk * LANE:(blk + 1) * LANE]
            rhs = jnp.concatenate([xblk, sblk.astype(BF16)], axis=0)
            yy = jnp.dot(jnp.concatenate(lhs, axis=0), rhs, preferred_element_type=F32)
            y_blocks[n].append(jnp.where(low_half, yy[:l], yy[l:]))
            ss = jnp.dot(jnp.concatenate(scaled_bt, axis=0), xblk, preferred_element_type=F32)
            kr = keep_rows[n]
            keep = jnp.where(low_half[:1], kr[2 * blk:2 * blk + 1, :], kr[2 * blk + 1:2 * blk + 2, :])
            new_state[n].append(sblk * rows(keep) + jnp.where(low_half, ss[:n_state], ss[n_state:]))

    for n, (gi, b) in enumerate(chains):
        state_scr[gi, b] = jnp.concatenate(new_state[n], axis=1)
        y = jnp.concatenate(y_blocks[n], axis=1) + acts[n][:, :inner] * dskip[gi]
        zf = jnp.concatenate([z_ref[gi * nxb + c, b] for c in range(nxb)], axis=1).astype(F32)
        y = y * _silu(zf)
        ms = jnp.mean(y * y, axis=-1, keepdims=True)
        y = y * lax.rsqrt(ms + EPS) * norm_g[gi]
        y_ref[b, :, gi * inner:(gi + 1) * inner] = y.astype(y_ref.dtype)


def _ssd(p, dt_t, conv_w, conv_b, dtb, alog, dskip_e, norm_g, dims, batch, seq, chunk, blk_off, gps):
    g = dims.ssd_groups
    inner_g = dims.ssd_inner // g
    nxb = inner_g // LANE
    hg = dims.ssd_heads // g
    n_state = dims.ssd_state
    nt = seq // chunk
    kconv = conv_w.shape[0]
    z_off, x_off, b_off, c_off = blk_off
    p4 = p.reshape(p.shape[0], batch, seq, LANE)
    sb_off = dims.ssd_inner // n_state
    assert all(o % (gps * nxb) == 0 for o in (z_off, x_off)) and all(o % gps == 0 for o in (b_off, c_off, sb_off, g))
    shift = np.zeros((chunk, kconv * chunk), np.float32)
    for k in range(kconv):
        t = np.arange(kconv - 1 - k, chunk)
        shift[t, chunk * k + t - (kconv - 1) + k] = 1.0
    shift = jnp.asarray(shift, BF16)
    in_specs = [
        pl.BlockSpec((gps * nxb, batch, chunk, LANE), lambda gg, t: (x_off // (gps * nxb) + gg, 0, t, 0)),
        pl.BlockSpec((gps, batch, chunk, LANE), lambda gg, t: (b_off // gps + gg, 0, t, 0)),
        pl.BlockSpec((gps, batch, chunk, LANE), lambda gg, t: (c_off // gps + gg, 0, t, 0)),
        pl.BlockSpec((gps * nxb, batch, chunk, LANE), lambda gg, t: (z_off // (gps * nxb) + gg, 0, t, 0)),
        pl.BlockSpec((batch, gps * hg, chunk), lambda gg, t: (0, gg, t)),
        pl.BlockSpec((kconv, gps * inner_g), lambda gg, t: (0, gg)),
        pl.BlockSpec((kconv, gps * n_state), lambda gg, t: (0, sb_off // gps + gg)),
        pl.BlockSpec((kconv, gps * n_state), lambda gg, t: (0, (sb_off + g) // gps + gg)),
        pl.BlockSpec((1, gps * inner_g), lambda gg, t: (0, gg)),
        pl.BlockSpec((1, gps * n_state), lambda gg, t: (0, sb_off // gps + gg)),
        pl.BlockSpec((1, gps * n_state), lambda gg, t: (0, (sb_off + g) // gps + gg)),
        pl.BlockSpec((gps * hg, LANE), lambda gg, t: (gg, 0)),
        pl.BlockSpec((gps * hg, LANE), lambda gg, t: (gg, 0)),
        pl.BlockSpec((1, gps * inner_g), lambda gg, t: (0, gg)),
        pl.BlockSpec((1, gps * inner_g), lambda gg, t: (0, gg)),
        pl.BlockSpec(shift.shape, lambda gg, t: (0, 0)),
    ]
    y = pl.pallas_call(
        functools.partial(_ssd_kernel, hd=dims.ssd_head_dim, kconv=kconv, gps=gps),
        grid=(g // gps, nt),
        in_specs=in_specs,
        out_specs=pl.BlockSpec((batch, chunk, gps * inner_g), lambda gg, t: (0, t, gg)),
        out_shape=jax.ShapeDtypeStruct((batch, seq, dims.ssd_inner), BF16),
        scratch_shapes=[pltpu.VMEM((gps, batch, 8, inner_g + 2 * n_state), F32),
                        pltpu.VMEM((gps, batch, n_state, inner_g), F32)],
        compiler_params=_params("parallel", "arbitrary"),
        name="ssd",
    )(p4, p4, p4, p4, dt_t, conv_w, conv_w, conv_w, conv_b, conv_b, conv_b, dtb, alog, dskip_e, norm_g, shift)
    return y.reshape(batch * seq, dims.ssd_inner)


def _merge_kernel(o_ref, y_ref, gsb_ref, gssd_ref, wsb_ref, wssd_ref, out_ref):
    a = jnp.dot(o_ref[...], wsb_ref[...], preferred_element_type=F32)
    b = jnp.dot(y_ref[...], wssd_ref[...], preferred_element_type=F32)
    gsb = jnp.concatenate([gsb_ref[c] for c in range(gsb_ref.shape[0])], axis=1).astype(F32)
    gssd = jnp.concatenate([gssd_ref[c] for c in range(gssd_ref.shape[0])], axis=1).astype(F32)
    out_ref[...] = (jax.nn.sigmoid(gsb) * a + jax.nn.sigmoid(gssd) * b).astype(out_ref.dtype)


def _merge(o_sb, y, p, w_sb, w_ssd, gsb_off, gssd_off, tm, tn):
    m, ko = o_sb.shape
    ky = y.shape[1]
    d = w_sb.shape[1]
    nb = tn // LANE
    return pl.pallas_call(
        _merge_kernel,
        grid=(m // tm, d // tn),
        in_specs=[
            pl.BlockSpec((tm, ko), lambda i, j: (i, 0)),
            pl.BlockSpec((tm, ky), lambda i, j: (i, 0)),
            pl.BlockSpec((nb, tm, LANE), lambda i, j: (gsb_off // nb + j, i, 0)),
            pl.BlockSpec((nb, tm, LANE), lambda i, j: (gssd_off // nb + j, i, 0)),
            pl.BlockSpec((ko, tn), lambda i, j: (0, j)),
            pl.BlockSpec((ky, tn), lambda i, j: (0, j)),
        ],
        out_specs=pl.BlockSpec((tm, tn), lambda i, j: (i, j)),
        out_shape=jax.ShapeDtypeStruct((m, d), BF16),
        compiler_params=_params("parallel", "arbitrary"),
        name="merge",
    )(o_sb, y, p, p, w_sb, w_ssd)


def _proj_norm_res_kernel(a_ref, w_ref, g_ref, x_ref, out_ref, f_scr):
    nj, tm, tn = f_scr.shape
    j = pl.program_id(1)
    f_scr[j] = jnp.dot(a_ref[...], w_ref[...], preferred_element_type=F32)

    @pl.when(j == nj - 1)
    def _():
        ss = jnp.zeros((tm, 1), F32)
        for c in range(nj):
            f = f_scr[c]
            ss = ss + jnp.sum(f * f, axis=-1, keepdims=True)
        inv = lax.rsqrt(ss * (1.0 / (nj * tn)) + EPS)
        for c in range(nj):
            cols = slice(c * tn, (c + 1) * tn)
            out_ref[:, cols] = x_ref[:, cols] + f_scr[c] * inv * _rows(g_ref[:, cols], tm)


def _proj_norm_res(a, w, gain, x2, tm, tn, name):
    m, kdim = a.shape
    d = w.shape[1]
    return pl.pallas_call(
        _proj_norm_res_kernel,
        grid=(m // tm, d // tn),
        in_specs=[
            pl.BlockSpec((tm, kdim), lambda i, j: (i, 0)),
            pl.BlockSpec((kdim, tn), lambda i, j: (0, j)),
            pl.BlockSpec((1, d), lambda i, j: (0, 0)),
            pl.BlockSpec((tm, d), lambda i, j: (i, 0)),
        ],
        out_specs=pl.BlockSpec((tm, d), lambda i, j: (i, 0)),
        out_shape=jax.ShapeDtypeStruct((m, d), F32),
        scratch_shapes=[pltpu.VMEM((d // tn, tm, tn), F32)],
        compiler_params=_params("parallel", "arbitrary"),
        name=name,
    )(a, w, gain, x2)


FFN_HALO = 16


def _ffn_up_kernel(x_ref, halo_ref, g_ref, wg_ref, wv_ref, cwg_ref, cwv_ref, cbg_ref, cbv_ref, out_ref, h_scr,
                   *, tiles_per_seq, kconv):
    tm = x_ref.shape[0]

    def norm(v):
        ms = jnp.mean(v * v, axis=-1, keepdims=True)
        return (v * lax.rsqrt(ms + EPS) * g_ref[...]).astype(BF16)

    @pl.when(pl.program_id(1) == 0)
    def _():
        seq_start = (pl.program_id(0) % tiles_per_seq) == 0
        h_scr[:FFN_HALO, :] = jnp.where(seq_start, jnp.zeros((), BF16), norm(halo_ref[...]))
        h_scr[FFN_HALO:, :] = norm(x_ref[...])

    h = h_scr[...]

    def conv(w_ref, cw_ref, cb_ref):
        up = jnp.dot(h, w_ref[...], preferred_element_type=F32)
        u = _rows(cb_ref[...], tm)
        for k in range(kconv):
            off = FFN_HALO - (kconv - 1) + k
            u = u + _rows(cw_ref[k:k + 1, :], tm) * up[off:off + tm, :]
        return u

    gate = conv(wg_ref, cwg_ref, cbg_ref)
    val = conv(wv_ref, cwv_ref, cbv_ref)
    c = math.sqrt(2.0 / math.pi)
    half = 0.5 * gate
    act = half + half * jnp.tanh(gate * (c + (c * 0.044715) * (gate * gate)))
    out_ref[...] = (act * val).astype(out_ref.dtype)


def _ffn_up(x1, gain, w_up, conv_w, conv_b, seq, tm, tn):
    m, d = x1.shape
    dff = w_up.shape[1] // 2
    nj = dff // tn
    kconv = conv_w.shape[0]
    hb = tm // FFN_HALO
    return pl.pallas_call(
        functools.partial(_ffn_up_kernel, tiles_per_seq=seq // tm, kconv=kconv),
        grid=(m // tm, nj),
        in_specs=[
            pl.BlockSpec((tm, d), lambda i, j: (i, 0)),
            pl.BlockSpec((FFN_HALO, d), lambda i, j: (jnp.maximum(i * hb - 1, 0), 0)),
            pl.BlockSpec((1, d), lambda i, j: (0, 0)),
            pl.BlockSpec((d, tn), lambda i, j: (0, j)),
            pl.BlockSpec((d, tn), lambda i, j: (0, nj + j)),
            pl.BlockSpec((kconv, tn), lambda i, j: (0, j)),
            pl.BlockSpec((kconv, tn), lambda i, j: (0, nj + j)),
            pl.BlockSpec((1, tn), lambda i, j: (0, j)),
            pl.BlockSpec((1, tn), lambda i, j: (0, nj + j)),
        ],
        out_specs=pl.BlockSpec((tm, tn), lambda i, j: (i, j)),
        out_shape=jax.ShapeDtypeStruct((m, dff), BF16),
        scratch_shapes=[pltpu.VMEM((FFN_HALO + tm, d), BF16)],
        compiler_params=_params("parallel", "arbitrary"),
        name="ffn_up",
    )(x1, x1, gain, w_up, w_up, conv_w, conv_w, conv_b, conv_b)


def _largest_tile(total, want, quantum=LANE):
    t = min(want, total)
    while total % t or t % quantum:
        t -= quantum
    return t


def _block(x, norm_mix_pre, w_in, ssd_conv_w, ssd_conv_b, dt_bias, a_log, d_skip, ssd_norm, w_sb_proj,
           w_ssd_proj, w_out, norm_mix_post, norm_ffn_pre, w_up, ffn_conv_w, ffn_conv_b, w_down, norm_ffn_post,
           dims, tiles):
    batch, seq, d = x.shape
    m = batch * seq
    sbw, inner, xbc, heads = dims.sb_width, dims.ssd_inner, dims.ssd_xbc, dims.ssd_heads
    x2 = x.reshape(m, d)
    row = lambda v: v.reshape(1, -1)

    dt_lo = 3 * sbw + inner + xbc
    w_in_t = w_in.T
    row_scale = jnp.where(jnp.arange(dt_lo) < sbw, dims.sb_head_dim ** -0.5, 1.0).astype(F32)[:, None]
    w_a = (w_in_t[:dt_lo] * row_scale).astype(BF16)
    w_b = w_in_t[dt_lo + heads:].astype(BF16)
    w_dt = w_in_t[dt_lo:dt_lo + heads]
    z_off = 3 * sbw // LANE
    x_off = z_off + inner // LANE
    b_off = x_off + inner // LANE
    c_off = b_off + dims.ssd_groups * dims.ssd_state // LANE
    gsb_off = c_off + dims.ssd_groups * dims.ssd_state // LANE
    gssd_off = gsb_off + d // LANE

    tm = _largest_tile(seq, tiles["tm"], 16)
    p, dt_t = _in_proj(x2, row(norm_mix_pre), w_a, w_b, w_dt, heads, seq, tm,
                       _largest_tile(math.gcd(w_a.shape[0], w_b.shape[0]), tiles["tn_in"]))

    o_sb = _sb_attn(p, dims, batch, seq, min(tiles["tq"], seq), tiles["sb_chains"])

    chunk = min(tiles["chunk"], seq)
    bcast = lambda v: jnp.broadcast_to(v.reshape(-1, 1), (heads, LANE))
    y = _ssd(p, dt_t, ssd_conv_w, row(ssd_conv_b), bcast(dt_bias), bcast(a_log),
             row(jnp.repeat(d_skip, dims.ssd_head_dim)), row(ssd_norm), dims, batch, seq, chunk,
             (z_off, x_off, b_off, c_off), tiles["ssd_groups_per_step"])

    merged = _merge(o_sb, y, p, w_sb_proj.astype(BF16), w_ssd_proj.astype(BF16), gsb_off, gssd_off,
                    _largest_tile(m, tiles["tm_merge"], 16), _largest_tile(d, tiles["tn_merge"]))
    tm2 = _largest_tile(m, tiles["tm_res"], 16)
    x1 = _proj_norm_res(merged, w_out.astype(BF16), row(norm_mix_post), x2, tm2, d, "out_proj")

    tm_up = _largest_tile(seq, tiles["tm"], 16)
    act = _ffn_up(x1, row(norm_ffn_pre), w_up.astype(BF16), ffn_conv_w, row(ffn_conv_b), seq, tm_up,
                  _largest_tile(dims.d_ff, tiles["tn_up"]))
    out = _proj_norm_res(act, w_down.astype(BF16), row(norm_ffn_post), x1, tm2,
                         _largest_tile(d, tiles["tn_down"]), "ffn_down")
    return out.reshape(batch, seq, d)


TILES = dict(tm=1024, tn_in=1024, tq=128, sb_chains=8, chunk=128, ssd_groups_per_step=4, tm_merge=1024, tn_merge=512, tm_res=512, tn_up=512, tn_down=512)


def kernel(x, norm_mix_pre, w_in, ssd_conv_w, ssd_conv_b, dt_bias, a_log, d_skip, ssd_norm, w_sb_proj, w_ssd_proj,
           w_out, norm_mix_post, norm_ffn_pre, w_up, ffn_conv_w, ffn_conv_b, w_down, norm_ffn_post):
    dims = Dims()
    args = (norm_mix_pre, w_in, ssd_conv_w, ssd_conv_b, dt_bias, a_log, d_skip, ssd_norm, w_sb_proj, w_ssd_proj,
            w_out, norm_mix_post, norm_ffn_pre, w_up, ffn_conv_w, ffn_conv_b, w_down, norm_ffn_post)
    for layer in range(w_in.shape[0]):
        x = _block(x, *(a[layer] for a in args), dims, TILES)
    return x
```

```python
import dataclasses
import functools
import math

import numpy as np

import jax
import jax.numpy as jnp
from jax import lax
from jax.experimental import pallas as pl
from jax.experimental.pallas import tpu as pltpu

F32 = jnp.float32
BF16 = jnp.bfloat16
LANE = 128
EPS = 1e-6


@dataclasses.dataclass(frozen=True)
class Dims:
    d_model: int = 2048
    sb_heads: int = 16
    sb_head_dim: int = 128
    ssd_inner: int = 4096
    ssd_head_dim: int = 64
    ssd_groups: int = 8
    ssd_state: int = 128
    ssd_conv: int = 4
    d_ff: int = 5632
    ffn_conv: int = 3

    @property
    def sb_width(self):
        return self.sb_heads * self.sb_head_dim

    @property
    def ssd_heads(self):
        return self.ssd_inner // self.ssd_head_dim

    @property
    def ssd_xbc(self):
        return self.ssd_inner + 2 * self.ssd_groups * self.ssd_state


SB_LOG_WEIGHT_FLOOR = -110.0
SB_EAGER_STEPS = 2
VMEM_LIMIT = 56 * 1024 * 1024


def _softplus(x):
    return jnp.maximum(x, 0.0) + jnp.log1p(jnp.exp(-jnp.abs(x)))


def _rows(v, n):
    return jnp.tile(jnp.broadcast_to(v, (8, v.shape[1])), (n // 8, 1))


def _silu(x):
    h = 0.5 * x
    return h + h * jnp.tanh(h)


def _params(*sem):
    return pltpu.CompilerParams(dimension_semantics=sem, vmem_limit_bytes=VMEM_LIMIT)


def _in_proj_kernel(x_ref, g_ref, wa_ref, wb_ref, wdt_ref, p_ref, dt_ref, h_scr, *, na):
    j = pl.program_id(1)

    @pl.when(j == 0)
    def _():
        x = x_ref[...]
        ms = jnp.mean(x * x, axis=-1, keepdims=True)
        h = (x * lax.rsqrt(ms + EPS) * g_ref[...]).astype(BF16)
        h_scr[...] = h
        dt_ref[...] = lax.dot_general(wdt_ref[...].astype(BF16), h, (((1,), (1,)), ((), ())),
                                      preferred_element_type=F32)

    def emit(w_ref):
        acc = lax.dot_general(h_scr[...], w_ref[...], (((1,), (1,)), ((), ())), preferred_element_type=F32)
        for c in range(p_ref.shape[0]):
            p_ref[c] = acc[:, c * LANE:(c + 1) * LANE].astype(BF16)

    pl.when(j < na)(lambda: emit(wa_ref))
    pl.when(j >= na)(lambda: emit(wb_ref))


def _in_proj(x2, gain, w_a, w_b, w_dt, hd, seq, tm, tn):
    m, d = x2.shape
    na, nb = w_a.shape[0] // tn, w_b.shape[0] // tn
    n = w_a.shape[0] + w_b.shape[0]
    tps = seq // tm
    return pl.pallas_call(
        functools.partial(_in_proj_kernel, na=na),
        grid=(m // tm, na + nb),
        in_specs=[
            pl.BlockSpec((tm, d), lambda i, j: (i, 0)),
            pl.BlockSpec((1, d), lambda i, j: (0, 0)),
            pl.BlockSpec((tn, d), lambda i, j: (jnp.minimum(j, na - 1), 0)),
            pl.BlockSpec((tn, d), lambda i, j: (jnp.maximum(j - na, 0), 0)),
            pl.BlockSpec((hd, d), lambda i, j: (0, 0)),
        ],
        out_specs=[
            pl.BlockSpec((tn // LANE, tm, LANE), lambda i, j: (j, i, 0)),
            pl.BlockSpec((None, hd, tm), lambda i, j: (i // tps, 0, i % tps)),
        ],
        out_shape=[
            jax.ShapeDtypeStruct((n // LANE, m, LANE), BF16),
            jax.ShapeDtypeStruct((m // seq, hd, seq), F32),
        ],
        scratch_shapes=[pltpu.VMEM((tm, d), BF16)],
        compiler_params=_params("parallel", "arbitrary"),
        name="in_proj",
    )(x2, gain, w_a, w_b, w_dt)


def _sb_attn_kernel(q_ref, k_ref, v_ref, o_ref, *, tq, nchain):
    s, dh = q_ref.shape
    ngroups = s // (tq * nchain)
    row = lax.broadcasted_iota(jnp.int32, (tq, tq), 0)
    col = lax.broadcasted_iota(jnp.int32, (tq, tq), 1)
    causal = col < row
    suffix = jnp.where(row > col, 1.0, 0.0).astype(BF16)
    sum_rhs = jnp.concatenate([suffix, jnp.ones((tq, tq), BF16)], axis=1)

    def scores(qbs, js):
        return [lax.dot_general(qb, k_ref[pl.ds(pl.multiple_of(j * tq, tq), tq), :], (((1,), (1,)), ((), ())),
                                preferred_element_type=F32) for qb, j in zip(qbs, js)]

    def key_blocks(zs, js, carries, accs, masked):
        starts = [pl.multiple_of(j * tq, tq) for j in js]
        sps, parts = [], []
        for z in zs:
            sp = jnp.maximum(z, 0.0) + jnp.log(1.0 + jnp.exp(-jnp.abs(z)))
            if masked:
                sp = jnp.where(causal, sp, 0.0)
            sps.append(sp)
            parts.append(sp.astype(BF16))
        sums = jnp.dot(jnp.concatenate(parts, axis=0), sum_rhs, preferred_element_type=F32)
        ws = []
        for c, (z, sp) in enumerate(zip(zs, sps)):
            w = jnp.exp(z - sp - sums[c * tq:(c + 1) * tq, :tq] - carries[c])
            if masked:
                w = jnp.where(causal, w, 0.0)
            ws.append(w.astype(BF16))
        new_a = [acc + jnp.dot(w, v_ref[pl.ds(st, tq), :], preferred_element_type=F32)
                 for acc, w, st in zip(accs, ws, starts)]
        new_c = [carries[c] + sums[c * tq:(c + 1) * tq, tq:] for c in range(len(js))]
        return new_c, new_a

    def q_group(gi, _):
        qi = [gi * nchain + c for c in range(nchain)]
        qstart = [pl.multiple_of(i * tq, tq) for i in qi]
        qb = [q_ref[pl.ds(st, tq), :] for st in qstart]

        def retire(t, carries):
            carries = [jnp.where(qi[c] - t >= 0, carries[c], jnp.inf) for c in range(nchain)]
            return carries, jnp.min(functools.reduce(jnp.minimum, carries))

        def back(t):
            return [jnp.maximum(qi[c] - t, 0) for c in range(nchain)]

        zs = [scores(qb, back(t)) for t in range(SB_EAGER_STEPS + 1)]
        carries, accs = key_blocks(zs[0], qi, [jnp.zeros((tq, tq), F32)] * nchain,
                                   [jnp.zeros((tq, dh), F32)] * nchain, True)
        carries, low = retire(1, carries)
        for t in range(1, SB_EAGER_STEPS + 1):
            carries, accs = key_blocks(zs[t], back(t), carries, accs, False)
            carries, low = retire(t + 1, carries)

        def cond(st):
            return st[1] < -SB_LOG_WEIGHT_FLOOR

        def body(st):
            t, _, carries, accs = st
            carries, accs = key_blocks(scores(qb, back(t)), back(t), carries, accs, False)
            carries, low = retire(t + 1, carries)
            return t + 1, low, carries, accs

        _, _, _, accs = lax.while_loop(cond, body, (SB_EAGER_STEPS + 1, low, carries, accs))
        for c in range(nchain):
            o_ref[pl.ds(qstart[c], tq), :] = accs[c].astype(o_ref.dtype)
        return 0

    lax.fori_loop(0, ngroups, q_group, 0)


def _sb_attn(p, dims, batch, seq, tq, nchain):
    h, dh = dims.sb_heads, dims.sb_head_dim
    m = batch * seq
    spec = lambda off: pl.BlockSpec((None, seq, dh), lambda b, hh: (off + hh, b, 0))
    return pl.pallas_call(
        functools.partial(_sb_attn_kernel, tq=tq, nchain=nchain),
        grid=(batch, h),
        in_specs=[spec(0), spec(h), spec(2 * h)],
        out_specs=pl.BlockSpec((seq, dh), lambda b, hh: (b, hh)),
        out_shape=jax.ShapeDtypeStruct((m, h * dh), BF16),
        compiler_params=_params("parallel", "parallel"),
        name="sb_attn",
    )(p, p, p)


def _ssd_kernel(x_ref, b_ref, c_ref, z_ref, dt_ref, wx_ref, wb_ref, wc_ref, bx_ref, bb_ref, bc_ref,
                dtb_ref, alog_ref, dskip_ref, norm_ref, shift_ref, y_ref, halo_scr, state_scr, *, hd, kconv, gps):
    _, nbatch, l, _ = x_ref.shape
    nxb = x_ref.shape[0] // gps
    hg = dt_ref.shape[1] // gps
    inner = nxb * LANE
    n_state = b_ref.shape[-1]
    width = inner + 2 * n_state
    assert l == LANE and 2 * hd == LANE and n_state == LANE

    @pl.when(pl.program_id(1) == 0)
    def _():
        halo_scr[...] = jnp.zeros_like(halo_scr)
        state_scr[...] = jnp.zeros_like(state_scr)

    rows = functools.partial(_rows, n=l)
    lane = lax.broadcasted_iota(jnp.int32, (l, LANE), 1)
    lane_t = lax.broadcasted_iota(jnp.int32, (hg, l), 1)
    low_half = lane < hd
    tril = lane <= lax.broadcasted_iota(jnp.int32, (l, l), 0)
    pad = jnp.zeros((LANE - hg, l), F32)

    gcols = lambda ref, gi, w: ref[:, gi * w:(gi + 1) * w]
    taps8, taps, bcat, dskip, norm_g, neg_a, dtb = [], [], [], [], [], [], []
    for gi in range(gps):
        wcat = jnp.concatenate([gcols(wx_ref, gi, inner), gcols(wb_ref, gi, n_state), gcols(wc_ref, gi, n_state)],
                               axis=1)
        wcat16 = wcat.astype(BF16)
        taps8.append([jnp.broadcast_to(wcat[k:k + 1, :], (8, width)) for k in range(kconv)])
        taps.append([jnp.tile(jnp.broadcast_to(wcat16[k:k + 1, :], (16, width)), (l // 16, 1))
                     for k in range(kconv)])
        bcat.append(rows(jnp.concatenate([gcols(bx_ref, gi, inner), gcols(bb_ref, gi, n_state),
                                          gcols(bc_ref, gi, n_state)], axis=1)))
        dskip.append(rows(gcols(dskip_ref, gi, inner)))
        norm_g.append(rows(gcols(norm_ref, gi, inner)))
        neg_a.append(-jnp.exp(alog_ref[gi * hg:(gi + 1) * hg, :]))
        dtb.append(dtb_ref[gi * hg:(gi + 1) * hg, :])

    chains = [(gi, b) for gi in range(gps) for b in range(nbatch)]

    acts = []
    for gi, b in chains:
        raw = jnp.concatenate([x_ref[gi * nxb + c, b] for c in range(nxb)] + [b_ref[gi, b], c_ref[gi, b]],
                              axis=1)
        body = jnp.dot(shift_ref[...], jnp.concatenate([raw * taps[gi][k] for k in range(kconv)], axis=0),
                       preferred_element_type=F32)
        head = jnp.concatenate([halo_scr[gi, b], raw[:8].astype(F32)], axis=0)
        halo_scr[gi, b] = raw[l - 8:, :].astype(F32)
        first = jnp.zeros((8, width), F32)
        for k in range(kconv):
            off = 8 - (kconv - 1) + k
            first = first + taps8[gi][k] * head[off:off + 8, :]
        acts.append(_silu(jnp.concatenate([first, body[8:]], axis=0) + bcat[gi]))
    x16 = [a[:, :inner].astype(BF16) for a in acts]
    bms = [a[:, inner:inner + n_state] for a in acts]
    cms = [a[:, inner + n_state:] for a in acts]

    cbs = [lax.dot_general(cm.astype(BF16), bm.astype(BF16), (((1,), (1,)), ((), ())),
                           preferred_element_type=F32) for cm, bm in zip(cms, bms)]
    bm_ts = [bm.T for bm in bms]
    src_rows, coef_rows, keep_rows, cs_cols = [], [], [], []
    for gi, b in chains:
        dt = _softplus(dt_ref[b, gi * hg:(gi + 1) * hg, :] + dtb[gi])
        cs = dt * neg_a[gi]
        sh = 1
        while sh < l:
            cs = cs + jnp.where(lane_t >= sh, pltpu.roll(cs, sh, axis=1), 0.0)
            sh *= 2
        cs_last = jnp.broadcast_to(cs[:, l - 1:l], (hg, l))
        src_rows.append(cs - jnp.log(dt))
        coef_rows.append(dt * jnp.exp(cs_last - cs))
        keep_rows.append(jnp.exp(cs_last))
        cs_cols.append(jnp.concatenate([cs, pad], axis=0).T)

    states = [state_scr[gi, b] for gi, b in chains]
    y_blocks = [[] for _ in chains]
    new_state = [[] for _ in chains]
    for blk in range(nxb):
        for n in range(len(chains)):
            lhs, scaled_bt = [], []
            for h in (2 * blk, 2 * blk + 1):
                ccol = jnp.broadcast_to(cs_cols[n][:, h:h + 1], (l, LANE))
                seg = ccol - rows(src_rows[n][h:h + 1, :])
                w_in = (cbs[n] * jnp.exp(jnp.where(tril, seg, -jnp.inf))).astype(BF16)
                w_prev = (cms[n] * jnp.exp(ccol)).astype(BF16)
                lhs.append(jnp.concatenate([w_in, w_prev], axis=1))
                scaled_bt.append((bm_ts[n] * rows(coef_rows[n][h:h + 1, :])).astype(BF16))
            xblk = x16[n][:, blk * LANE:(blk + 1) * LANE]
            sblk = states[n][:, blk * LANE:(blk + 1) * LANE]
            rhs = jnp.concatenate([xblk, sblk.astype(BF16)], axis=0)
            yy = jnp.dot(jnp.concatenate(lhs, axis=0), rhs, preferred_element_type=F32)
            y_blocks[n].append(jnp.where(low_half, yy[:l], yy[l:]))
            ss = jnp.dot(jnp.concatenate(scaled_bt, axis=0), xblk, preferred_element_type=F32)
            kr = keep_rows[n]
            keep = jnp.where(low_half[:1], kr[2 * blk:2 * blk + 1, :], kr[2 * blk + 1:2 * blk + 2, :])
            new_state[n].append(sblk * rows(keep) + jnp.where(low_half, ss[:n_state], ss[n_state:]))

    for n, (gi, b) in enumerate(chains):
        state_scr[gi, b] = jnp.concatenate(new_state[n], axis=1)
        y = jnp.concatenate(y_blocks[n], axis=1) + acts[n][:, :inner] * dskip[gi]
        zf = jnp.concatenate([z_ref[gi * nxb + c, b] for c in range(nxb)], axis=1).astype(F32)
        y = y * _silu(zf)
        ms = jnp.mean(y * y, axis=-1, keepdims=True)
        y = y * lax.rsqrt(ms + EPS) * norm_g[gi]
        y_ref[b, :, gi * inner:(gi + 1) * inner] = y.astype(y_ref.dtype)


def _ssd(p, dt_t, conv_w, conv_b, dtb, alog, dskip_e, norm_g, dims, batch, seq, chunk, blk_off, gps):
    g = dims.ssd_groups
    inner_g = dims.ssd_inner // g
    nxb = inner_g // LANE
    hg = dims.ssd_heads // g
    n_state = dims.ssd_state
    nt = seq // chunk
    kconv = conv_w.shape[0]
    z_off, x_off, b_off, c_off = blk_off
    p4 = p.reshape(p.shape[0], batch, seq, LANE)
    sb_off = dims.ssd_inner // n_state
    assert all(o % (gps * nxb) == 0 for o in (z_off, x_off)) and all(o % gps == 0 for o in (b_off, c_off, sb_off, g))
    shift = np.zeros((chunk, kconv * chunk), np.float32)
    for k in range(kconv):
        t = np.arange(kconv - 1 - k, chunk)
        shift[t, chunk * k + t - (kconv - 1) + k] = 1.0
    shift = jnp.asarray(shift, BF16)
    in_specs = [
        pl.BlockSpec((gps * nxb, batch, chunk, LANE), lambda gg, t: (x_off // (gps * nxb) + gg, 0, t, 0)),
        pl.BlockSpec((gps, batch, chunk, LANE), lambda gg, t: (b_off // gps + gg, 0, t, 0)),
        pl.BlockSpec((gps, batch, chunk, LANE), lambda gg, t: (c_off // gps + gg, 0, t, 0)),
        pl.BlockSpec((gps * nxb, batch, chunk, LANE), lambda gg, t: (z_off // (gps * nxb) + gg, 0, t, 0)),
        pl.BlockSpec((batch, gps * hg, chunk), lambda gg, t: (0, gg, t)),
        pl.BlockSpec((kconv, gps * inner_g), lambda gg, t: (0, gg)),
        pl.BlockSpec((kconv, gps * n_state), lambda gg, t: (0, sb_off // gps + gg)),
        pl.BlockSpec((kconv, gps * n_state), lambda gg, t: (0, (sb_off + g) // gps + gg)),
        pl.BlockSpec((1, gps * inner_g), lambda gg, t: (0, gg)),
        pl.BlockSpec((1, gps * n_state), lambda gg, t: (0, sb_off // gps + gg)),
        pl.BlockSpec((1, gps * n_state), lambda gg, t: (0, (sb_off + g) // gps + gg)),
        pl.BlockSpec((gps * hg, LANE), lambda gg, t: (gg, 0)),
        pl.BlockSpec((gps * hg, LANE), lambda gg, t: (gg, 0)),
        pl.BlockSpec((1, gps * inner_g), lambda gg, t: (0, gg)),
        pl.BlockSpec((1, gps * inner_g), lambda gg, t: (0, gg)),
        pl.BlockSpec(shift.shape, lambda gg, t: (0, 0)),
    ]
    y = pl.pallas_call(
        functools.partial(_ssd_kernel, hd=dims.ssd_head_dim, kconv=kconv, gps=gps),
        grid=(g // gps, nt),
        in_specs=in_specs,
        out_specs=pl.BlockSpec((batch, chunk, gps * inner_g), lambda gg, t: (0, t, gg)),
        out_shape=jax.ShapeDtypeStruct((batch, seq, dims.ssd_inner), BF16),
        scratch_shapes=[pltpu.VMEM((gps, batch, 8, inner_g + 2 * n_state), F32),
                        pltpu.VMEM((gps, batch, n_state, inner_g), F32)],
        compiler_params=_params("parallel", "arbitrary"),
        name="ssd",
    )(p4, p4, p4, p4, dt_t, conv_w, conv_w, conv_w, conv_b, conv_b, conv_b, dtb, alog, dskip_e, norm_g, shift)
    return y.reshape(batch * seq, dims.ssd_inner)


def _merge_kernel(o_ref, y_ref, gsb_ref, gssd_ref, wsb_ref, wssd_ref, out_ref):
    a = jnp.dot(o_ref[...], wsb_ref[...], preferred_element_type=F32)
    b = jnp.dot(y_ref[...], wssd_ref[...], preferred_element_type=F32)
    gsb = jnp.concatenate([gsb_ref[c] for c in range(gsb_ref.shape[0])], axis=1).astype(F32)
    gssd = jnp.concatenate([gssd_ref[c] for c in range(gssd_ref.shape[0])], axis=1).astype(F32)
    out_ref[...] = (jax.nn.sigmoid(gsb) * a + jax.nn.sigmoid(gssd) * b).astype(out_ref.dtype)


def _merge(o_sb, y, p, w_sb, w_ssd, gsb_off, gssd_off, tm, tn):
    m, ko = o_sb.shape
    ky = y.shape[1]
    d = w_sb.shape[1]
    nb = tn // LANE
    return pl.pallas_call(
        _merge_kernel,
        grid=(m // tm, d // tn),
        in_specs=[
            pl.BlockSpec((tm, ko), lambda i, j: (i, 0)),
            pl.BlockSpec((tm, ky), lambda i, j: (i, 0)),
            pl.BlockSpec((nb, tm, LANE), lambda i, j: (gsb_off // nb + j, i, 0)),
            pl.BlockSpec((nb, tm, LANE), lambda i, j: (gssd_off // nb + j, i, 0)),
            pl.BlockSpec((ko, tn), lambda i, j: (0, j)),
            pl.BlockSpec((ky, tn), lambda i, j: (0, j)),
        ],
        out_specs=pl.BlockSpec((tm, tn), lambda i, j: (i, j)),
        out_shape=jax.ShapeDtypeStruct((m, d), BF16),
        compiler_params=_params("parallel", "arbitrary"),
        name="merge",
    )(o_sb, y, p, p, w_sb, w_ssd)


def _proj_norm_res_kernel(a_ref, w_ref, g_ref, x_ref, out_ref, f_scr):
    nj, tm, tn = f_scr.shape
    j = pl.program_id(1)
    f_scr[j] = jnp.dot(a_ref[...], w_ref[...], preferred_element_type=F32)

    @pl.when(j == nj - 1)
    def _():
        ss = jnp.zeros((tm, 1), F32)
        for c in range(nj):
            f = f_scr[c]
            ss = ss + jnp.sum(f * f, axis=-1, keepdims=True)
        inv = lax.rsqrt(ss * (1.0 / (nj * tn)) + EPS)
        for c in range(nj):
            cols = slice(c * tn, (c + 1) * tn)
            out_ref[:, cols] = x_ref[:, cols] + f_scr[c] * inv * _rows(g_ref[:, cols], tm)


def _proj_norm_res(a, w, gain, x2, tm, tn, name):
    m, kdim = a.shape
    d = w.shape[1]
    return pl.pallas_call(
        _proj_norm_res_kernel,
        grid=(m // tm, d // tn),
        in_specs=[
            pl.BlockSpec((tm, kdim), lambda i, j: (i, 0)),
            pl.BlockSpec((kdim, tn), lambda i, j: (0, j)),
            pl.BlockSpec((1, d), lambda i, j: (0, 0)),
            pl.BlockSpec((tm, d), lambda i, j: (i, 0)),
        ],
        out_specs=pl.BlockSpec((tm, d), lambda i, j: (i, 0)),
        out_shape=jax.ShapeDtypeStruct((m, d), F32),
        scratch_shapes=[pltpu.VMEM((d // tn, tm, tn), F32)],
        compiler_params=_params("parallel", "arbitrary"),
        name=name,
    )(a, w, gain, x2)


FFN_HALO = 16


def _ffn_up_kernel(x_ref, halo_ref, g_ref, wg_ref, wv_ref, cwg_ref, cwv_ref, cbg_ref, cbv_ref, out_ref, h_scr,
                   *, tiles_per_seq, kconv):
    tm = x_ref.shape[0]

    def norm(v):
        ms = jnp.mean(v * v, axis=-1, keepdims=True)
        return (v * lax.rsqrt(ms + EPS) * g_ref[...]).astype(BF16)

    @pl.when(pl.program_id(1) == 0)
    def _():
        seq_start = (pl.program_id(0) % tiles_per_seq) == 0
        h_scr[:FFN_HALO, :] = jnp.where(seq_start, jnp.zeros((), BF16), norm(halo_ref[...]))
        h_scr[FFN_HALO:, :] = norm(x_ref[...])

    h = h_scr[...]

    def conv(w_ref, cw_ref, cb_ref):
        up = jnp.dot(h, w_ref[...], preferred_element_type=F32)
        u = _rows(cb_ref[...], tm)
        for k in range(kconv):
            off = FFN_HALO - (kconv - 1) + k
            u = u + _rows(cw_ref[k:k + 1, :], tm) * up[off:off + tm, :]
        return u

    gate = conv(wg_ref, cwg_ref, cbg_ref)
    val = conv(wv_ref, cwv_ref, cbv_ref)
    c = math.sqrt(2.0 / math.pi)
    half = 0.5 * gate
    act = half + half * jnp.tanh(gate * (c + (c * 0.044715) * (gate * gate)))
    out_ref[...] = (act * val).astype(out_ref.dtype)


def _ffn_up(x1, gain, w_up, conv_w, conv_b, seq, tm, tn):
    m, d = x1.shape
    dff = w_up.shape[1] // 2
    nj = dff // tn
    kconv = conv_w.shape[0]
    hb = tm // FFN_HALO
    return pl.pallas_call(
        functools.partial(_ffn_up_kernel, tiles_per_seq=seq // tm, kconv=kconv),
        grid=(m // tm, nj),
        in_specs=[
            pl.BlockSpec((tm, d), lambda i, j: (i, 0)),
            pl.BlockSpec((FFN_HALO, d), lambda i, j: (jnp.maximum(i * hb - 1, 0), 0)),
            pl.BlockSpec((1, d), lambda i, j: (0, 0)),
            pl.BlockSpec((d, tn), lambda i, j: (0, j)),
            pl.BlockSpec((d, tn), lambda i, j: (0, nj + j)),
            pl.BlockSpec((kconv, tn), lambda i, j: (0, j)),
            pl.BlockSpec((kconv, tn), lambda i, j: (0, nj + j)),
            pl.BlockSpec((1, tn), lambda i, j: (0, j)),
            pl.BlockSpec((1, tn), lambda i, j: (0, nj + j)),
        ],
        out_specs=pl.BlockSpec((tm, tn), lambda i, j: (i, j)),
        out_shape=jax.ShapeDtypeStruct((m, dff), BF16),
        scratch_shapes=[pltpu.VMEM((FFN_HALO + tm, d), BF16)],
        compiler_params=_params("parallel", "arbitrary"),
        name="ffn_up",
    )(x1, x1, gain, w_up, w_up, conv_w, conv_w, conv_b, conv_b)


def _largest_tile(total, want, quantum=LANE):
    t = min(want, total)
    while total % t or t % quantum:
        t -= quantum
    return t


def _block(x, norm_mix_pre, w_in, ssd_conv_w, ssd_conv_b, dt_bias, a_log, d_skip, ssd_norm, w_sb_proj,
           w_ssd_proj, w_out, norm_mix_post, norm_ffn_pre, w_up, ffn_conv_w, ffn_conv_b, w_down, norm_ffn_post,
           dims, tiles):
    batch, seq, d = x.shape
    m = batch * seq
    sbw, inner, xbc, heads = dims.sb_width, dims.ssd_inner, dims.ssd_xbc, dims.ssd_heads
    x2 = x.reshape(m, d)
    row = lambda v: v.reshape(1, -1)

    dt_lo = 3 * sbw + inner + xbc
    w_in_t = w_in.T
    row_scale = jnp.where(jnp.arange(dt_lo) < sbw, dims.sb_head_dim ** -0.5, 1.0).astype(F32)[:, None]
    w_a = (w_in_t[:dt_lo] * row_scale).astype(BF16)
    w_b = w_in_t[dt_lo + heads:].astype(BF16)
    w_dt = w_in_t[dt_lo:dt_lo + heads]
    z_off = 3 * sbw // LANE
    x_off = z_off + inner // LANE
    b_off = x_off + inner // LANE
    c_off = b_off + dims.ssd_groups * dims.ssd_state // LANE
    gsb_off = c_off + dims.ssd_groups * dims.ssd_state // LANE
    gssd_off = gsb_off + d // LANE

    tm = _largest_tile(seq, tiles["tm"], 16)
    p, dt_t = _in_proj(x2, row(norm_mix_pre), w_a, w_b, w_dt, heads, seq, tm,
                       _largest_tile(math.gcd(w_a.shape[0], w_b.shape[0]), tiles["tn_in"]))

    o_sb = _sb_attn(p, dims, batch, seq, min(tiles["tq"], seq), tiles["sb_chains"])

    chunk = min(tiles["chunk"], seq)
    bcast = lambda v: jnp.broadcast_to(v.reshape(-1, 1), (heads, LANE))
    y = _ssd(p, dt_t, ssd_conv_w, row(ssd_conv_b), bcast(dt_bias), bcast(a_log),
             row(jnp.repeat(d_skip, dims.ssd_head_dim)), row(ssd_norm), dims, batch, seq, chunk,
             (z_off, x_off, b_off, c_off), tiles["ssd_groups_per_step"])

    merged = _merge(o_sb, y, p, w_sb_proj.astype(BF16), w_ssd_proj.astype(BF16), gsb_off, gssd_off,
                    _largest_tile(m, tiles["tm_merge"], 16), _largest_tile(d, tiles["tn_merge"]))
    tm2 = _largest_tile(m, tiles["tm_res"], 16)
    x1 = _proj_norm_res(merged, w_out.astype(BF16), row(norm_mix_post), x2, tm2, d, "out_proj")

    tm_up = _largest_tile(seq, tiles["tm"], 16)
    act = _ffn_up(x1, row(norm_ffn_pre), w_up.astype(BF16), ffn_conv_w, row(ffn_conv_b), seq, tm_up,
                  _largest_tile(dims.d_ff, tiles["tn_up"]))
    out = _proj_norm_res(act, w_down.astype(BF16), row(norm_ffn_post), x1, tm2,
                         _largest_tile(d, tiles["tn_down"]), "ffn_down")
    return out.reshape(batch, seq, d)


TILES = dict(tm=1024, tn_in=1024, tq=128, sb_chains=8, chunk=128, ssd_groups_per_step=4, tm_merge=1024, tn_merge=512, tm_res=512, tn_up=512, tn_down=512)


def kernel(x, norm_mix_pre, w_in, ssd_conv_w, ssd_conv_b, dt_bias, a_log, d_skip, ssd_norm, w_sb_proj, w_ssd_proj,
           w_out, norm_mix_post, norm_ffn_pre, w_up, ffn_conv_w, ffn_conv_b, w_down, norm_ffn_post):
    dims = Dims()
    args = (norm_mix_pre, w_in, ssd_conv_w, ssd_conv_b, dt_bias, a_log, d_skip, ssd_norm, w_sb_proj, w_ssd_proj,
            w_out, norm_mix_post, norm_ffn_pre, w_up, ffn_conv_w, ffn_conv_b, w_down, norm_ffn_post)
    for layer in range(w_in.shape[0]):
        x = _block(x, *(a[layer] for a in args), dims, TILES)
    return x
```

```python
import dataclasses
import functools
import math

import numpy as np

import jax
import jax.numpy as jnp
from jax import lax
from jax.experimental import pallas as pl
from jax.experimental.pallas import tpu as pltpu

F32 = jnp.float32
BF16 = jnp.bfloat16
LANE = 128
EPS = 1e-6


@dataclasses.dataclass(frozen=True)
class Dims:
    d_model: int = 2048
    sb_heads: int = 16
    sb_head_dim: int = 128
    ssd_inner: int = 4096
    ssd_head_dim: int = 64
    ssd_groups: int = 8
    ssd_state: int = 128
    ssd_conv: int = 4
    d_ff: int = 5632
    ffn_conv: int = 3

    @property
    def sb_width(self):
        return self.sb_heads * self.sb_head_dim

    @property
    def ssd_heads(self):
        return self.ssd_inner // self.ssd_head_dim

    @property
    def ssd_xbc(self):
        return self.ssd_inner + 2 * self.ssd_groups * self.ssd_state


SB_LOG_WEIGHT_FLOOR = -110.0
SB_EAGER_STEPS = 2
VMEM_LIMIT = 56 * 1024 * 1024


def _softplus(x):
    return jnp.maximum(x, 0.0) + jnp.log1p(jnp.exp(-jnp.abs(x)))


def _rows(v, n):
    return jnp.tile(jnp.broadcast_to(v, (8, v.shape[1])), (n // 8, 1))


def _silu(x):
    h = 0.5 * x
    return h + h * jnp.tanh(h)


def _params(*sem):
    return pltpu.CompilerParams(dimension_semantics=sem, vmem_limit_bytes=VMEM_LIMIT)


def _in_proj_kernel(x_ref, g_ref, wa_ref, wb_ref, wdt_ref, p_ref, dt_ref, h_scr, *, na):
    j = pl.program_id(1)

    @pl.when(j == 0)
    def _():
        x = x_ref[...]
        ms = jnp.mean(x * x, axis=-1, keepdims=True)
        h = (x * lax.rsqrt(ms + EPS) * g_ref[...]).astype(BF16)
        h_scr[...] = h
        dt_ref[...] = lax.dot_general(wdt_ref[...].astype(BF16), h, (((1,), (1,)), ((), ())),
                                      preferred_element_type=F32)

    def emit(w_ref):
        acc = lax.dot_general(h_scr[...], w_ref[...], (((1,), (1,)), ((), ())), preferred_element_type=F32)
        for c in range(p_ref.shape[0]):
            p_ref[c] = acc[:, c * LANE:(c + 1) * LANE].astype(BF16)

    pl.when(j < na)(lambda: emit(wa_ref))
    pl.when(j >= na)(lambda: emit(wb_ref))


def _in_proj(x2, gain, w_a, w_b, w_dt, hd, seq, tm, tn):
    m, d = x2.shape
    na, nb = w_a.shape[0] // tn, w_b.shape[0] // tn
    n = w_a.shape[0] + w_b.shape[0]
    tps = seq // tm
    return pl.pallas_call(
        functools.partial(_in_proj_kernel, na=na),
        grid=(m // tm, na + nb),
        in_specs=[
            pl.BlockSpec((tm, d), lambda i, j: (i, 0)),
            pl.BlockSpec((1, d), lambda i, j: (0, 0)),
            pl.BlockSpec((tn, d), lambda i, j: (jnp.minimum(j, na - 1), 0)),
            pl.BlockSpec((tn, d), lambda i, j: (jnp.maximum(j - na, 0), 0)),
            pl.BlockSpec((hd, d), lambda i, j: (0, 0)),
        ],
        out_specs=[
            pl.BlockSpec((tn // LANE, tm, LANE), lambda i, j: (j, i, 0)),
            pl.BlockSpec((None, hd, tm), lambda i, j: (i // tps, 0, i % tps)),
        ],
        out_shape=[
            jax.ShapeDtypeStruct((n // LANE, m, LANE), BF16),
            jax.ShapeDtypeStruct((m // seq, hd, seq), F32),
        ],
        scratch_shapes=[pltpu.VMEM((tm, d), BF16)],
        compiler_params=_params("parallel", "arbitrary"),
        name="in_proj",
    )(x2, gain, w_a, w_b, w_dt)


def _sb_attn_kernel(q_ref, k_ref, v_ref, *rest, tq, nchain, n_cast):
    o_ref = rest[n_cast]
    for src, dst in zip(rest[:n_cast], rest[n_cast + 1:]):
        dst[...] = src[...].astype(dst.dtype)
    s, dh = q_ref.shape
    ngroups = s // (tq * nchain)
    row = lax.broadcasted_iota(jnp.int32, (tq, tq), 0)
    col = lax.broadcasted_iota(jnp.int32, (tq, tq), 1)
    causal = col < row
    suffix = jnp.where(row > col, 1.0, 0.0).astype(BF16)
    sum_rhs = jnp.concatenate([suffix, jnp.ones((tq, tq), BF16)], axis=1)

    def scores(qbs, js):
        return [lax.dot_general(qb, k_ref[pl.ds(pl.multiple_of(j * tq, tq), tq), :], (((1,), (1,)), ((), ())),
                                preferred_element_type=F32) for qb, j in zip(qbs, js)]

    def key_blocks(zs, js, carries, accs, masked):
        starts = [pl.multiple_of(j * tq, tq) for j in js]
        sps, parts = [], []
        for z in zs:
            sp = jnp.maximum(z, 0.0) + jnp.log(1.0 + jnp.exp(-jnp.abs(z)))
            if masked:
                sp = jnp.where(causal, sp, 0.0)
            sps.append(sp)
            parts.append(sp.astype(BF16))
        sums = jnp.dot(jnp.concatenate(parts, axis=0), sum_rhs, preferred_element_type=F32)
        ws = []
        for c, (z, sp) in enumerate(zip(zs, sps)):
            w = jnp.exp(z - sp - sums[c * tq:(c + 1) * tq, :tq] - carries[c])
            if masked:
                w = jnp.where(causal, w, 0.0)
            ws.append(w.astype(BF16))
        new_a = [acc + jnp.dot(w, v_ref[pl.ds(st, tq), :], preferred_element_type=F32)
                 for acc, w, st in zip(accs, ws, starts)]
        new_c = [carries[c] + sums[c * tq:(c + 1) * tq, tq:] for c in range(len(js))]
        return new_c, new_a

    def q_group(gi, _):
        qi = [gi * nchain + c for c in range(nchain)]
        qstart = [pl.multiple_of(i * tq, tq) for i in qi]
        qb = [q_ref[pl.ds(st, tq), :] for st in qstart]

        def retire(t, carries):
            carries = [jnp.where(qi[c] - t >= 0, carries[c], jnp.inf) for c in range(nchain)]
            return carries, jnp.min(functools.reduce(jnp.minimum, carries))

        def back(t):
            return [jnp.maximum(qi[c] - t, 0) for c in range(nchain)]

        zs = [scores(qb, back(t)) for t in range(SB_EAGER_STEPS + 1)]
        carries, accs = key_blocks(zs[0], qi, [jnp.zeros((tq, tq), F32)] * nchain,
                                   [jnp.zeros((tq, dh), F32)] * nchain, True)
        carries, low = retire(1, carries)
        for t in range(1, SB_EAGER_STEPS + 1):
            carries, accs = key_blocks(zs[t], back(t), carries, accs, False)
            carries, low = retire(t + 1, carries)

        def cond(st):
            return st[1] < -SB_LOG_WEIGHT_FLOOR

        def body(st):
            t, _, carries, accs = st
            carries, accs = key_blocks(scores(qb, back(t)), back(t), carries, accs, False)
            carries, low = retire(t + 1, carries)
            return t + 1, low, carries, accs

        _, _, _, accs = lax.while_loop(cond, body, (SB_EAGER_STEPS + 1, low, carries, accs))
        for c in range(nchain):
            o_ref[pl.ds(qstart[c], tq), :] = accs[c].astype(o_ref.dtype)
        return 0

    lax.fori_loop(0, ngroups, q_group, 0)


def _sb_attn(p, dims, batch, seq, tq, nchain, cast_weights):
    h, dh = dims.sb_heads, dims.sb_head_dim
    m = batch * seq
    steps = batch * h
    spec = lambda off: pl.BlockSpec((None, seq, dh), lambda b, hh: (off + hh, b, 0))
    slab = lambda w: pl.BlockSpec((w.shape[0] // steps, w.shape[1]), lambda b, hh: (b * h + hh, 0))
    outs = pl.pallas_call(
        functools.partial(_sb_attn_kernel, tq=tq, nchain=nchain, n_cast=len(cast_weights)),
        grid=(batch, h),
        in_specs=[spec(0), spec(h), spec(2 * h)] + [slab(w) for w in cast_weights],
        out_specs=[pl.BlockSpec((seq, dh), lambda b, hh: (b, hh))] + [slab(w) for w in cast_weights],
        out_shape=[jax.ShapeDtypeStruct((m, h * dh), BF16)]
                  + [jax.ShapeDtypeStruct(w.shape, BF16) for w in cast_weights],
        compiler_params=_params("arbitrary", "arbitrary"),
        name="sb_attn",
    )(p, p, p, *cast_weights)
    return outs[0], outs[1:]


def _ssd_kernel(x_ref, b_ref, c_ref, z_ref, dt_ref, wx_ref, wb_ref, wc_ref, bx_ref, bb_ref, bc_ref,
                dtb_ref, alog_ref, dskip_ref, norm_ref, shift_ref, y_ref, halo_scr, state_scr, *, hd, kconv, gps):
    _, nbatch, l, _ = x_ref.shape
    nxb = x_ref.shape[0] // gps
    hg = dt_ref.shape[1] // gps
    inner = nxb * LANE
    n_state = b_ref.shape[-1]
    width = inner + 2 * n_state
    assert l == LANE and 2 * hd == LANE and n_state == LANE

    @pl.when(pl.program_id(1) == 0)
    def _():
        halo_scr[...] = jnp.zeros_like(halo_scr)
        state_scr[...] = jnp.zeros_like(state_scr)

    rows = functools.partial(_rows, n=l)
    lane = lax.broadcasted_iota(jnp.int32, (l, LANE), 1)
    lane_t = lax.broadcasted_iota(jnp.int32, (hg, l), 1)
    low_half = lane < hd
    tril = lane <= lax.broadcasted_iota(jnp.int32, (l, l), 0)
    pad = jnp.zeros((LANE - hg, l), F32)

    gcols = lambda ref, gi, w: ref[:, gi * w:(gi + 1) * w]
    taps8, taps, bcat, dskip, norm_g, neg_a, dtb = [], [], [], [], [], [], []
    for gi in range(gps):
        wcat = jnp.concatenate([gcols(wx_ref, gi, inner), gcols(wb_ref, gi, n_state), gcols(wc_ref, gi, n_state)],
                               axis=1)
        wcat16 = wcat.astype(BF16)
        taps8.append([jnp.broadcast_to(wcat[k:k + 1, :], (8, width)) for k in range(kconv)])
        taps.append([jnp.tile(jnp.broadcast_to(wcat16[k:k + 1, :], (16, width)), (l // 16, 1))
                     for k in range(kconv)])
        bcat.append(rows(jnp.concatenate([gcols(bx_ref, gi, inner), gcols(bb_ref, gi, n_state),
                                          gcols(bc_ref, gi, n_state)], axis=1)))
        dskip.append(rows(gcols(dskip_ref, gi, inner)))
        norm_g.append(rows(gcols(norm_ref, gi, inner)))
        neg_a.append(-jnp.exp(alog_ref[gi * hg:(gi + 1) * hg, :]))
        dtb.append(dtb_ref[gi * hg:(gi + 1) * hg, :])

    chains = [(gi, b) for gi in range(gps) for b in range(nbatch)]

    acts = []
    for gi, b in chains:
        raw = jnp.concatenate([x_ref[gi * nxb + c, b] for c in range(nxb)] + [b_ref[gi, b], c_ref[gi, b]],
                              axis=1)
        body = jnp.dot(shift_ref[...], jnp.concatenate([raw * taps[gi][k] for k in range(kconv)], axis=0),
                       preferred_element_type=F32)
        head = jnp.concatenate([halo_scr[gi, b], raw[:8].astype(F32)], axis=0)
        halo_scr[gi, b] = raw[l - 8:, :].astype(F32)
        first = jnp.zeros((8, width), F32)
        for k in range(kconv):
            off = 8 - (kconv - 1) + k
            first = first + taps8[gi][k] * head[off:off + 8, :]
        acts.append(_silu(jnp.concatenate([first, body[8:]], axis=0) + bcat[gi]))
    x16 = [a[:, :inner].astype(BF16) for a in acts]
    bms = [a[:, inner:inner + n_state] for a in acts]
    cms = [a[:, inner + n_state:] for a in acts]

    cbs = [lax.dot_general(cm.astype(BF16), bm.astype(BF16), (((1,), (1,)), ((), ())),
                           preferred_element_type=F32) for cm, bm in zip(cms, bms)]
    bm_ts = [bm.T for bm in bms]
    src_rows, coef_rows, keep_rows, cs_cols = [], [], [], []
    for gi, b in chains:
        dt = _softplus(dt_ref[b, gi * hg:(gi + 1) * hg, :] + dtb[gi])
        cs = dt * neg_a[gi]
        sh = 1
        while sh < l:
            cs = cs + jnp.where(lane_t >= sh, pltpu.roll(cs, sh, axis=1), 0.0)
            sh *= 2
        cs_last = jnp.broadcast_to(cs[:, l - 1:l], (hg, l))
        src_rows.append(cs - jnp.log(dt))
        coef_rows.append(dt * jnp.exp(cs_last - cs))
        keep_rows.append(jnp.exp(cs_last))
        cs_cols.append(jnp.concatenate([cs, pad], axis=0).T)

    states = [state_scr[gi, b] for gi, b in chains]
    y_blocks = [[] for _ in chains]
    new_state = [[] for _ in chains]
    for blk in range(nxb):
        for n in range(len(chains)):
            lhs, scaled_bt = [], []
            for h in (2 * blk, 2 * blk + 1):
                ccol = jnp.broadcast_to(cs_cols[n][:, h:h + 1], (l, LANE))
                seg = ccol - rows(src_rows[n][h:h + 1, :])
                w_in = (cbs[n] * jnp.exp(jnp.where(tril, seg, -jnp.inf))).astype(BF16)
                w_prev = (cms[n] * jnp.exp(ccol)).astype(BF16)
                lhs.append(jnp.concatenate([w_in, w_prev], axis=1))
                scaled_bt.append((bm_ts[n] * rows(coef_rows[n][h:h + 1, :])).astype(BF16))
            xblk = x16[n][:, blk * LANE:(blk + 1) * LANE]
            sblk = states[n][:, blk * LANE:(blk + 1) * LANE]
            rhs = jnp.concatenate([xblk, sblk.astype(BF16)], axis=0)
            yy = jnp.dot(jnp.concatenate(lhs, axis=0), rhs, preferred_element_type=F32)
            y_blocks[n].append(jnp.where(low_half, yy[:l], yy[l:]))
            ss = jnp.dot(jnp.concatenate(scaled_bt, axis=0), xblk, preferred_element_type=F32)
            kr = keep_rows[n]
            keep = jnp.where(low_half[:1], kr[2 * blk:2 * blk + 1, :], kr[2 * blk + 1:2 * blk + 2, :])
            new_state[n].append(sblk * rows(keep) + jnp.where(low_half, ss[:n_state], ss[n_state:]))

    for n, (gi, b) in enumerate(chains):
        state_scr[gi, b] = jnp.concatenate(new_state[n], axis=1)
        y = jnp.concatenate(y_blocks[n], axis=1) + acts[n][:, :inner] * dskip[gi]
        zf = jnp.concatenate([z_ref[gi * nxb + c, b] for c in range(nxb)], axis=1).astype(F32)
        y = y * _silu(zf)
        ms = jnp.mean(y * y, axis=-1, keepdims=True)
        y = y * lax.rsqrt(ms + EPS) * norm_g[gi]
        y_ref[b, :, gi * inner:(gi + 1) * inner] = y.astype(y_ref.dtype)


def _ssd(p, dt_t, conv_w, conv_b, dtb, alog, dskip_e, norm_g, dims, batch, seq, chunk, blk_off, gps):
    g = dims.ssd_groups
    inner_g = dims.ssd_inner // g
    nxb = inner_g // LANE
    hg = dims.ssd_heads // g
    n_state = dims.ssd_state
    nt = seq // chunk
    kconv = conv_w.shape[0]
    z_off, x_off, b_off, c_off = blk_off
    p4 = p.reshape(p.shape[0], batch, seq, LANE)
    sb_off = dims.ssd_inner // n_state
    assert all(o % (gps * nxb) == 0 for o in (z_off, x_off)) and all(o % gps == 0 for o in (b_off, c_off, sb_off, g))
    shift = np.zeros((chunk, kconv * chunk), np.float32)
    for k in range(kconv):
        t = np.arange(kconv - 1 - k, chunk)
        shift[t, chunk * k + t - (kconv - 1) + k] = 1.0
    shift = jnp.asarray(shift, BF16)
    in_specs = [
        pl.BlockSpec((gps * nxb, batch, chunk, LANE), lambda gg, t: (x_off // (gps * nxb) + gg, 0, t, 0)),
        pl.BlockSpec((gps, batch, chunk, LANE), lambda gg, t: (b_off // gps + gg, 0, t, 0)),
        pl.BlockSpec((gps, batch, chunk, LANE), lambda gg, t: (c_off // gps + gg, 0, t, 0)),
        pl.BlockSpec((gps * nxb, batch, chunk, LANE), lambda gg, t: (z_off // (gps * nxb) + gg, 0, t, 0)),
        pl.BlockSpec((batch, gps * hg, chunk), lambda gg, t: (0, gg, t)),
        pl.BlockSpec((kconv, gps * inner_g), lambda gg, t: (0, gg)),
        pl.BlockSpec((kconv, gps * n_state), lambda gg, t: (0, sb_off // gps + gg)),
        pl.BlockSpec((kconv, gps * n_state), lambda gg, t: (0, (sb_off + g) // gps + gg)),
        pl.BlockSpec((1, gps * inner_g), lambda gg, t: (0, gg)),
        pl.BlockSpec((1, gps * n_state), lambda gg, t: (0, sb_off // gps + gg)),
        pl.BlockSpec((1, gps * n_state), lambda gg, t: (0, (sb_off + g) // gps + gg)),
        pl.BlockSpec((gps * hg, LANE), lambda gg, t: (gg, 0)),
        pl.BlockSpec((gps * hg, LANE), lambda gg, t: (gg, 0)),
        pl.BlockSpec((1, gps * inner_g), lambda gg, t: (0, gg)),
        pl.BlockSpec((1, gps * inner_g), lambda gg, t: (0, gg)),
        pl.BlockSpec(shift.shape, lambda gg, t: (0, 0)),
    ]
    y = pl.pallas_call(
        functools.partial(_ssd_kernel, hd=dims.ssd_head_dim, kconv=kconv, gps=gps),
        grid=(g // gps, nt),
        in_specs=in_specs,
        out_specs=pl.BlockSpec((batch, chunk, gps * inner_g), lambda gg, t: (0, t, gg)),
        out_shape=jax.ShapeDtypeStruct((batch, seq, dims.ssd_inner), BF16),
        scratch_shapes=[pltpu.VMEM((gps, batch, 8, inner_g + 2 * n_state), F32),
                        pltpu.VMEM((gps, batch, n_state, inner_g), F32)],
        compiler_params=_params("parallel", "arbitrary"),
        name="ssd",
    )(p4, p4, p4, p4, dt_t, conv_w, conv_w, conv_w, conv_b, conv_b, conv_b, dtb, alog, dskip_e, norm_g, shift)
    return y.reshape(batch * seq, dims.ssd_inner)


def _merge_kernel(o_ref, y_ref, gsb_ref, gssd_ref, wsb_ref, wssd_ref, out_ref):
    a = jnp.dot(o_ref[...], wsb_ref[...], preferred_element_type=F32)
    b = jnp.dot(y_ref[...], wssd_ref[...], preferred_element_type=F32)
    gsb = jnp.concatenate([gsb_ref[c] for c in range(gsb_ref.shape[0])], axis=1).astype(F32)
    gssd = jnp.concatenate([gssd_ref[c] for c in range(gssd_ref.shape[0])], axis=1).astype(F32)
    out_ref[...] = (jax.nn.sigmoid(gsb) * a + jax.nn.sigmoid(gssd) * b).astype(out_ref.dtype)


def _merge(o_sb, y, p, w_sb, w_ssd, gsb_off, gssd_off, tm, tn):
    m, ko = o_sb.shape
    ky = y.shape[1]
    d = w_sb.shape[1]
    nb = tn // LANE
    return pl.pallas_call(
        _merge_kernel,
        grid=(m // tm, d // tn),
        in_specs=[
            pl.BlockSpec((tm, ko), lambda i, j: (i, 0)),
            pl.BlockSpec((tm, ky), lambda i, j: (i, 0)),
            pl.BlockSpec((nb, tm, LANE), lambda i, j: (gsb_off // nb + j, i, 0)),
            pl.BlockSpec((nb, tm, LANE), lambda i, j: (gssd_off // nb + j, i, 0)),
            pl.BlockSpec((ko, tn), lambda i, j: (0, j)),
            pl.BlockSpec((ky, tn), lambda i, j: (0, j)),
        ],
        out_specs=pl.BlockSpec((tm, tn), lambda i, j: (i, j)),
        out_shape=jax.ShapeDtypeStruct((m, d), BF16),
        compiler_params=_params("parallel", "arbitrary"),
        name="merge",
    )(o_sb, y, p, p, w_sb, w_ssd)


def _proj_norm_res_kernel(a_ref, w_ref, g_ref, x_ref, out_ref, f_scr):
    nj, tm, tn = f_scr.shape
    j = pl.program_id(1)
    f_scr[j] = jnp.dot(a_ref[...], w_ref[...], preferred_element_type=F32)

    @pl.when(j == nj - 1)
    def _():
        ss = jnp.zeros((tm, 1), F32)
        for c in range(nj):
            f = f_scr[c]
            ss = ss + jnp.sum(f * f, axis=-1, keepdims=True)
        inv = lax.rsqrt(ss * (1.0 / (nj * tn)) + EPS)
        for c in range(nj):
            cols = slice(c * tn, (c + 1) * tn)
            out_ref[:, cols] = x_ref[:, cols] + f_scr[c] * inv * _rows(g_ref[:, cols], tm)


def _proj_norm_res(a, w, gain, x2, tm, tn, name):
    m, kdim = a.shape
    d = w.shape[1]
    return pl.pallas_call(
        _proj_norm_res_kernel,
        grid=(m // tm, d // tn),
        in_specs=[
            pl.BlockSpec((tm, kdim), lambda i, j: (i, 0)),
            pl.BlockSpec((kdim, tn), lambda i, j: (0, j)),
            pl.BlockSpec((1, d), lambda i, j: (0, 0)),
            pl.BlockSpec((tm, d), lambda i, j: (i, 0)),
        ],
        out_specs=pl.BlockSpec((tm, d), lambda i, j: (i, 0)),
        out_shape=jax.ShapeDtypeStruct((m, d), F32),
        scratch_shapes=[pltpu.VMEM((d // tn, tm, tn), F32)],
        compiler_params=_params("parallel", "arbitrary"),
        name=name,
    )(a, w, gain, x2)


FFN_HALO = 16


def _ffn_up_kernel(x_ref, halo_ref, g_ref, wg_ref, wv_ref, cwg_ref, cwv_ref, cbg_ref, cbv_ref, out_ref, h_scr,
                   *, tiles_per_seq, kconv):
    tm = x_ref.shape[0]

    def norm(v):
        ms = jnp.mean(v * v, axis=-1, keepdims=True)
        return (v * lax.rsqrt(ms + EPS) * g_ref[...]).astype(BF16)

    @pl.when(pl.program_id(1) == 0)
    def _():
        seq_start = (pl.program_id(0) % tiles_per_seq) == 0
        h_scr[:FFN_HALO, :] = jnp.where(seq_start, jnp.zeros((), BF16), norm(halo_ref[...]))
        h_scr[FFN_HALO:, :] = norm(x_ref[...])

    h = h_scr[...]

    def conv(w_ref, cw_ref, cb_ref):
        up = jnp.dot(h, w_ref[...], preferred_element_type=F32)
        u = _rows(cb_ref[...], tm)
        for k in range(kconv):
            off = FFN_HALO - (kconv - 1) + k
            u = u + _rows(cw_ref[k:k + 1, :], tm) * up[off:off + tm, :]
        return u

    gate = conv(wg_ref, cwg_ref, cbg_ref)
    val = conv(wv_ref, cwv_ref, cbv_ref)
    c = math.sqrt(2.0 / math.pi)
    half = 0.5 * gate
    act = half + half * jnp.tanh(gate * (c + (c * 0.044715) * (gate * gate)))
    out_ref[...] = (act * val).astype(out_ref.dtype)


def _ffn_up(x1, gain, w_up, conv_w, conv_b, seq, tm, tn):
    m, d = x1.shape
    dff = w_up.shape[1] // 2
    nj = dff // tn
    kconv = conv_w.shape[0]
    hb = tm // FFN_HALO
    return pl.pallas_call(
        functools.partial(_ffn_up_kernel, tiles_per_seq=seq // tm, kconv=kconv),
        grid=(m // tm, nj),
        in_specs=[
            pl.BlockSpec((tm, d), lambda i, j: (i, 0)),
            pl.BlockSpec((FFN_HALO, d), lambda i, j: (jnp.maximum(i * hb - 1, 0), 0)),
            pl.BlockSpec((1, d), lambda i, j: (0, 0)),
            pl.BlockSpec((d, tn), lambda i, j: (0, j)),
            pl.BlockSpec((d, tn), lambda i, j: (0, nj + j)),
            pl.BlockSpec((kconv, tn), lambda i, j: (0, j)),
            pl.BlockSpec((kconv, tn), lambda i, j: (0, nj + j)),
            pl.BlockSpec((1, tn), lambda i, j: (0, j)),
            pl.BlockSpec((1, tn), lambda i, j: (0, nj + j)),
        ],
        out_specs=pl.BlockSpec((tm, tn), lambda i, j: (i, j)),
        out_shape=jax.ShapeDtypeStruct((m, dff), BF16),
        scratch_shapes=[pltpu.VMEM((FFN_HALO + tm, d), BF16)],
        compiler_params=_params("parallel", "arbitrary"),
        name="ffn_up",
    )(x1, x1, gain, w_up, w_up, conv_w, conv_w, conv_b, conv_b)


def _largest_tile(total, want, quantum=LANE):
    t = min(want, total)
    while total % t or t % quantum:
        t -= quantum
    return t


def _block(x, norm_mix_pre, w_in, ssd_conv_w, ssd_conv_b, dt_bias, a_log, d_skip, ssd_norm, w_sb_proj,
           w_ssd_proj, w_out, norm_mix_post, norm_ffn_pre, w_up, ffn_conv_w, ffn_conv_b, w_down, norm_ffn_post,
           dims, tiles):
    batch, seq, d = x.shape
    m = batch * seq
    sbw, inner, xbc, heads = dims.sb_width, dims.ssd_inner, dims.ssd_xbc, dims.ssd_heads
    x2 = x.reshape(m, d)
    row = lambda v: v.reshape(1, -1)

    dt_lo = 3 * sbw + inner + xbc
    w_in_t = w_in.T
    row_scale = jnp.where(jnp.arange(dt_lo) < sbw, dims.sb_head_dim ** -0.5, 1.0).astype(F32)[:, None]
    w_a = (w_in_t[:dt_lo] * row_scale).astype(BF16)
    w_b = w_in_t[dt_lo + heads:].astype(BF16)
    w_dt = w_in_t[dt_lo:dt_lo + heads]
    z_off = 3 * sbw // LANE
    x_off = z_off + inner // LANE
    b_off = x_off + inner // LANE
    c_off = b_off + dims.ssd_groups * dims.ssd_state // LANE
    gsb_off = c_off + dims.ssd_groups * dims.ssd_state // LANE
    gssd_off = gsb_off + d // LANE

    tm = _largest_tile(seq, tiles["tm"], 16)
    p, dt_t = _in_proj(x2, row(norm_mix_pre), w_a, w_b, w_dt, heads, seq, tm,
                       _largest_tile(math.gcd(w_a.shape[0], w_b.shape[0]), tiles["tn_in"]))

    o_sb, (w_sb16, w_ssd16, w_out16, w_up16, w_down16) = _sb_attn(
        p, dims, batch, seq, min(tiles["tq"], seq), tiles["sb_chains"],
        [w_sb_proj, w_ssd_proj, w_out, w_up, w_down])

    chunk = min(tiles["chunk"], seq)
    bcast = lambda v: jnp.broadcast_to(v.reshape(-1, 1), (heads, LANE))
    y = _ssd(p, dt_t, ssd_conv_w, row(ssd_conv_b), bcast(dt_bias), bcast(a_log),
             row(jnp.repeat(d_skip, dims.ssd_head_dim)), row(ssd_norm), dims, batch, seq, chunk,
             (z_off, x_off, b_off, c_off), tiles["ssd_groups_per_step"])

    merged = _merge(o_sb, y, p, w_sb16, w_ssd16, gsb_off, gssd_off,
                    _largest_tile(m, tiles["tm_merge"], 16), _largest_tile(d, tiles["tn_merge"]))
    tm2 = _largest_tile(m, tiles["tm_res"], 16)
    x1 = _proj_norm_res(merged, w_out16, row(norm_mix_post), x2, tm2, d, "out_proj")

    tm_up = _largest_tile(seq, tiles["tm"], 16)
    act = _ffn_up(x1, row(norm_ffn_pre), w_up16, ffn_conv_w, row(ffn_conv_b), seq, tm_up,
                  _largest_tile(dims.d_ff, tiles["tn_up"]))
    out = _proj_norm_res(act, w_down16, row(norm_ffn_post), x1, tm2,
                         _largest_tile(d, tiles["tn_down"]), "ffn_down")
    return out.reshape(batch, seq, d)


TILES = dict(tm=1024, tn_in=1024, tq=128, sb_chains=8, chunk=128, ssd_groups_per_step=4, tm_merge=1024, tn_merge=512, tm_res=512, tn_up=512, tn_down=512)


def kernel(x, norm_mix_pre, w_in, ssd_conv_w, ssd_conv_b, dt_bias, a_log, d_skip, ssd_norm, w_sb_proj, w_ssd_proj,
           w_out, norm_mix_post, norm_ffn_pre, w_up, ffn_conv_w, ffn_conv_b, w_down, norm_ffn_post):
    dims = Dims()
    args = (norm_mix_pre, w_in, ssd_conv_w, ssd_conv_b, dt_bias, a_log, d_skip, ssd_norm, w_sb_proj, w_ssd_proj,
            w_out, norm_mix_post, norm_ffn_pre, w_up, ffn_conv_w, ffn_conv_b, w_down, norm_ffn_post)
    for layer in range(w_in.shape[0]):
        x = _block(x, *(a[layer] for a in args), dims, TILES)
    return x
```

```python
import dataclasses
import functools
import math

import numpy as np

import jax
import jax.numpy as jnp
from jax import lax
from jax.experimental import pallas as pl
from jax.experimental.pallas import tpu as pltpu

F32 = jnp.float32
BF16 = jnp.bfloat16
LANE = 128
EPS = 1e-6


@dataclasses.dataclass(frozen=True)
class Dims:
    d_model: int = 2048
    sb_heads: int = 16
    sb_head_dim: int = 128
    ssd_inner: int = 4096
    ssd_head_dim: int = 64
    ssd_groups: int = 8
    ssd_state: int = 128
    ssd_conv: int = 4
    d_ff: int = 5632
    ffn_conv: int = 3

    @property
    def sb_width(self):
        return self.sb_heads * self.sb_head_dim

    @property
    def ssd_heads(self):
        return self.ssd_inner // self.ssd_head_dim

    @property
    def ssd_xbc(self):
        return self.ssd_inner + 2 * self.ssd_groups * self.ssd_state


SB_LOG_WEIGHT_FLOOR = -110.0
SB_EAGER_STEPS = 2
VMEM_LIMIT = 56 * 1024 * 1024


def _softplus(x):
    return jnp.maximum(x, 0.0) + jnp.log1p(jnp.exp(-jnp.abs(x)))


def _rows(v, n):
    return jnp.tile(jnp.broadcast_to(v, (8, v.shape[1])), (n // 8, 1))


def _silu(x):
    h = 0.5 * x
    return h + h * jnp.tanh(h)


def _params(*sem):
    return pltpu.CompilerParams(dimension_semantics=sem, vmem_limit_bytes=VMEM_LIMIT)


def _in_proj_kernel(x_ref, g_ref, wa_ref, wb_ref, wdt_ref, p_ref, dt_ref, h_scr, *, na):
    j = pl.program_id(1)

    @pl.when(j == 0)
    def _():
        x = x_ref[...]
        ms = jnp.mean(x * x, axis=-1, keepdims=True)
        h = (x * lax.rsqrt(ms + EPS) * g_ref[...]).astype(BF16)
        h_scr[...] = h
        dt_ref[...] = lax.dot_general(wdt_ref[...].astype(BF16), h, (((1,), (1,)), ((), ())),
                                      preferred_element_type=F32)

    def emit(w_ref):
        acc = lax.dot_general(h_scr[...], w_ref[...], (((1,), (1,)), ((), ())), preferred_element_type=F32)
        for c in range(p_ref.shape[0]):
            p_ref[c] = acc[:, c * LANE:(c + 1) * LANE].astype(BF16)

    pl.when(j < na)(lambda: emit(wa_ref))
    pl.when(j >= na)(lambda: emit(wb_ref))


def _in_proj(x2, gain, w_a, w_b, w_dt, hd, seq, tm, tn):
    m, d = x2.shape
    na, nb = w_a.shape[0] // tn, w_b.shape[0] // tn
    n = w_a.shape[0] + w_b.shape[0]
    tps = seq // tm
    return pl.pallas_call(
        functools.partial(_in_proj_kernel, na=na),
        grid=(m // tm, na + nb),
        in_specs=[
            pl.BlockSpec((tm, d), lambda i, j: (i, 0)),
            pl.BlockSpec((1, d), lambda i, j: (0, 0)),
            pl.BlockSpec((tn, d), lambda i, j: (jnp.minimum(j, na - 1), 0)),
            pl.BlockSpec((tn, d), lambda i, j: (jnp.maximum(j - na, 0), 0)),
            pl.BlockSpec((hd, d), lambda i, j: (0, 0)),
        ],
        out_specs=[
            pl.BlockSpec((tn // LANE, tm, LANE), lambda i, j: (j, i, 0)),
            pl.BlockSpec((None, hd, tm), lambda i, j: (i // tps, 0, i % tps)),
        ],
        out_shape=[
            jax.ShapeDtypeStruct((n // LANE, m, LANE), BF16),
            jax.ShapeDtypeStruct((m // seq, hd, seq), F32),
        ],
        scratch_shapes=[pltpu.VMEM((tm, d), BF16)],
        compiler_params=_params("parallel", "arbitrary"),
        name="in_proj",
    )(x2, gain, w_a, w_b, w_dt)


def _sb_attn_kernel(q_ref, k_ref, v_ref, *rest, tq, nchain, n_cast):
    o_ref = rest[n_cast]
    for src, dst in zip(rest[:n_cast], rest[n_cast + 1:]):
        dst[...] = src[...].astype(dst.dtype)
    s, dh = q_ref.shape
    ngroups = s // (tq * nchain)
    row = lax.broadcasted_iota(jnp.int32, (tq, tq), 0)
    col = lax.broadcasted_iota(jnp.int32, (tq, tq), 1)
    causal = col < row
    suffix = jnp.where(row > col, 1.0, 0.0).astype(BF16)
    sum_rhs = jnp.concatenate([suffix, jnp.ones((tq, tq), BF16)], axis=1)

    def scores(qbs, js):
        return [lax.dot_general(qb, k_ref[pl.ds(pl.multiple_of(j * tq, tq), tq), :], (((1,), (1,)), ((), ())),
                                preferred_element_type=F32) for qb, j in zip(qbs, js)]

    def key_blocks(zs, js, carries, accs, masked):
        starts = [pl.multiple_of(j * tq, tq) for j in js]
        sps, parts = [], []
        for z in zs:
            sp = jnp.maximum(z, 0.0) + jnp.log(1.0 + jnp.exp(-jnp.abs(z)))
            if masked:
                sp = jnp.where(causal, sp, 0.0)
            sps.append(sp)
            parts.append(sp.astype(BF16))
        sums = jnp.dot(jnp.concatenate(parts, axis=0), sum_rhs, preferred_element_type=F32)
        ws = []
        for c, (z, sp) in enumerate(zip(zs, sps)):
            w = jnp.exp(z - sp - sums[c * tq:(c + 1) * tq, :tq] - carries[c])
            if masked:
                w = jnp.where(causal, w, 0.0)
            ws.append(w.astype(BF16))
        new_a = [acc + jnp.dot(w, v_ref[pl.ds(st, tq), :], preferred_element_type=F32)
                 for acc, w, st in zip(accs, ws, starts)]
        new_c = [carries[c] + sums[c * tq:(c + 1) * tq, tq:] for c in range(len(js))]
        return new_c, new_a

    def q_group(gi, _):
        qi = [gi * nchain + c for c in range(nchain)]
        qstart = [pl.multiple_of(i * tq, tq) for i in qi]
        qb = [q_ref[pl.ds(st, tq), :] for st in qstart]

        def retire(t, carries):
            carries = [jnp.where(qi[c] - t >= 0, carries[c], jnp.inf) for c in range(nchain)]
            return carries, jnp.min(functools.reduce(jnp.minimum, carries))

        def back(t):
            return [jnp.maximum(qi[c] - t, 0) for c in range(nchain)]

        zs = [scores(qb, back(t)) for t in range(SB_EAGER_STEPS + 1)]
        carries, accs = key_blocks(zs[0], qi, [jnp.zeros((tq, tq), F32)] * nchain,
                                   [jnp.zeros((tq, dh), F32)] * nchain, True)
        carries, low = retire(1, carries)
        for t in range(1, SB_EAGER_STEPS + 1):
            carries, accs = key_blocks(zs[t], back(t), carries, accs, False)
            carries, low = retire(t + 1, carries)

        def cond(st):
            return st[1] < -SB_LOG_WEIGHT_FLOOR

        def body(st):
            t, _, carries, accs = st
            carries, accs = key_blocks(scores(qb, back(t)), back(t), carries, accs, False)
            carries, low = retire(t + 1, carries)
            return t + 1, low, carries, accs

        _, _, _, accs = lax.while_loop(cond, body, (SB_EAGER_STEPS + 1, low, carries, accs))
        for c in range(nchain):
            o_ref[pl.ds(qstart[c], tq), :] = accs[c].astype(o_ref.dtype)
        return 0

    lax.fori_loop(0, ngroups, q_group, 0)


def _sb_attn(p, dims, batch, seq, tq, nchain, cast_weights):
    h, dh = dims.sb_heads, dims.sb_head_dim
    m = batch * seq
    steps = batch * h
    spec = lambda off: pl.BlockSpec((None, seq, dh), lambda b, hh: (off + hh, b, 0))
    slab = lambda w: pl.BlockSpec((w.shape[0] // steps, w.shape[1]), lambda b, hh: (b * h + hh, 0))
    outs = pl.pallas_call(
        functools.partial(_sb_attn_kernel, tq=tq, nchain=nchain, n_cast=len(cast_weights)),
        grid=(batch, h),
        in_specs=[spec(0), spec(h), spec(2 * h)] + [slab(w) for w in cast_weights],
        out_specs=[pl.BlockSpec((seq, dh), lambda b, hh: (b, hh))] + [slab(w) for w in cast_weights],
        out_shape=[jax.ShapeDtypeStruct((m, h * dh), BF16)]
                  + [jax.ShapeDtypeStruct(w.shape, BF16) for w in cast_weights],
        compiler_params=_params("arbitrary", "arbitrary"),
        name="sb_attn",
    )(p, p, p, *cast_weights)
    return outs[0], outs[1:]


def _ssd_kernel(x_ref, b_ref, c_ref, z_ref, dt_ref, wx_ref, wb_ref, wc_ref, bx_ref, bb_ref, bc_ref,
                dtb_ref, alog_ref, dskip_ref, norm_ref, shift_ref, y_ref, halo_scr, state_scr, *, hd, kconv, gps):
    _, nbatch, l, _ = x_ref.shape
    nxb = x_ref.shape[0] // gps
    hg = dt_ref.shape[1] // gps
    inner = nxb * LANE
    n_state = b_ref.shape[-1]
    width = inner + 2 * n_state
    assert l == LANE and 2 * hd == LANE and n_state == LANE

    @pl.when(pl.program_id(1) == 0)
    def _():
        halo_scr[...] = jnp.zeros_like(halo_scr)
        state_scr[...] = jnp.zeros_like(state_scr)

    rows = functools.partial(_rows, n=l)
    lane = lax.broadcasted_iota(jnp.int32, (l, LANE), 1)
    lane_t = lax.broadcasted_iota(jnp.int32, (hg, l), 1)
    low_half = lane < hd
    tril = lane <= lax.broadcasted_iota(jnp.int32, (l, l), 0)
    pad = jnp.zeros((LANE - hg, l), F32)

    gcols = lambda ref, gi, w: ref[:, gi * w:(gi + 1) * w]
    taps8, taps, bcat, dskip, norm_g, neg_a, dtb = [], [], [], [], [], [], []
    for gi in range(gps):
        wcat = jnp.concatenate([gcols(wx_ref, gi, inner), gcols(wb_ref, gi, n_state), gcols(wc_ref, gi, n_state)],
                               axis=1)
        wcat16 = wcat.astype(BF16)
        taps8.append([jnp.broadcast_to(wcat[k:k + 1, :], (8, width)) for k in range(kconv)])
        taps.append([jnp.tile(jnp.broadcast_to(wcat16[k:k + 1, :], (16, width)), (l // 16, 1))
                     for k in range(kconv)])
        bcat.append(rows(jnp.concatenate([gcols(bx_ref, gi, inner), gcols(bb_ref, gi, n_state),
                                          gcols(bc_ref, gi, n_state)], axis=1)))
        dskip.append(rows(gcols(dskip_ref, gi, inner)))
        norm_g.append(rows(gcols(norm_ref, gi, inner)))
        neg_a.append(-jnp.exp(alog_ref[gi * hg:(gi + 1) * hg, :]))
        dtb.append(dtb_ref[gi * hg:(gi + 1) * hg, :])

    chains = [(gi, b) for gi in range(gps) for b in range(nbatch)]

    acts = []
    for gi, b in chains:
        raw = jnp.concatenate([x_ref[gi * nxb + c, b] for c in range(nxb)] + [b_ref[gi, b], c_ref[gi, b]],
                              axis=1)
        body = jnp.dot(shift_ref[...], jnp.concatenate([raw * taps[gi][k] for k in range(kconv)], axis=0),
                       preferred_element_type=F32)
        head = jnp.concatenate([halo_scr[gi, b], raw[:8].astype(F32)], axis=0)
        halo_scr[gi, b] = raw[l - 8:, :].astype(F32)
        first = jnp.zeros((8, width), F32)
        for k in range(kconv):
            off = 8 - (kconv - 1) + k
            first = first + taps8[gi][k] * head[off:off + 8, :]
        acts.append(_silu(jnp.concatenate([first, body[8:]], axis=0) + bcat[gi]))
    x16 = [a[:, :inner].astype(BF16) for a in acts]
    bms = [a[:, inner:inner + n_state] for a in acts]
    cms = [a[:, inner + n_state:] for a in acts]

    cbs = [lax.dot_general(cm.astype(BF16), bm.astype(BF16), (((1,), (1,)), ((), ())),
                           preferred_element_type=F32) for cm, bm in zip(cms, bms)]
    bm_ts = [bm.T for bm in bms]
    src_rows, coef_rows, keep_rows, cs_cols = [], [], [], []
    for gi, b in chains:
        dt = _softplus(dt_ref[b, gi * hg:(gi + 1) * hg, :] + dtb[gi])
        cs = dt * neg_a[gi]
        sh = 1
        while sh < l:
            cs = cs + jnp.where(lane_t >= sh, pltpu.roll(cs, sh, axis=1), 0.0)
            sh *= 2
        cs_last = jnp.broadcast_to(cs[:, l - 1:l], (hg, l))
        src_rows.append(cs - jnp.log(dt))
        coef_rows.append(dt * jnp.exp(cs_last - cs))
        keep_rows.append(jnp.exp(cs_last))
        cs_cols.append(jnp.concatenate([cs, pad], axis=0).T)

    states = [state_scr[gi, b] for gi, b in chains]
    y_blocks = [[] for _ in chains]
    new_state = [[] for _ in chains]
    for blk in range(nxb):
        for n in range(len(chains)):
            lhs, scaled_bt = [], []
            for h in (2 * blk, 2 * blk + 1):
                ccol = jnp.broadcast_to(cs_cols[n][:, h:h + 1], (l, LANE))
                seg = ccol - rows(src_rows[n][h:h + 1, :])
                w_in = (cbs[n] * jnp.exp(jnp.where(tril, seg, -jnp.inf))).astype(BF16)
                w_prev = (cms[n] * jnp.exp(ccol)).astype(BF16)
                lhs.append(jnp.concatenate([w_in, w_prev], axis=1))
                scaled_bt.append((bm_ts[n] * rows(coef_rows[n][h:h + 1, :])).astype(BF16))
            xblk = x16[n][:, blk * LANE:(blk + 1) * LANE]
            sblk = states[n][:, blk * LANE:(blk + 1) * LANE]
            rhs = jnp.concatenate([xblk, sblk.astype(BF16)], axis=0)
            yy = jnp.dot(jnp.concatenate(lhs, axis=0), rhs, preferred_element_type=F32)
            y_blocks[n].append(jnp.where(low_half, yy[:l], yy[l:]))
            ss = jnp.dot(jnp.concatenate(scaled_bt, axis=0), xblk, preferred_element_type=F32)
            kr = keep_rows[n]
            keep = jnp.where(low_half[:1], kr[2 * blk:2 * blk + 1, :], kr[2 * blk + 1:2 * blk + 2, :])
            new_state[n].append(sblk * rows(keep) + jnp.where(low_half, ss[:n_state], ss[n_state:]))

    for n, (gi, b) in enumerate(chains):
        state_scr[gi, b] = jnp.concatenate(new_state[n], axis=1)
        y = jnp.concatenate(y_blocks[n], axis=1) + acts[n][:, :inner] * dskip[gi]
        zf = jnp.concatenate([z_ref[gi * nxb + c, b] for c in range(nxb)], axis=1).astype(F32)
        y = y * _silu(zf)
        ms = jnp.mean(y * y, axis=-1, keepdims=True)
        y = y * lax.rsqrt(ms + EPS) * norm_g[gi]
        y_ref[b, :, gi * inner:(gi + 1) * inner] = y.astype(y_ref.dtype)


def _ssd(p, dt_t, conv_w, conv_b, dtb, alog, dskip_e, norm_g, dims, batch, seq, chunk, blk_off, gps):
    g = dims.ssd_groups
    inner_g = dims.ssd_inner // g
    nxb = inner_g // LANE
    hg = dims.ssd_heads // g
    n_state = dims.ssd_state
    nt = seq // chunk
    kconv = conv_w.shape[0]
    z_off, x_off, b_off, c_off = blk_off
    p4 = p.reshape(p.shape[0], batch, seq, LANE)
    sb_off = dims.ssd_inner // n_state
    assert all(o % (gps * nxb) == 0 for o in (z_off, x_off)) and all(o % gps == 0 for o in (b_off, c_off, sb_off, g))
    shift = np.zeros((chunk, kconv * chunk), np.float32)
    for k in range(kconv):
        t = np.arange(kconv - 1 - k, chunk)
        shift[t, chunk * k + t - (kconv - 1) + k] = 1.0
    shift = jnp.asarray(shift, BF16)
    in_specs = [
        pl.BlockSpec((gps * nxb, batch, chunk, LANE), lambda gg, t: (x_off // (gps * nxb) + gg, 0, t, 0)),
        pl.BlockSpec((gps, batch, chunk, LANE), lambda gg, t: (b_off // gps + gg, 0, t, 0)),
        pl.BlockSpec((gps, batch, chunk, LANE), lambda gg, t: (c_off // gps + gg, 0, t, 0)),
        pl.BlockSpec((gps * nxb, batch, chunk, LANE), lambda gg, t: (z_off // (gps * nxb) + gg, 0, t, 0)),
        pl.BlockSpec((batch, gps * hg, chunk), lambda gg, t: (0, gg, t)),
        pl.BlockSpec((kconv, gps * inner_g), lambda gg, t: (0, gg)),
        pl.BlockSpec((kconv, gps * n_state), lambda gg, t: (0, sb_off // gps + gg)),
        pl.BlockSpec((kconv, gps * n_state), lambda gg, t: (0, (sb_off + g) // gps + gg)),
        pl.BlockSpec((1, gps * inner_g), lambda gg, t: (0, gg)),
        pl.BlockSpec((1, gps * n_state), lambda gg, t: (0, sb_off // gps + gg)),
        pl.BlockSpec((1, gps * n_state), lambda gg, t: (0, (sb_off + g) // gps + gg)),
        pl.BlockSpec((gps * hg, LANE), lambda gg, t: (gg, 0)),
        pl.BlockSpec((gps * hg, LANE), lambda gg, t: (gg, 0)),
        pl.BlockSpec((1, gps * inner_g), lambda gg, t: (0, gg)),
        pl.BlockSpec((1, gps * inner_g), lambda gg, t: (0, gg)),
        pl.BlockSpec(shift.shape, lambda gg, t: (0, 0)),
    ]
    y = pl.pallas_call(
        functools.partial(_ssd_kernel, hd=dims.ssd_head_dim, kconv=kconv, gps=gps),
        grid=(g // gps, nt),
        in_specs=in_specs,
        out_specs=pl.BlockSpec((batch, chunk, gps * inner_g), lambda gg, t: (0, t, gg)),
        out_shape=jax.ShapeDtypeStruct((batch, seq, dims.ssd_inner), BF16),
        scratch_shapes=[pltpu.VMEM((gps, batch, 8, inner_g + 2 * n_state), F32),
                        pltpu.VMEM((gps, batch, n_state, inner_g), F32)],
        compiler_params=_params("parallel", "arbitrary"),
        name="ssd",
    )(p4, p4, p4, p4, dt_t, conv_w, conv_w, conv_w, conv_b, conv_b, conv_b, dtb, alog, dskip_e, norm_g, shift)
    return y.reshape(batch * seq, dims.ssd_inner)


def _merge_kernel(o_ref, y_ref, gsb_ref, gssd_ref, wsb_ref, wssd_ref, out_ref):
    a = jnp.dot(o_ref[...], wsb_ref[...], preferred_element_type=F32)
    b = jnp.dot(y_ref[...], wssd_ref[...], preferred_element_type=F32)
    gsb = jnp.concatenate([gsb_ref[c] for c in range(gsb_ref.shape[0])], axis=1).astype(F32)
    gssd = jnp.concatenate([gssd_ref[c] for c in range(gssd_ref.shape[0])], axis=1).astype(F32)
    out_ref[...] = (jax.nn.sigmoid(gsb) * a + jax.nn.sigmoid(gssd) * b).astype(out_ref.dtype)


def _merge(o_sb, y, p, w_sb, w_ssd, gsb_off, gssd_off, tm, tn):
    m, ko = o_sb.shape
    ky = y.shape[1]
    d = w_sb.shape[1]
    nb = tn // LANE
    return pl.pallas_call(
        _merge_kernel,
        grid=(m // tm, d // tn),
        in_specs=[
            pl.BlockSpec((tm, ko), lambda i, j: (i, 0)),
            pl.BlockSpec((tm, ky), lambda i, j: (i, 0)),
            pl.BlockSpec((nb, tm, LANE), lambda i, j: (gsb_off // nb + j, i, 0)),
            pl.BlockSpec((nb, tm, LANE), lambda i, j: (gssd_off // nb + j, i, 0)),
            pl.BlockSpec((ko, tn), lambda i, j: (0, j)),
            pl.BlockSpec((ky, tn), lambda i, j: (0, j)),
        ],
        out_specs=pl.BlockSpec((tm, tn), lambda i, j: (i, j)),
        out_shape=jax.ShapeDtypeStruct((m, d), BF16),
        compiler_params=_params("parallel", "arbitrary"),
        name="merge",
    )(o_sb, y, p, p, w_sb, w_ssd)


def _proj_norm_res_kernel(a_ref, w_ref, g_ref, x_ref, out_ref, f_scr):
    nj, tm, tn = f_scr.shape
    j = pl.program_id(1)
    f_scr[j] = jnp.dot(a_ref[...], w_ref[...], preferred_element_type=F32)

    @pl.when(j == nj - 1)
    def _():
        ss = jnp.zeros((tm, 1), F32)
        for c in range(nj):
            f = f_scr[c]
            ss = ss + jnp.sum(f * f, axis=-1, keepdims=True)
        inv = lax.rsqrt(ss * (1.0 / (nj * tn)) + EPS)
        for c in range(nj):
            cols = slice(c * tn, (c + 1) * tn)
            out_ref[:, cols] = x_ref[:, cols] + f_scr[c] * inv * _rows(g_ref[:, cols], tm)


def _proj_norm_res(a, w, gain, x2, tm, tn, name):
    m, kdim = a.shape
    d = w.shape[1]
    return pl.pallas_call(
        _proj_norm_res_kernel,
        grid=(m // tm, d // tn),
        in_specs=[
            pl.BlockSpec((tm, kdim), lambda i, j: (i, 0)),
            pl.BlockSpec((kdim, tn), lambda i, j: (0, j), pipeline_mode=pl.Buffered(1 if tn == d else 2)),
            pl.BlockSpec((1, d), lambda i, j: (0, 0)),
            pl.BlockSpec((tm, d), lambda i, j: (i, 0)),
        ],
        out_specs=pl.BlockSpec((tm, d), lambda i, j: (i, 0)),
        out_shape=jax.ShapeDtypeStruct((m, d), F32),
        scratch_shapes=[pltpu.VMEM((d // tn, tm, tn), F32)],
        compiler_params=_params("parallel", "arbitrary"),
        name=name,
    )(a, w, gain, x2)


FFN_HALO = 16


def _ffn_up_kernel(x_ref, halo_ref, g_ref, wg_ref, wv_ref, cwg_ref, cwv_ref, cbg_ref, cbv_ref, out_ref, h_scr,
                   *, tiles_per_seq, kconv):
    tm = x_ref.shape[0]

    def norm(v):
        ms = jnp.mean(v * v, axis=-1, keepdims=True)
        return (v * lax.rsqrt(ms + EPS) * g_ref[...]).astype(BF16)

    @pl.when(pl.program_id(1) == 0)
    def _():
        seq_start = (pl.program_id(0) % tiles_per_seq) == 0
        h_scr[:FFN_HALO, :] = jnp.where(seq_start, jnp.zeros((), BF16), norm(halo_ref[...]))
        h_scr[FFN_HALO:, :] = norm(x_ref[...])

    h = h_scr[...]

    def conv(w_ref, cw_ref, cb_ref):
        up = jnp.dot(h, w_ref[...], preferred_element_type=F32)
        u = _rows(cb_ref[...], tm)
        for k in range(kconv):
            off = FFN_HALO - (kconv - 1) + k
            u = u + _rows(cw_ref[k:k + 1, :], tm) * up[off:off + tm, :]
        return u

    gate = conv(wg_ref, cwg_ref, cbg_ref)
    val = conv(wv_ref, cwv_ref, cbv_ref)
    c = math.sqrt(2.0 / math.pi)
    half = 0.5 * gate
    act = half + half * jnp.tanh(gate * (c + (c * 0.044715) * (gate * gate)))
    out_ref[...] = (act * val).astype(out_ref.dtype)


def _ffn_up(x1, gain, w_up, conv_w, conv_b, seq, tm, tn):
    m, d = x1.shape
    dff = w_up.shape[1] // 2
    nj = dff // tn
    kconv = conv_w.shape[0]
    hb = tm // FFN_HALO
    return pl.pallas_call(
        functools.partial(_ffn_up_kernel, tiles_per_seq=seq // tm, kconv=kconv),
        grid=(m // tm, nj),
        in_specs=[
            pl.BlockSpec((tm, d), lambda i, j: (i, 0)),
            pl.BlockSpec((FFN_HALO, d), lambda i, j: (jnp.maximum(i * hb - 1, 0), 0)),
            pl.BlockSpec((1, d), lambda i, j: (0, 0)),
            pl.BlockSpec((d, tn), lambda i, j: (0, j)),
            pl.BlockSpec((d, tn), lambda i, j: (0, nj + j)),
            pl.BlockSpec((kconv, tn), lambda i, j: (0, j)),
            pl.BlockSpec((kconv, tn), lambda i, j: (0, nj + j)),
            pl.BlockSpec((1, tn), lambda i, j: (0, j)),
            pl.BlockSpec((1, tn), lambda i, j: (0, nj + j)),
        ],
        out_specs=pl.BlockSpec((tm, tn), lambda i, j: (i, j)),
        out_shape=jax.ShapeDtypeStruct((m, dff), BF16),
        scratch_shapes=[pltpu.VMEM((FFN_HALO + tm, d), BF16)],
        compiler_params=_params("parallel", "arbitrary"),
        name="ffn_up",
    )(x1, x1, gain, w_up, w_up, conv_w, conv_w, conv_b, conv_b)


def _largest_tile(total, want, quantum=LANE):
    t = min(want, total)
    while total % t or t % quantum:
        t -= quantum
    return t


def _block(x, norm_mix_pre, w_in, ssd_conv_w, ssd_conv_b, dt_bias, a_log, d_skip, ssd_norm, w_sb_proj,
           w_ssd_proj, w_out, norm_mix_post, norm_ffn_pre, w_up, ffn_conv_w, ffn_conv_b, w_down, norm_ffn_post,
           dims, tiles):
    batch, seq, d = x.shape
    m = batch * seq
    sbw, inner, xbc, heads = dims.sb_width, dims.ssd_inner, dims.ssd_xbc, dims.ssd_heads
    x2 = x.reshape(m, d)
    row = lambda v: v.reshape(1, -1)

    dt_lo = 3 * sbw + inner + xbc
    w_in_t = w_in.T
    row_scale = jnp.where(jnp.arange(dt_lo) < sbw, dims.sb_head_dim ** -0.5, 1.0).astype(F32)[:, None]
    w_a = (w_in_t[:dt_lo] * row_scale).astype(BF16)
    w_b = w_in_t[dt_lo + heads:].astype(BF16)
    w_dt = w_in_t[dt_lo:dt_lo + heads]
    z_off = 3 * sbw // LANE
    x_off = z_off + inner // LANE
    b_off = x_off + inner // LANE
    c_off = b_off + dims.ssd_groups * dims.ssd_state // LANE
    gsb_off = c_off + dims.ssd_groups * dims.ssd_state // LANE
    gssd_off = gsb_off + d // LANE

    tm = _largest_tile(seq, tiles["tm"], 16)
    p, dt_t = _in_proj(x2, row(norm_mix_pre), w_a, w_b, w_dt, heads, seq, tm,
                       _largest_tile(math.gcd(w_a.shape[0], w_b.shape[0]), tiles["tn_in"]))

    o_sb, (w_sb16, w_ssd16, w_out16, w_up16, w_down16) = _sb_attn(
        p, dims, batch, seq, min(tiles["tq"], seq), tiles["sb_chains"],
        [w_sb_proj, w_ssd_proj, w_out, w_up, w_down])

    chunk = min(tiles["chunk"], seq)
    bcast = lambda v: jnp.broadcast_to(v.reshape(-1, 1), (heads, LANE))
    y = _ssd(p, dt_t, ssd_conv_w, row(ssd_conv_b), bcast(dt_bias), bcast(a_log),
             row(jnp.repeat(d_skip, dims.ssd_head_dim)), row(ssd_norm), dims, batch, seq, chunk,
             (z_off, x_off, b_off, c_off), tiles["ssd_groups_per_step"])

    merged = _merge(o_sb, y, p, w_sb16, w_ssd16, gsb_off, gssd_off,
                    _largest_tile(m, tiles["tm_merge"], 16), _largest_tile(d, tiles["tn_merge"]))
    tm2 = _largest_tile(m, tiles["tm_res"], 16)
    x1 = _proj_norm_res(merged, w_out16, row(norm_mix_post), x2, tm2, d, "out_proj")

    tm_up = _largest_tile(seq, tiles["tm"], 16)
    act = _ffn_up(x1, row(norm_ffn_pre), w_up16, ffn_conv_w, row(ffn_conv_b), seq, tm_up,
                  _largest_tile(dims.d_ff, tiles["tn_up"]))
    out = _proj_norm_res(act, w_down16, row(norm_ffn_post), x1, tm2,
                         _largest_tile(d, tiles["tn_down"]), "ffn_down")
    return out.reshape(batch, seq, d)


TILES = dict(tm=1024, tn_in=1024, tq=128, sb_chains=8, chunk=128, ssd_groups_per_step=4, tm_merge=1024, tn_merge=512, tm_res=512, tn_up=512, tn_down=2048)


def kernel(x, norm_mix_pre, w_in, ssd_conv_w, ssd_conv_b, dt_bias, a_log, d_skip, ssd_norm, w_sb_proj, w_ssd_proj,
           w_out, norm_mix_post, norm_ffn_pre, w_up, ffn_conv_w, ffn_conv_b, w_down, norm_ffn_post):
    dims = Dims()
    args = (norm_mix_pre, w_in, ssd_conv_w, ssd_conv_b, dt_bias, a_log, d_skip, ssd_norm, w_sb_proj, w_ssd_proj,
            w_out, norm_mix_post, norm_ffn_pre, w_up, ffn_conv_w, ffn_conv_b, w_down, norm_ffn_post)
    for layer in range(w_in.shape[0]):
        x = _block(x, *(a[layer] for a in args), dims, TILES)
    return x
```

```python
import dataclasses
import functools
import math

import numpy as np

import jax
import jax.numpy as jnp
from jax import lax
from jax.experimental import pallas as pl
from jax.experimental.pallas import tpu as pltpu

F32 = jnp.float32
BF16 = jnp.bfloat16
LANE = 128
EPS = 1e-6


@dataclasses.dataclass(frozen=True)
class Dims:
    d_model: int = 2048
    sb_heads: int = 16
    sb_head_dim: int = 128
    ssd_inner: int = 4096
    ssd_head_dim: int = 64
    ssd_groups: int = 8
    ssd_state: int = 128
    ssd_conv: int = 4
    d_ff: int = 5632
    ffn_conv: int = 3

    @property
    def sb_width(self):
        return self.sb_heads * self.sb_head_dim

    @property
    def ssd_heads(self):
        return self.ssd_inner // self.ssd_head_dim

    @property
    def ssd_xbc(self):
        return self.ssd_inner + 2 * self.ssd_groups * self.ssd_state


SB_LOG_WEIGHT_FLOOR = -110.0
SB_EAGER_STEPS = 2
VMEM_LIMIT = 60 * 1024 * 1024


def _softplus(x):
    return jnp.maximum(x, 0.0) + jnp.log1p(jnp.exp(-jnp.abs(x)))


def _rows(v, n):
    return jnp.tile(jnp.broadcast_to(v, (8, v.shape[1])), (n // 8, 1))


def _silu(x):
    h = 0.5 * x
    return h + h * jnp.tanh(h)


def _params(*sem):
    return pltpu.CompilerParams(dimension_semantics=sem, vmem_limit_bytes=VMEM_LIMIT)


def _in_proj_kernel(x_ref, g_ref, wa_ref, wb_ref, wdt_ref, p_ref, dt_ref, h_scr, *, na):
    j = pl.program_id(1)

    @pl.when(j == 0)
    def _():
        x = x_ref[...]
        ms = jnp.mean(x * x, axis=-1, keepdims=True)
        h = (x * lax.rsqrt(ms + EPS) * g_ref[...]).astype(BF16)
        h_scr[...] = h
        dt_ref[...] = lax.dot_general(wdt_ref[...].astype(BF16), h, (((1,), (1,)), ((), ())),
                                      preferred_element_type=F32)

    def emit(w_ref):
        acc = lax.dot_general(h_scr[...], w_ref[...], (((1,), (1,)), ((), ())), preferred_element_type=F32)
        for c in range(p_ref.shape[0]):
            p_ref[c] = acc[:, c * LANE:(c + 1) * LANE].astype(BF16)

    pl.when(j < na)(lambda: emit(wa_ref))
    pl.when(j >= na)(lambda: emit(wb_ref))


def _in_proj(x2, gain, w_a, w_b, w_dt, hd, seq, tm, tn):
    m, d = x2.shape
    na, nb = w_a.shape[0] // tn, w_b.shape[0] // tn
    n = w_a.shape[0] + w_b.shape[0]
    tps = seq // tm
    return pl.pallas_call(
        functools.partial(_in_proj_kernel, na=na),
        grid=(m // tm, na + nb),
        in_specs=[
            pl.BlockSpec((tm, d), lambda i, j: (i, 0)),
            pl.BlockSpec((1, d), lambda i, j: (0, 0)),
            pl.BlockSpec((tn, d), lambda i, j: (jnp.minimum(j, na - 1), 0)),
            pl.BlockSpec((tn, d), lambda i, j: (jnp.maximum(j - na, 0), 0)),
            pl.BlockSpec((hd, d), lambda i, j: (0, 0)),
        ],
        out_specs=[
            pl.BlockSpec((tn // LANE, tm, LANE), lambda i, j: (j, i, 0)),
            pl.BlockSpec((None, hd, tm), lambda i, j: (i // tps, 0, i % tps)),
        ],
        out_shape=[
            jax.ShapeDtypeStruct((n // LANE, m, LANE), BF16),
            jax.ShapeDtypeStruct((m // seq, hd, seq), F32),
        ],
        scratch_shapes=[pltpu.VMEM((tm, d), BF16)],
        compiler_params=_params("parallel", "arbitrary"),
        name="in_proj",
    )(x2, gain, w_a, w_b, w_dt)


def _sb_attn_kernel(q_ref, k_ref, v_ref, *rest, tq, nchain, n_cast):
    o_ref = rest[n_cast]
    for src, dst in zip(rest[:n_cast], rest[n_cast + 1:]):
        dst[...] = src[...].astype(dst.dtype)
    s, dh = q_ref.shape
    ngroups = s // (tq * nchain)
    row = lax.broadcasted_iota(jnp.int32, (tq, tq), 0)
    col = lax.broadcasted_iota(jnp.int32, (tq, tq), 1)
    causal = col < row
    suffix = jnp.where(row > col, 1.0, 0.0).astype(BF16)
    sum_rhs = jnp.concatenate([suffix, jnp.ones((tq, tq), BF16)], axis=1)

    def scores(qbs, js):
        return [lax.dot_general(qb, k_ref[pl.ds(pl.multiple_of(j * tq, tq), tq), :], (((1,), (1,)), ((), ())),
                                preferred_element_type=F32) for qb, j in zip(qbs, js)]

    def key_blocks(zs, js, carries, accs, masked):
        starts = [pl.multiple_of(j * tq, tq) for j in js]
        sps, parts = [], []
        for z in zs:
            sp = jnp.maximum(z, 0.0) + jnp.log(1.0 + jnp.exp(-jnp.abs(z)))
            if masked:
                sp = jnp.where(causal, sp, 0.0)
            sps.append(sp)
            parts.append(sp.astype(BF16))
        sums = jnp.dot(jnp.concatenate(parts, axis=0), sum_rhs, preferred_element_type=F32)
        ws = []
        for c, (z, sp) in enumerate(zip(zs, sps)):
            w = jnp.exp(z - sp - sums[c * tq:(c + 1) * tq, :tq] - carries[c])
            if masked:
                w = jnp.where(causal, w, 0.0)
            ws.append(w.astype(BF16))
        new_a = [acc + jnp.dot(w, v_ref[pl.ds(st, tq), :], preferred_element_type=F32)
                 for acc, w, st in zip(accs, ws, starts)]
        new_c = [carries[c] + sums[c * tq:(c + 1) * tq, tq:] for c in range(len(js))]
        return new_c, new_a

    def q_group(gi, _):
        qi = [gi * nchain + c for c in range(nchain)]
        qstart = [pl.multiple_of(i * tq, tq) for i in qi]
        qb = [q_ref[pl.ds(st, tq), :] for st in qstart]

        def retire(t, carries):
            carries = [jnp.where(qi[c] - t >= 0, carries[c], jnp.inf) for c in range(nchain)]
            return carries, jnp.min(functools.reduce(jnp.minimum, carries))

        def back(t):
            return [jnp.maximum(qi[c] - t, 0) for c in range(nchain)]

        zs = [scores(qb, back(t)) for t in range(SB_EAGER_STEPS + 1)]
        carries, accs = key_blocks(zs[0], qi, [jnp.zeros((tq, tq), F32)] * nchain,
                                   [jnp.zeros((tq, dh), F32)] * nchain, True)
        carries, low = retire(1, carries)
        for t in range(1, SB_EAGER_STEPS + 1):
            carries, accs = key_blocks(zs[t], back(t), carries, accs, False)
            carries, low = retire(t + 1, carries)

        def cond(st):
            return st[1] < -SB_LOG_WEIGHT_FLOOR

        def body(st):
            t, _, carries, accs = st
            carries, accs = key_blocks(scores(qb, back(t)), back(t), carries, accs, False)
            carries, low = retire(t + 1, carries)
            return t + 1, low, carries, accs

        _, _, _, accs = lax.while_loop(cond, body, (SB_EAGER_STEPS + 1, low, carries, accs))
        for c in range(nchain):
            o_ref[pl.ds(qstart[c], tq), :] = accs[c].astype(o_ref.dtype)
        return 0

    lax.fori_loop(0, ngroups, q_group, 0)


def _sb_attn(p, dims, batch, seq, tq, nchain, cast_weights):
    h, dh = dims.sb_heads, dims.sb_head_dim
    m = batch * seq
    steps = batch * h
    spec = lambda off: pl.BlockSpec((None, seq, dh), lambda b, hh: (off + hh, b, 0))
    slab = lambda w: pl.BlockSpec((w.shape[0] // steps, w.shape[1]), lambda b, hh: (b * h + hh, 0))
    outs = pl.pallas_call(
        functools.partial(_sb_attn_kernel, tq=tq, nchain=nchain, n_cast=len(cast_weights)),
        grid=(batch, h),
        in_specs=[spec(0), spec(h), spec(2 * h)] + [slab(w) for w in cast_weights],
        out_specs=[pl.BlockSpec((seq, dh), lambda b, hh: (b, hh))] + [slab(w) for w in cast_weights],
        out_shape=[jax.ShapeDtypeStruct((m, h * dh), BF16)]
                  + [jax.ShapeDtypeStruct(w.shape, BF16) for w in cast_weights],
        compiler_params=_params("arbitrary", "arbitrary"),
        name="sb_attn",
    )(p, p, p, *cast_weights)
    return outs[0], outs[1:]


def _ssd_kernel(x_ref, b_ref, c_ref, z_ref, dt_ref, wx_ref, wb_ref, wc_ref, bx_ref, bb_ref, bc_ref,
                dtb_ref, alog_ref, dskip_ref, norm_ref, shift_ref, y_ref, halo_scr, state_scr, *, hd, kconv, gps):
    _, nbatch, l, _ = x_ref.shape
    nxb = x_ref.shape[0] // gps
    hg = dt_ref.shape[1] // gps
    inner = nxb * LANE
    n_state = b_ref.shape[-1]
    width = inner + 2 * n_state
    assert l == LANE and 2 * hd == LANE and n_state == LANE

    @pl.when(pl.program_id(1) == 0)
    def _():
        halo_scr[...] = jnp.zeros_like(halo_scr)
        state_scr[...] = jnp.zeros_like(state_scr)

    rows = functools.partial(_rows, n=l)
    lane = lax.broadcasted_iota(jnp.int32, (l, LANE), 1)
    lane_t = lax.broadcasted_iota(jnp.int32, (hg, l), 1)
    low_half = lane < hd
    tril = lane <= lax.broadcasted_iota(jnp.int32, (l, l), 0)
    pad = jnp.zeros((LANE - hg, l), F32)

    gcols = lambda ref, gi, w: ref[:, gi * w:(gi + 1) * w]
    taps8, taps, bcat, dskip, norm_g, neg_a, dtb = [], [], [], [], [], [], []
    for gi in range(gps):
        wcat = jnp.concatenate([gcols(wx_ref, gi, inner), gcols(wb_ref, gi, n_state), gcols(wc_ref, gi, n_state)],
                               axis=1)
        wcat16 = wcat.astype(BF16)
        taps8.append([jnp.broadcast_to(wcat[k:k + 1, :], (8, width)) for k in range(kconv)])
        taps.append([jnp.tile(jnp.broadcast_to(wcat16[k:k + 1, :], (16, width)), (l // 16, 1))
                     for k in range(kconv)])
        bcat.append(rows(jnp.concatenate([gcols(bx_ref, gi, inner), gcols(bb_ref, gi, n_state),
                                          gcols(bc_ref, gi, n_state)], axis=1)))
        dskip.append(rows(gcols(dskip_ref, gi, inner)))
        norm_g.append(rows(gcols(norm_ref, gi, inner)))
        neg_a.append(-jnp.exp(alog_ref[gi * hg:(gi + 1) * hg, :]))
        dtb.append(dtb_ref[gi * hg:(gi + 1) * hg, :])

    chains = [(gi, b) for gi in range(gps) for b in range(nbatch)]

    acts = []
    for gi, b in chains:
        raw = jnp.concatenate([x_ref[gi * nxb + c, b] for c in range(nxb)] + [b_ref[gi, b], c_ref[gi, b]],
                              axis=1)
        body = jnp.dot(shift_ref[...], jnp.concatenate([raw * taps[gi][k] for k in range(kconv)], axis=0),
                       preferred_element_type=F32)
        head = jnp.concatenate([halo_scr[gi, b], raw[:8].astype(F32)], axis=0)
        halo_scr[gi, b] = raw[l - 8:, :].astype(F32)
        first = jnp.zeros((8, width), F32)
        for k in range(kconv):
            off = 8 - (kconv - 1) + k
            first = first + taps8[gi][k] * head[off:off + 8, :]
        acts.append(_silu(jnp.concatenate([first, body[8:]], axis=0) + bcat[gi]))
    x16 = [a[:, :inner].astype(BF16) for a in acts]
    bms = [a[:, inner:inner + n_state] for a in acts]
    cms = [a[:, inner + n_state:] for a in acts]

    cbs = [lax.dot_general(cm.astype(BF16), bm.astype(BF16), (((1,), (1,)), ((), ())),
                           preferred_element_type=F32) for cm, bm in zip(cms, bms)]
    bm_ts = [bm.T for bm in bms]
    src_rows, coef_rows, keep_rows, cs_cols = [], [], [], []
    for gi, b in chains:
        dt = _softplus(dt_ref[b, gi * hg:(gi + 1) * hg, :] + dtb[gi])
        cs = dt * neg_a[gi]
        sh = 1
        while sh < l:
            cs = cs + jnp.where(lane_t >= sh, pltpu.roll(cs, sh, axis=1), 0.0)
            sh *= 2
        cs_last = jnp.broadcast_to(cs[:, l - 1:l], (hg, l))
        src_rows.append(cs - jnp.log(dt))
        coef_rows.append(dt * jnp.exp(cs_last - cs))
        keep_rows.append(jnp.exp(cs_last))
        cs_cols.append(jnp.concatenate([cs, pad], axis=0).T)

    states = [state_scr[gi, b] for gi, b in chains]
    y_blocks = [[] for _ in chains]
    new_state = [[] for _ in chains]
    for blk in range(nxb):
        for n in range(len(chains)):
            lhs, scaled_bt = [], []
            for h in (2 * blk, 2 * blk + 1):
                ccol = jnp.broadcast_to(cs_cols[n][:, h:h + 1], (l, LANE))
                seg = ccol - rows(src_rows[n][h:h + 1, :])
                w_in = (cbs[n] * jnp.exp(jnp.where(tril, seg, -jnp.inf))).astype(BF16)
                w_prev = (cms[n] * jnp.exp(ccol)).astype(BF16)
                lhs.append(jnp.concatenate([w_in, w_prev], axis=1))
                scaled_bt.append((bm_ts[n] * rows(coef_rows[n][h:h + 1, :])).astype(BF16))
            xblk = x16[n][:, blk * LANE:(blk + 1) * LANE]
            sblk = states[n][:, blk * LANE:(blk + 1) * LANE]
            rhs = jnp.concatenate([xblk, sblk.astype(BF16)], axis=0)
            yy = jnp.dot(jnp.concatenate(lhs, axis=0), rhs, preferred_element_type=F32)
            y_blocks[n].append(jnp.where(low_half, yy[:l], yy[l:]))
            ss = jnp.dot(jnp.concatenate(scaled_bt, axis=0), xblk, preferred_element_type=F32)
            kr = keep_rows[n]
            keep = jnp.where(low_half[:1], kr[2 * blk:2 * blk + 1, :], kr[2 * blk + 1:2 * blk + 2, :])
            new_state[n].append(sblk * rows(keep) + jnp.where(low_half, ss[:n_state], ss[n_state:]))

    for n, (gi, b) in enumerate(chains):
        state_scr[gi, b] = jnp.concatenate(new_state[n], axis=1)
        y = jnp.concatenate(y_blocks[n], axis=1) + acts[n][:, :inner] * dskip[gi]
        zf = jnp.concatenate([z_ref[gi * nxb + c, b] for c in range(nxb)], axis=1).astype(F32)
        y = y * _silu(zf)
        ms = jnp.mean(y * y, axis=-1, keepdims=True)
        y = y * lax.rsqrt(ms + EPS) * norm_g[gi]
        y_ref[b, :, gi * inner:(gi + 1) * inner] = y.astype(y_ref.dtype)


def _ssd(p, dt_t, conv_w, conv_b, dtb, alog, dskip_e, norm_g, dims, batch, seq, chunk, blk_off, gps):
    g = dims.ssd_groups
    inner_g = dims.ssd_inner // g
    nxb = inner_g // LANE
    hg = dims.ssd_heads // g
    n_state = dims.ssd_state
    nt = seq // chunk
    kconv = conv_w.shape[0]
    z_off, x_off, b_off, c_off = blk_off
    p4 = p.reshape(p.shape[0], batch, seq, LANE)
    sb_off = dims.ssd_inner // n_state
    assert all(o % (gps * nxb) == 0 for o in (z_off, x_off)) and all(o % gps == 0 for o in (b_off, c_off, sb_off, g))
    shift = np.zeros((chunk, kconv * chunk), np.float32)
    for k in range(kconv):
        t = np.arange(kconv - 1 - k, chunk)
        shift[t, chunk * k + t - (kconv - 1) + k] = 1.0
    shift = jnp.asarray(shift, BF16)
    in_specs = [
        pl.BlockSpec((gps * nxb, batch, chunk, LANE), lambda gg, t: (x_off // (gps * nxb) + gg, 0, t, 0)),
        pl.BlockSpec((gps, batch, chunk, LANE), lambda gg, t: (b_off // gps + gg, 0, t, 0)),
        pl.BlockSpec((gps, batch, chunk, LANE), lambda gg, t: (c_off // gps + gg, 0, t, 0)),
        pl.BlockSpec((gps * nxb, batch, chunk, LANE), lambda gg, t: (z_off // (gps * nxb) + gg, 0, t, 0)),
        pl.BlockSpec((batch, gps * hg, chunk), lambda gg, t: (0, gg, t)),
        pl.BlockSpec((kconv, gps * inner_g), lambda gg, t: (0, gg)),
        pl.BlockSpec((kconv, gps * n_state), lambda gg, t: (0, sb_off // gps + gg)),
        pl.BlockSpec((kconv, gps * n_state), lambda gg, t: (0, (sb_off + g) // gps + gg)),
        pl.BlockSpec((1, gps * inner_g), lambda gg, t: (0, gg)),
        pl.BlockSpec((1, gps * n_state), lambda gg, t: (0, sb_off // gps + gg)),
        pl.BlockSpec((1, gps * n_state), lambda gg, t: (0, (sb_off + g) // gps + gg)),
        pl.BlockSpec((gps * hg, LANE), lambda gg, t: (gg, 0)),
        pl.BlockSpec((gps * hg, LANE), lambda gg, t: (gg, 0)),
        pl.BlockSpec((1, gps * inner_g), lambda gg, t: (0, gg)),
        pl.BlockSpec((1, gps * inner_g), lambda gg, t: (0, gg)),
        pl.BlockSpec(shift.shape, lambda gg, t: (0, 0)),
    ]
    y = pl.pallas_call(
        functools.partial(_ssd_kernel, hd=dims.ssd_head_dim, kconv=kconv, gps=gps),
        grid=(g // gps, nt),
        in_specs=in_specs,
        out_specs=pl.BlockSpec((batch, chunk, gps * inner_g), lambda gg, t: (0, t, gg)),
        out_shape=jax.ShapeDtypeStruct((batch, seq, dims.ssd_inner), BF16),
        scratch_shapes=[pltpu.VMEM((gps, batch, 8, inner_g + 2 * n_state), F32),
                        pltpu.VMEM((gps, batch, n_state, inner_g), F32)],
        compiler_params=_params("parallel", "arbitrary"),
        name="ssd",
    )(p4, p4, p4, p4, dt_t, conv_w, conv_w, conv_w, conv_b, conv_b, conv_b, dtb, alog, dskip_e, norm_g, shift)
    return y.reshape(batch * seq, dims.ssd_inner)


def _merge_kernel(o_ref, y_ref, gsb_ref, gssd_ref, wsb_ref, wssd_ref, out_ref):
    a = jnp.dot(o_ref[...], wsb_ref[...], preferred_element_type=F32)
    b = jnp.dot(y_ref[...], wssd_ref[...], preferred_element_type=F32)
    gsb = jnp.concatenate([gsb_ref[c] for c in range(gsb_ref.shape[0])], axis=1).astype(F32)
    gssd = jnp.concatenate([gssd_ref[c] for c in range(gssd_ref.shape[0])], axis=1).astype(F32)
    out_ref[...] = (jax.nn.sigmoid(gsb) * a + jax.nn.sigmoid(gssd) * b).astype(out_ref.dtype)


def _merge(o_sb, y, p, w_sb, w_ssd, gsb_off, gssd_off, tm, tn):
    m, ko = o_sb.shape
    ky = y.shape[1]
    d = w_sb.shape[1]
    nb = tn // LANE
    return pl.pallas_call(
        _merge_kernel,
        grid=(m // tm, d // tn),
        in_specs=[
            pl.BlockSpec((tm, ko), lambda i, j: (i, 0)),
            pl.BlockSpec((tm, ky), lambda i, j: (i, 0)),
            pl.BlockSpec((nb, tm, LANE), lambda i, j: (gsb_off // nb + j, i, 0)),
            pl.BlockSpec((nb, tm, LANE), lambda i, j: (gssd_off // nb + j, i, 0)),
            pl.BlockSpec((ko, tn), lambda i, j: (0, j), pipeline_mode=pl.Buffered(1 if tn == d else 2)),
            pl.BlockSpec((ky, tn), lambda i, j: (0, j), pipeline_mode=pl.Buffered(1 if tn == d else 2)),
        ],
        out_specs=pl.BlockSpec((tm, tn), lambda i, j: (i, j)),
        out_shape=jax.ShapeDtypeStruct((m, d), BF16),
        compiler_params=_params("parallel", "arbitrary"),
        name="merge",
    )(o_sb, y, p, p, w_sb, w_ssd)


def _proj_norm_res_kernel(a_ref, w_ref, g_ref, x_ref, out_ref, f_scr):
    nj, tm, tn = f_scr.shape
    j = pl.program_id(1)
    f_scr[j] = jnp.dot(a_ref[...], w_ref[...], preferred_element_type=F32)

    @pl.when(j == nj - 1)
    def _():
        ss = jnp.zeros((tm, 1), F32)
        for c in range(nj):
            f = f_scr[c]
            ss = ss + jnp.sum(f * f, axis=-1, keepdims=True)
        inv = lax.rsqrt(ss * (1.0 / (nj * tn)) + EPS)
        for c in range(nj):
            cols = slice(c * tn, (c + 1) * tn)
            out_ref[:, cols] = x_ref[:, cols] + f_scr[c] * inv * _rows(g_ref[:, cols], tm)


def _proj_norm_res(a, w, gain, x2, tm, tn, name):
    m, kdim = a.shape
    d = w.shape[1]
    return pl.pallas_call(
        _proj_norm_res_kernel,
        grid=(m // tm, d // tn),
        in_specs=[
            pl.BlockSpec((tm, kdim), lambda i, j: (i, 0)),
            pl.BlockSpec((kdim, tn), lambda i, j: (0, j), pipeline_mode=pl.Buffered(1 if tn == d else 2)),
            pl.BlockSpec((1, d), lambda i, j: (0, 0)),
            pl.BlockSpec((tm, d), lambda i, j: (i, 0)),
        ],
        out_specs=pl.BlockSpec((tm, d), lambda i, j: (i, 0)),
        out_shape=jax.ShapeDtypeStruct((m, d), F32),
        scratch_shapes=[pltpu.VMEM((d // tn, tm, tn), F32)],
        compiler_params=_params("parallel", "arbitrary"),
        name=name,
    )(a, w, gain, x2)


FFN_HALO = 16


def _ffn_up_kernel(x_ref, halo_ref, g_ref, wg_ref, wv_ref, cwg_ref, cwv_ref, cbg_ref, cbv_ref, out_ref, h_scr,
                   *, tiles_per_seq, kconv):
    tm = x_ref.shape[0]

    def norm(v):
        ms = jnp.mean(v * v, axis=-1, keepdims=True)
        return (v * lax.rsqrt(ms + EPS) * g_ref[...]).astype(BF16)

    @pl.when(pl.program_id(1) == 0)
    def _():
        seq_start = (pl.program_id(0) % tiles_per_seq) == 0
        h_scr[:FFN_HALO, :] = jnp.where(seq_start, jnp.zeros((), BF16), norm(halo_ref[...]))
        h_scr[FFN_HALO:, :] = norm(x_ref[...])

    h = h_scr[...]

    def conv(w_ref, cw_ref, cb_ref):
        up = jnp.dot(h, w_ref[...], preferred_element_type=F32)
        u = _rows(cb_ref[...], tm)
        for k in range(kconv):
            off = FFN_HALO - (kconv - 1) + k
            u = u + _rows(cw_ref[k:k + 1, :], tm) * up[off:off + tm, :]
        return u

    gate = conv(wg_ref, cwg_ref, cbg_ref)
    val = conv(wv_ref, cwv_ref, cbv_ref)
    c = math.sqrt(2.0 / math.pi)
    half = 0.5 * gate
    act = half + half * jnp.tanh(gate * (c + (c * 0.044715) * (gate * gate)))
    out_ref[...] = (act * val).astype(out_ref.dtype)


def _ffn_up(x1, gain, w_up, conv_w, conv_b, seq, tm, tn):
    m, d = x1.shape
    dff = w_up.shape[1] // 2
    nj = dff // tn
    kconv = conv_w.shape[0]
    hb = tm // FFN_HALO
    return pl.pallas_call(
        functools.partial(_ffn_up_kernel, tiles_per_seq=seq // tm, kconv=kconv),
        grid=(m // tm, nj),
        in_specs=[
            pl.BlockSpec((tm, d), lambda i, j: (i, 0)),
            pl.BlockSpec((FFN_HALO, d), lambda i, j: (jnp.maximum(i * hb - 1, 0), 0)),
            pl.BlockSpec((1, d), lambda i, j: (0, 0)),
            pl.BlockSpec((d, tn), lambda i, j: (0, j)),
            pl.BlockSpec((d, tn), lambda i, j: (0, nj + j)),
            pl.BlockSpec((kconv, tn), lambda i, j: (0, j)),
            pl.BlockSpec((kconv, tn), lambda i, j: (0, nj + j)),
            pl.BlockSpec((1, tn), lambda i, j: (0, j)),
            pl.BlockSpec((1, tn), lambda i, j: (0, nj + j)),
        ],
        out_specs=pl.BlockSpec((tm, tn), lambda i, j: (i, j)),
        out_shape=jax.ShapeDtypeStruct((m, dff), BF16),
        scratch_shapes=[pltpu.VMEM((FFN_HALO + tm, d), BF16)],
        compiler_params=_params("parallel", "arbitrary"),
        name="ffn_up",
    )(x1, x1, gain, w_up, w_up, conv_w, conv_w, conv_b, conv_b)


def _largest_tile(total, want, quantum=LANE):
    t = min(want, total)
    while total % t or t % quantum:
        t -= quantum
    return t


def _block(x, norm_mix_pre, w_in, ssd_conv_w, ssd_conv_b, dt_bias, a_log, d_skip, ssd_norm, w_sb_proj,
           w_ssd_proj, w_out, norm_mix_post, norm_ffn_pre, w_up, ffn_conv_w, ffn_conv_b, w_down, norm_ffn_post,
           dims, tiles):
    batch, seq, d = x.shape
    m = batch * seq
    sbw, inner, xbc, heads = dims.sb_width, dims.ssd_inner, dims.ssd_xbc, dims.ssd_heads
    x2 = x.reshape(m, d)
    row = lambda v: v.reshape(1, -1)

    dt_lo = 3 * sbw + inner + xbc
    w_in_t = w_in.T
    row_scale = jnp.where(jnp.arange(dt_lo) < sbw, dims.sb_head_dim ** -0.5, 1.0).astype(F32)[:, None]
    w_a = (w_in_t[:dt_lo] * row_scale).astype(BF16)
    w_b = w_in_t[dt_lo + heads:].astype(BF16)
    w_dt = w_in_t[dt_lo:dt_lo + heads]
    z_off = 3 * sbw // LANE
    x_off = z_off + inner // LANE
    b_off = x_off + inner // LANE
    c_off = b_off + dims.ssd_groups * dims.ssd_state // LANE
    gsb_off = c_off + dims.ssd_groups * dims.ssd_state // LANE
    gssd_off = gsb_off + d // LANE

    tm = _largest_tile(seq, tiles["tm"], 16)
    p, dt_t = _in_proj(x2, row(norm_mix_pre), w_a, w_b, w_dt, heads, seq, tm,
                       _largest_tile(math.gcd(w_a.shape[0], w_b.shape[0]), tiles["tn_in"]))

    o_sb, (w_sb16, w_ssd16, w_out16, w_up16, w_down16) = _sb_attn(
        p, dims, batch, seq, min(tiles["tq"], seq), tiles["sb_chains"],
        [w_sb_proj, w_ssd_proj, w_out, w_up, w_down])

    chunk = min(tiles["chunk"], seq)
    bcast = lambda v: jnp.broadcast_to(v.reshape(-1, 1), (heads, LANE))
    y = _ssd(p, dt_t, ssd_conv_w, row(ssd_conv_b), bcast(dt_bias), bcast(a_log),
             row(jnp.repeat(d_skip, dims.ssd_head_dim)), row(ssd_norm), dims, batch, seq, chunk,
             (z_off, x_off, b_off, c_off), tiles["ssd_groups_per_step"])

    merged = _merge(o_sb, y, p, w_sb16, w_ssd16, gsb_off, gssd_off,
                    _largest_tile(m, tiles["tm_merge"], 16), _largest_tile(d, tiles["tn_merge"]))
    tm2 = _largest_tile(m, tiles["tm_res"], 16)
    x1 = _proj_norm_res(merged, w_out16, row(norm_mix_post), x2, tm2, d, "out_proj")

    tm_up = _largest_tile(seq, tiles["tm"], 16)
    act = _ffn_up(x1, row(norm_ffn_pre), w_up16, ffn_conv_w, row(ffn_conv_b), seq, tm_up,
                  _largest_tile(dims.d_ff, tiles["tn_up"]))
    out = _proj_norm_res(act, w_down16, row(norm_ffn_post), x1, tm2,
                         _largest_tile(d, tiles["tn_down"]), "ffn_down")
    return out.reshape(batch, seq, d)


TILES = dict(tm=1024, tn_in=1024, tq=128, sb_chains=8, chunk=128, ssd_groups_per_step=4, tm_merge=512, tn_merge=2048, tm_res=512, tn_up=512, tn_down=2048)


def kernel(x, norm_mix_pre, w_in, ssd_conv_w, ssd_conv_b, dt_bias, a_log, d_skip, ssd_norm, w_sb_proj, w_ssd_proj,
           w_out, norm_mix_post, norm_ffn_pre, w_up, ffn_conv_w, ffn_conv_b, w_down, norm_ffn_post):
    dims = Dims()
    args = (norm_mix_pre, w_in, ssd_conv_w, ssd_conv_b, dt_bias, a_log, d_skip, ssd_norm, w_sb_proj, w_ssd_proj,
            w_out, norm_mix_post, norm_ffn_pre, w_up, ffn_conv_w, ffn_conv_b, w_down, norm_ffn_post)
    for layer in range(w_in.shape[0]):
        x = _block(x, *(a[layer] for a in args), dims, TILES)
    return x
```

```python
import dataclasses
import functools
import math

import numpy as np

import jax
import jax.numpy as jnp
from jax import lax
from jax.experimental import pallas as pl
from jax.experimental.pallas import tpu as pltpu

F32 = jnp.float32
BF16 = jnp.bfloat16
LANE = 128
EPS = 1e-6


@dataclasses.dataclass(frozen=True)
class Dims:
    d_model: int = 2048
    sb_heads: int = 16
    sb_head_dim: int = 128
    ssd_inner: int = 4096
    ssd_head_dim: int = 64
    ssd_groups: int = 8
    ssd_state: int = 128
    ssd_conv: int = 4
    d_ff: int = 5632
    ffn_conv: int = 3

    @property
    def sb_width(self):
        return self.sb_heads * self.sb_head_dim

    @property
    def ssd_heads(self):
        return self.ssd_inner // self.ssd_head_dim

    @property
    def ssd_xbc(self):
        return self.ssd_inner + 2 * self.ssd_groups * self.ssd_state


SB_LOG_WEIGHT_FLOOR = -110.0
SB_EAGER_STEPS = 2
VMEM_LIMIT = 60 * 1024 * 1024


def _softplus(x):
    return jnp.maximum(x, 0.0) + jnp.log1p(jnp.exp(-jnp.abs(x)))


def _rows(v, n):
    return jnp.tile(jnp.broadcast_to(v, (8, v.shape[1])), (n // 8, 1))


def _silu(x):
    h = 0.5 * x
    return h + h * jnp.tanh(h)


def _params(*sem):
    return pltpu.CompilerParams(dimension_semantics=sem, vmem_limit_bytes=VMEM_LIMIT)


def _in_proj_kernel(x_ref, g_ref, w_ref, wdt_ref, p_ref, dt_ref, h_scr):
    j = pl.program_id(1)

    @pl.when(j == 0)
    def _():
        x = x_ref[...]
        ms = jnp.mean(x * x, axis=-1, keepdims=True)
        h = (x * lax.rsqrt(ms + EPS) * g_ref[...]).astype(BF16)
        h_scr[...] = h
        dt_ref[...] = lax.dot_general(wdt_ref[...].astype(BF16), h, (((1,), (1,)), ((), ())),
                                      preferred_element_type=F32)

    acc = lax.dot_general(h_scr[...], w_ref[...], (((1,), (1,)), ((), ())), preferred_element_type=F32)
    for c in range(p_ref.shape[0]):
        p_ref[c] = acc[:, c * LANE:(c + 1) * LANE].astype(BF16)


def _in_proj(x2, gain, w_all, w_dt, hd, seq, tm, tn):
    m, d = x2.shape
    n = w_all.shape[0]
    tps = seq // tm
    return pl.pallas_call(
        _in_proj_kernel,
        grid=(m // tm, n // tn),
        in_specs=[
            pl.BlockSpec((tm, d), lambda i, j: (i, 0)),
            pl.BlockSpec((1, d), lambda i, j: (0, 0)),
            pl.BlockSpec((tn, d), lambda i, j: (j, 0)),
            pl.BlockSpec((hd, d), lambda i, j: (0, 0)),
        ],
        out_specs=[
            pl.BlockSpec((tn // LANE, tm, LANE), lambda i, j: (j, i, 0)),
            pl.BlockSpec((None, hd, tm), lambda i, j: (i // tps, 0, i % tps)),
        ],
        out_shape=[
            jax.ShapeDtypeStruct((n // LANE, m, LANE), BF16),
            jax.ShapeDtypeStruct((m // seq, hd, seq), F32),
        ],
        scratch_shapes=[pltpu.VMEM((tm, d), BF16)],
        compiler_params=_params("parallel", "arbitrary"),
        name="in_proj",
    )(x2, gain, w_all, w_dt)


def _sb_attn_kernel(q_ref, k_ref, v_ref, *rest, tq, nchain, n_cast):
    o_ref = rest[n_cast]
    for src, dst in zip(rest[:n_cast], rest[n_cast + 1:]):
        dst[...] = src[...].astype(dst.dtype)
    s, dh = q_ref.shape
    ngroups = s // (tq * nchain)
    row = lax.broadcasted_iota(jnp.int32, (tq, tq), 0)
    col = lax.broadcasted_iota(jnp.int32, (tq, tq), 1)
    causal = col < row
    suffix = jnp.where(row > col, 1.0, 0.0).astype(BF16)
    sum_rhs = jnp.concatenate([suffix, jnp.ones((tq, tq), BF16)], axis=1)

    def scores(qbs, js):
        return [lax.dot_general(qb, k_ref[pl.ds(pl.multiple_of(j * tq, tq), tq), :], (((1,), (1,)), ((), ())),
                                preferred_element_type=F32) for qb, j in zip(qbs, js)]

    def key_blocks(zs, js, carries, accs, masked):
        starts = [pl.multiple_of(j * tq, tq) for j in js]
        sps, parts = [], []
        for z in zs:
            sp = jnp.maximum(z, 0.0) + jnp.log(1.0 + jnp.exp(-jnp.abs(z)))
            if masked:
                sp = jnp.where(causal, sp, 0.0)
            sps.append(sp)
            parts.append(sp.astype(BF16))
        sums = jnp.dot(jnp.concatenate(parts, axis=0), sum_rhs, preferred_element_type=F32)
        ws = []
        for c, (z, sp) in enumerate(zip(zs, sps)):
            w = jnp.exp(z - sp - sums[c * tq:(c + 1) * tq, :tq] - carries[c])
            if masked:
                w = jnp.where(causal, w, 0.0)
            ws.append(w.astype(BF16))
        new_a = [acc + jnp.dot(w, v_ref[pl.ds(st, tq), :], preferred_element_type=F32)
                 for acc, w, st in zip(accs, ws, starts)]
        new_c = [carries[c] + sums[c * tq:(c + 1) * tq, tq:] for c in range(len(js))]
        return new_c, new_a

    def q_group(gi, _):
        qi = [gi * nchain + c for c in range(nchain)]
        qstart = [pl.multiple_of(i * tq, tq) for i in qi]
        qb = [q_ref[pl.ds(st, tq), :] for st in qstart]

        def retire(t, carries):
            carries = [jnp.where(qi[c] - t >= 0, carries[c], jnp.inf) for c in range(nchain)]
            return carries, jnp.min(functools.reduce(jnp.minimum, carries))

        def back(t):
            return [jnp.maximum(qi[c] - t, 0) for c in range(nchain)]

        zs = [scores(qb, back(t)) for t in range(SB_EAGER_STEPS + 1)]
        carries, accs = key_blocks(zs[0], qi, [jnp.zeros((tq, tq), F32)] * nchain,
                                   [jnp.zeros((tq, dh), F32)] * nchain, True)
        carries, low = retire(1, carries)
        for t in range(1, SB_EAGER_STEPS + 1):
            carries, accs = key_blocks(zs[t], back(t), carries, accs, False)
            carries, low = retire(t + 1, carries)

        def cond(st):
            return st[1] < -SB_LOG_WEIGHT_FLOOR

        def body(st):
            t, _, carries, accs = st
            carries, accs = key_blocks(scores(qb, back(t)), back(t), carries, accs, False)
            carries, low = retire(t + 1, carries)
            return t + 1, low, carries, accs

        _, _, _, accs = lax.while_loop(cond, body, (SB_EAGER_STEPS + 1, low, carries, accs))
        for c in range(nchain):
            o_ref[pl.ds(qstart[c], tq), :] = accs[c].astype(o_ref.dtype)
        return 0

    lax.fori_loop(0, ngroups, q_group, 0)


def _sb_attn(p, dims, batch, seq, tq, nchain, cast_weights):
    h, dh = dims.sb_heads, dims.sb_head_dim
    m = batch * seq
    steps = batch * h
    spec = lambda off: pl.BlockSpec((None, seq, dh), lambda b, hh: (off + hh, b, 0))
    slab = lambda w: pl.BlockSpec((w.shape[0] // steps, w.shape[1]), lambda b, hh: (b * h + hh, 0))
    outs = pl.pallas_call(
        functools.partial(_sb_attn_kernel, tq=tq, nchain=nchain, n_cast=len(cast_weights)),
        grid=(batch, h),
        in_specs=[spec(0), spec(h), spec(2 * h)] + [slab(w) for w in cast_weights],
        out_specs=[pl.BlockSpec((seq, dh), lambda b, hh: (b, hh))] + [slab(w) for w in cast_weights],
        out_shape=[jax.ShapeDtypeStruct((m, h * dh), BF16)]
                  + [jax.ShapeDtypeStruct(w.shape, BF16) for w in cast_weights],
        compiler_params=_params("arbitrary", "arbitrary"),
        name="sb_attn",
    )(p, p, p, *cast_weights)
    return outs[0], outs[1:]


def _ssd_kernel(x_ref, b_ref, c_ref, z_ref, dt_ref, wx_ref, wb_ref, wc_ref, bx_ref, bb_ref, bc_ref,
                dtb_ref, alog_ref, dskip_ref, norm_ref, shift_ref, y_ref, halo_scr, state_scr, *, hd, kconv, gps):
    _, nbatch, l, _ = x_ref.shape
    nxb = x_ref.shape[0] // gps
    hg = dt_ref.shape[1] // gps
    inner = nxb * LANE
    n_state = b_ref.shape[-1]
    width = inner + 2 * n_state
    assert l == LANE and 2 * hd == LANE and n_state == LANE

    @pl.when(pl.program_id(1) == 0)
    def _():
        halo_scr[...] = jnp.zeros_like(halo_scr)
        state_scr[...] = jnp.zeros_like(state_scr)

    rows = functools.partial(_rows, n=l)
    lane = lax.broadcasted_iota(jnp.int32, (l, LANE), 1)
    lane_t = lax.broadcasted_iota(jnp.int32, (hg, l), 1)
    low_half = lane < hd
    tril = lane <= lax.broadcasted_iota(jnp.int32, (l, l), 0)
    pad = jnp.zeros((LANE - hg, l), F32)

    gcols = lambda ref, gi, w: ref[:, gi * w:(gi + 1) * w]
    taps8, taps, bcat, dskip, norm_g, neg_a, dtb = [], [], [], [], [], [], []
    for gi in range(gps):
        wcat = jnp.concatenate([gcols(wx_ref, gi, inner), gcols(wb_ref, gi, n_state), gcols(wc_ref, gi, n_state)],
                               axis=1)
        wcat16 = wcat.astype(BF16)
        taps8.append([jnp.broadcast_to(wcat[k:k + 1, :], (8, width)) for k in range(kconv)])
        taps.append([jnp.tile(jnp.broadcast_to(wcat16[k:k + 1, :], (16, width)), (l // 16, 1))
                     for k in range(kconv)])
        bcat.append(rows(jnp.concatenate([gcols(bx_ref, gi, inner), gcols(bb_ref, gi, n_state),
                                          gcols(bc_ref, gi, n_state)], axis=1)))
        dskip.append(rows(gcols(dskip_ref, gi, inner)))
        norm_g.append(rows(gcols(norm_ref, gi, inner)))
        neg_a.append(-jnp.exp(alog_ref[gi * hg:(gi + 1) * hg, :]))
        dtb.append(dtb_ref[gi * hg:(gi + 1) * hg, :])

    chains = [(gi, b) for gi in range(gps) for b in range(nbatch)]

    acts = []
    for gi, b in chains:
        raw = jnp.concatenate([x_ref[gi * nxb + c, b] for c in range(nxb)] + [b_ref[gi, b], c_ref[gi, b]],
                              axis=1)
        body = jnp.dot(shift_ref[...], jnp.concatenate([raw * taps[gi][k] for k in range(kconv)], axis=0),
                       preferred_element_type=F32)
        head = jnp.concatenate([halo_scr[gi, b], raw[:8].astype(F32)], axis=0)
        halo_scr[gi, b] = raw[l - 8:, :].astype(F32)
        first = jnp.zeros((8, width), F32)
        for k in range(kconv):
            off = 8 - (kconv - 1) + k
            first = first + taps8[gi][k] * head[off:off + 8, :]
        acts.append(_silu(jnp.concatenate([first, body[8:]], axis=0) + bcat[gi]))
    x16 = [a[:, :inner].astype(BF16) for a in acts]
    bms = [a[:, inner:inner + n_state] for a in acts]
    cms = [a[:, inner + n_state:] for a in acts]

    cbs = [lax.dot_general(cm.astype(BF16), bm.astype(BF16), (((1,), (1,)), ((), ())),
                           preferred_element_type=F32) for cm, bm in zip(cms, bms)]
    bm_ts = [bm.T for bm in bms]
    src_rows, coef_rows, keep_rows, cs_cols = [], [], [], []
    for gi, b in chains:
        dt = _softplus(dt_ref[b, gi * hg:(gi + 1) * hg, :] + dtb[gi])
        cs = dt * neg_a[gi]
        sh = 1
        while sh < l:
            cs = cs + jnp.where(lane_t >= sh, pltpu.roll(cs, sh, axis=1), 0.0)
            sh *= 2
        cs_last = jnp.broadcast_to(cs[:, l - 1:l], (hg, l))
        src_rows.append(cs - jnp.log(dt))
        coef_rows.append(dt * jnp.exp(cs_last - cs))
        keep_rows.append(jnp.exp(cs_last))
        cs_cols.append(jnp.concatenate([cs, pad], axis=0).T)

    states = [state_scr[gi, b] for gi, b in chains]
    y_blocks = [[] for _ in chains]
    new_state = [[] for _ in chains]
    for blk in range(nxb):
        for n in range(len(chains)):
            lhs, scaled_bt = [], []
            for h in (2 * blk, 2 * blk + 1):
                ccol = jnp.broadcast_to(cs_cols[n][:, h:h + 1], (l, LANE))
                seg = ccol - rows(src_rows[n][h:h + 1, :])
                w_in = (cbs[n] * jnp.exp(jnp.where(tril, seg, -jnp.inf))).astype(BF16)
                w_prev = (cms[n] * jnp.exp(ccol)).astype(BF16)
                lhs.append(jnp.concatenate([w_in, w_prev], axis=1))
                scaled_bt.append((bm_ts[n] * rows(coef_rows[n][h:h + 1, :])).astype(BF16))
            xblk = x16[n][:, blk * LANE:(blk + 1) * LANE]
            sblk = states[n][:, blk * LANE:(blk + 1) * LANE]
            rhs = jnp.concatenate([xblk, sblk.astype(BF16)], axis=0)
            yy = jnp.dot(jnp.concatenate(lhs, axis=0), rhs, preferred_element_type=F32)
            y_blocks[n].append(jnp.where(low_half, yy[:l], yy[l:]))
            ss = jnp.dot(jnp.concatenate(scaled_bt, axis=0), xblk, preferred_element_type=F32)
            kr = keep_rows[n]
            keep = jnp.where(low_half[:1], kr[2 * blk:2 * blk + 1, :], kr[2 * blk + 1:2 * blk + 2, :])
            new_state[n].append(sblk * rows(keep) + jnp.where(low_half, ss[:n_state], ss[n_state:]))

    for n, (gi, b) in enumerate(chains):
        state_scr[gi, b] = jnp.concatenate(new_state[n], axis=1)
        y = jnp.concatenate(y_blocks[n], axis=1) + acts[n][:, :inner] * dskip[gi]
        zf = jnp.concatenate([z_ref[gi * nxb + c, b] for c in range(nxb)], axis=1).astype(F32)
        y = y * _silu(zf)
        ms = jnp.mean(y * y, axis=-1, keepdims=True)
        y = y * lax.rsqrt(ms + EPS) * norm_g[gi]
        y_ref[b, :, gi * inner:(gi + 1) * inner] = y.astype(y_ref.dtype)


def _ssd(p, dt_t, conv_w, conv_b, dtb, alog, dskip_e, norm_g, dims, batch, seq, chunk, blk_off, gps):
    g = dims.ssd_groups
    inner_g = dims.ssd_inner // g
    nxb = inner_g // LANE
    hg = dims.ssd_heads // g
    n_state = dims.ssd_state
    nt = seq // chunk
    kconv = conv_w.shape[0]
    z_off, x_off, b_off, c_off = blk_off
    p4 = p.reshape(p.shape[0], batch, seq, LANE)
    sb_off = dims.ssd_inner // n_state
    assert all(o % (gps * nxb) == 0 for o in (z_off, x_off)) and all(o % gps == 0 for o in (b_off, c_off, sb_off, g))
    shift = np.zeros((chunk, kconv * chunk), np.float32)
    for k in range(kconv):
        t = np.arange(kconv - 1 - k, chunk)
        shift[t, chunk * k + t - (kconv - 1) + k] = 1.0
    shift = jnp.asarray(shift, BF16)
    in_specs = [
        pl.BlockSpec((gps * nxb, batch, chunk, LANE), lambda gg, t: (x_off // (gps * nxb) + gg, 0, t, 0)),
        pl.BlockSpec((gps, batch, chunk, LANE), lambda gg, t: (b_off // gps + gg, 0, t, 0)),
        pl.BlockSpec((gps, batch, chunk, LANE), lambda gg, t: (c_off // gps + gg, 0, t, 0)),
        pl.BlockSpec((gps * nxb, batch, chunk, LANE), lambda gg, t: (z_off // (gps * nxb) + gg, 0, t, 0)),
        pl.BlockSpec((batch, gps * hg, chunk), lambda gg, t: (0, gg, t)),
        pl.BlockSpec((kconv, gps * inner_g), lambda gg, t: (0, gg)),
        pl.BlockSpec((kconv, gps * n_state), lambda gg, t: (0, sb_off // gps + gg)),
        pl.BlockSpec((kconv, gps * n_state), lambda gg, t: (0, (sb_off + g) // gps + gg)),
        pl.BlockSpec((1, gps * inner_g), lambda gg, t: (0, gg)),
        pl.BlockSpec((1, gps * n_state), lambda gg, t: (0, sb_off // gps + gg)),
        pl.BlockSpec((1, gps * n_state), lambda gg, t: (0, (sb_off + g) // gps + gg)),
        pl.BlockSpec((gps * hg, LANE), lambda gg, t: (gg, 0)),
        pl.BlockSpec((gps * hg, LANE), lambda gg, t: (gg, 0)),
        pl.BlockSpec((1, gps * inner_g), lambda gg, t: (0, gg)),
        pl.BlockSpec((1, gps * inner_g), lambda gg, t: (0, gg)),
        pl.BlockSpec(shift.shape, lambda gg, t: (0, 0)),
    ]
    y = pl.pallas_call(
        functools.partial(_ssd_kernel, hd=dims.ssd_head_dim, kconv=kconv, gps=gps),
        grid=(g // gps, nt),
        in_specs=in_specs,
        out_specs=pl.BlockSpec((batch, chunk, gps * inner_g), lambda gg, t: (0, t, gg)),
        out_shape=jax.ShapeDtypeStruct((batch, seq, dims.ssd_inner), BF16),
        scratch_shapes=[pltpu.VMEM((gps, batch, 8, inner_g + 2 * n_state), F32),
                        pltpu.VMEM((gps, batch, n_state, inner_g), F32)],
        compiler_params=_params("parallel", "arbitrary"),
        name="ssd",
    )(p4, p4, p4, p4, dt_t, conv_w, conv_w, conv_w, conv_b, conv_b, conv_b, dtb, alog, dskip_e, norm_g, shift)
    return y.reshape(batch * seq, dims.ssd_inner)


def _merge_kernel(o_ref, y_ref, gsb_ref, gssd_ref, wsb_ref, wssd_ref, out_ref):
    a = jnp.dot(o_ref[...], wsb_ref[...], preferred_element_type=F32)
    b = jnp.dot(y_ref[...], wssd_ref[...], preferred_element_type=F32)
    gsb = jnp.concatenate([gsb_ref[c] for c in range(gsb_ref.shape[0])], axis=1).astype(F32)
    gssd = jnp.concatenate([gssd_ref[c] for c in range(gssd_ref.shape[0])], axis=1).astype(F32)
    out_ref[...] = (jax.nn.sigmoid(gsb) * a + jax.nn.sigmoid(gssd) * b).astype(out_ref.dtype)


def _merge(o_sb, y, p, w_sb, w_ssd, gsb_off, gssd_off, tm, tn):
    m, ko = o_sb.shape
    ky = y.shape[1]
    d = w_sb.shape[1]
    nb = tn // LANE
    return pl.pallas_call(
        _merge_kernel,
        grid=(m // tm, d // tn),
        in_specs=[
            pl.BlockSpec((tm, ko), lambda i, j: (i, 0)),
            pl.BlockSpec((tm, ky), lambda i, j: (i, 0)),
            pl.BlockSpec((nb, tm, LANE), lambda i, j: (gsb_off // nb + j, i, 0)),
            pl.BlockSpec((nb, tm, LANE), lambda i, j: (gssd_off // nb + j, i, 0)),
            pl.BlockSpec((ko, tn), lambda i, j: (0, j), pipeline_mode=pl.Buffered(1 if tn == d else 2)),
            pl.BlockSpec((ky, tn), lambda i, j: (0, j), pipeline_mode=pl.Buffered(1 if tn == d else 2)),
        ],
        out_specs=pl.BlockSpec((tm, tn), lambda i, j: (i, j)),
        out_shape=jax.ShapeDtypeStruct((m, d), BF16),
        compiler_params=_params("parallel", "arbitrary"),
        name="merge",
    )(o_sb, y, p, p, w_sb, w_ssd)


def _proj_norm_res_kernel(a_ref, w_ref, g_ref, x_ref, out_ref, f_scr):
    nj, tm, tn = f_scr.shape
    j = pl.program_id(1)
    f_scr[j] = jnp.dot(a_ref[...], w_ref[...], preferred_element_type=F32)

    @pl.when(j == nj - 1)
    def _():
        ss = jnp.zeros((tm, 1), F32)
        for c in range(nj):
            f = f_scr[c]
            ss = ss + jnp.sum(f * f, axis=-1, keepdims=True)
        inv = lax.rsqrt(ss * (1.0 / (nj * tn)) + EPS)
        for c in range(nj):
            cols = slice(c * tn, (c + 1) * tn)
            out_ref[:, cols] = x_ref[:, cols] + f_scr[c] * inv * _rows(g_ref[:, cols], tm)


def _proj_norm_res(a, w, gain, x2, tm, tn, name):
    m, kdim = a.shape
    d = w.shape[1]
    return pl.pallas_call(
        _proj_norm_res_kernel,
        grid=(m // tm, d // tn),
        in_specs=[
            pl.BlockSpec((tm, kdim), lambda i, j: (i, 0)),
            pl.BlockSpec((kdim, tn), lambda i, j: (0, j), pipeline_mode=pl.Buffered(1 if tn == d else 2)),
            pl.BlockSpec((1, d), lambda i, j: (0, 0)),
            pl.BlockSpec((tm, d), lambda i, j: (i, 0)),
        ],
        out_specs=pl.BlockSpec((tm, d), lambda i, j: (i, 0)),
        out_shape=jax.ShapeDtypeStruct((m, d), F32),
        scratch_shapes=[pltpu.VMEM((d // tn, tm, tn), F32)],
        compiler_params=_params("parallel", "arbitrary"),
        name=name,
    )(a, w, gain, x2)


FFN_HALO = 16


def _ffn_up_kernel(x_ref, halo_ref, g_ref, wg_ref, wv_ref, cwg_ref, cwv_ref, cbg_ref, cbv_ref, out_ref, h_scr,
                   *, tiles_per_seq, kconv):
    tm = x_ref.shape[0]

    def norm(v):
        ms = jnp.mean(v * v, axis=-1, keepdims=True)
        return (v * lax.rsqrt(ms + EPS) * g_ref[...]).astype(BF16)

    @pl.when(pl.program_id(1) == 0)
    def _():
        seq_start = (pl.program_id(0) % tiles_per_seq) == 0
        h_scr[:FFN_HALO, :] = jnp.where(seq_start, jnp.zeros((), BF16), norm(halo_ref[...]))
        h_scr[FFN_HALO:, :] = norm(x_ref[...])

    h = h_scr[...]

    def conv(w_ref, cw_ref, cb_ref):
        up = jnp.dot(h, w_ref[...], preferred_element_type=F32)
        u = _rows(cb_ref[...], tm)
        for k in range(kconv):
            off = FFN_HALO - (kconv - 1) + k
            u = u + _rows(cw_ref[k:k + 1, :], tm) * up[off:off + tm, :]
        return u

    gate = conv(wg_ref, cwg_ref, cbg_ref)
    val = conv(wv_ref, cwv_ref, cbv_ref)
    c = math.sqrt(2.0 / math.pi)
    half = 0.5 * gate
    act = half + half * jnp.tanh(gate * (c + (c * 0.044715) * (gate * gate)))
    out_ref[...] = (act * val).astype(out_ref.dtype)


def _ffn_up(x1, gain, w_up, conv_w, conv_b, seq, tm, tn):
    m, d = x1.shape
    dff = w_up.shape[1] // 2
    nj = dff // tn
    kconv = conv_w.shape[0]
    hb = tm // FFN_HALO
    return pl.pallas_call(
        functools.partial(_ffn_up_kernel, tiles_per_seq=seq // tm, kconv=kconv),
        grid=(m // tm, nj),
        in_specs=[
            pl.BlockSpec((tm, d), lambda i, j: (i, 0)),
            pl.BlockSpec((FFN_HALO, d), lambda i, j: (jnp.maximum(i * hb - 1, 0), 0)),
            pl.BlockSpec((1, d), lambda i, j: (0, 0)),
            pl.BlockSpec((d, tn), lambda i, j: (0, j)),
            pl.BlockSpec((d, tn), lambda i, j: (0, nj + j)),
            pl.BlockSpec((kconv, tn), lambda i, j: (0, j)),
            pl.BlockSpec((kconv, tn), lambda i, j: (0, nj + j)),
            pl.BlockSpec((1, tn), lambda i, j: (0, j)),
            pl.BlockSpec((1, tn), lambda i, j: (0, nj + j)),
        ],
        out_specs=pl.BlockSpec((tm, tn), lambda i, j: (i, j)),
        out_shape=jax.ShapeDtypeStruct((m, dff), BF16),
        scratch_shapes=[pltpu.VMEM((FFN_HALO + tm, d), BF16)],
        compiler_params=_params("parallel", "arbitrary"),
        name="ffn_up",
    )(x1, x1, gain, w_up, w_up, conv_w, conv_w, conv_b, conv_b)


def _largest_tile(total, want, quantum=LANE):
    t = min(want, total)
    while total % t or t % quantum:
        t -= quantum
    return t


def _block(x, norm_mix_pre, w_in, ssd_conv_w, ssd_conv_b, dt_bias, a_log, d_skip, ssd_norm, w_sb_proj,
           w_ssd_proj, w_out, norm_mix_post, norm_ffn_pre, w_up, ffn_conv_w, ffn_conv_b, w_down, norm_ffn_post,
           dims, tiles):
    batch, seq, d = x.shape
    m = batch * seq
    sbw, inner, xbc, heads = dims.sb_width, dims.ssd_inner, dims.ssd_xbc, dims.ssd_heads
    x2 = x.reshape(m, d)
    row = lambda v: v.reshape(1, -1)

    dt_lo = 3 * sbw + inner + xbc
    w_in_t = w_in.T
    n_all = w_in_t.shape[0] - heads
    row_scale = jnp.where(jnp.arange(n_all) < sbw, dims.sb_head_dim ** -0.5, 1.0).astype(F32)[:, None]
    base = (w_in_t[:n_all] * row_scale).astype(BF16)
    w_all = lax.dynamic_update_slice(base, w_in_t[dt_lo + heads:].astype(BF16), (dt_lo, 0))
    w_dt = w_in_t[dt_lo:dt_lo + heads]
    z_off = 3 * sbw // LANE
    x_off = z_off + inner // LANE
    b_off = x_off + inner // LANE
    c_off = b_off + dims.ssd_groups * dims.ssd_state // LANE
    gsb_off = c_off + dims.ssd_groups * dims.ssd_state // LANE
    gssd_off = gsb_off + d // LANE

    tm = _largest_tile(seq, tiles["tm"], 16)
    p, dt_t = _in_proj(x2, row(norm_mix_pre), w_all, w_dt, heads, seq, tm,
                       _largest_tile(n_all, tiles["tn_in"]))

    o_sb, (w_sb16, w_ssd16, w_out16, w_up16, w_down16) = _sb_attn(
        p, dims, batch, seq, min(tiles["tq"], seq), tiles["sb_chains"],
        [w_sb_proj, w_ssd_proj, w_out, w_up, w_down])

    chunk = min(tiles["chunk"], seq)
    bcast = lambda v: jnp.broadcast_to(v.reshape(-1, 1), (heads, LANE))
    y = _ssd(p, dt_t, ssd_conv_w, row(ssd_conv_b), bcast(dt_bias), bcast(a_log),
             row(jnp.repeat(d_skip, dims.ssd_head_dim)), row(ssd_norm), dims, batch, seq, chunk,
             (z_off, x_off, b_off, c_off), tiles["ssd_groups_per_step"])

    merged = _merge(o_sb, y, p, w_sb16, w_ssd16, gsb_off, gssd_off,
                    _largest_tile(m, tiles["tm_merge"], 16), _largest_tile(d, tiles["tn_merge"]))
    tm2 = _largest_tile(m, tiles["tm_res"], 16)
    x1 = _proj_norm_res(merged, w_out16, row(norm_mix_post), x2, tm2, d, "out_proj")

    tm_up = _largest_tile(seq, tiles["tm"], 16)
    act = _ffn_up(x1, row(norm_ffn_pre), w_up16, ffn_conv_w, row(ffn_conv_b), seq, tm_up,
                  _largest_tile(dims.d_ff, tiles["tn_up"]))
    out = _proj_norm_res(act, w_down16, row(norm_ffn_post), x1, tm2,
                         _largest_tile(d, tiles["tn_down"]), "ffn_down")
    return out.reshape(batch, seq, d)


TILES = dict(tm=1024, tn_in=2048, tq=128, sb_chains=8, chunk=128, ssd_groups_per_step=4, tm_merge=512, tn_merge=2048, tm_res=512, tn_up=512, tn_down=2048)


def kernel(x, norm_mix_pre, w_in, ssd_conv_w, ssd_conv_b, dt_bias, a_log, d_skip, ssd_norm, w_sb_proj, w_ssd_proj,
           w_out, norm_mix_post, norm_ffn_pre, w_up, ffn_conv_w, ffn_conv_b, w_down, norm_ffn_post):
    dims = Dims()
    args = (norm_mix_pre, w_in, ssd_conv_w, ssd_conv_b, dt_bias, a_log, d_skip, ssd_norm, w_sb_proj, w_ssd_proj,
            w_out, norm_mix_post, norm_ffn_pre, w_up, ffn_conv_w, ffn_conv_b, w_down, norm_ffn_post)
    for layer in range(w_in.shape[0]):
        x = _block(x, *(a[layer] for a in args), dims, TILES)
    return x
```

```python
import dataclasses
import functools
import math

import numpy as np

import jax
import jax.numpy as jnp
from jax import lax
from jax.experimental import pallas as pl
from jax.experimental.pallas import tpu as pltpu

F32 = jnp.float32
BF16 = jnp.bfloat16
LANE = 128
EPS = 1e-6


@dataclasses.dataclass(frozen=True)
class Dims:
    d_model: int = 2048
    sb_heads: int = 16
    sb_head_dim: int = 128
    ssd_inner: int = 4096
    ssd_head_dim: int = 64
    ssd_groups: int = 8
    ssd_state: int = 128
    ssd_conv: int = 4
    d_ff: int = 5632
    ffn_conv: int = 3

    @property
    def sb_width(self):
        return self.sb_heads * self.sb_head_dim

    @property
    def ssd_heads(self):
        return self.ssd_inner // self.ssd_head_dim

    @property
    def ssd_xbc(self):
        return self.ssd_inner + 2 * self.ssd_groups * self.ssd_state


SB_LOG_WEIGHT_FLOOR = -110.0
SB_EAGER_STEPS = 2
VMEM_LIMIT = 60 * 1024 * 1024


def _softplus(x):
    return jnp.maximum(x, 0.0) + jnp.log1p(jnp.exp(-jnp.abs(x)))


def _rows(v, n):
    return jnp.tile(jnp.broadcast_to(v, (8, v.shape[1])), (n // 8, 1))


def _silu(x):
    h = 0.5 * x
    return h + h * jnp.tanh(h)


def _params(*sem):
    return pltpu.CompilerParams(dimension_semantics=sem, vmem_limit_bytes=VMEM_LIMIT)


def _in_proj_kernel(x_ref, g_ref, w_ref, wdt_ref, p_ref, dt_ref, h_scr):
    j = pl.program_id(1)

    @pl.when(j == 0)
    def _():
        x = x_ref[...]
        ms = jnp.mean(x * x, axis=-1, keepdims=True)
        h = (x * lax.rsqrt(ms + EPS) * g_ref[...]).astype(BF16)
        h_scr[...] = h
        dt_ref[...] = lax.dot_general(wdt_ref[...].astype(BF16), h, (((1,), (1,)), ((), ())),
                                      preferred_element_type=F32)

    acc = lax.dot_general(h_scr[...], w_ref[...], (((1,), (1,)), ((), ())), preferred_element_type=F32)
    for c in range(p_ref.shape[0]):
        p_ref[c] = acc[:, c * LANE:(c + 1) * LANE].astype(BF16)


def _in_proj(x2, gain, w_all, w_dt, hd, seq, tm, tn):
    m, d = x2.shape
    n = w_all.shape[0]
    tps = seq // tm
    return pl.pallas_call(
        _in_proj_kernel,
        grid=(m // tm, n // tn),
        in_specs=[
            pl.BlockSpec((tm, d), lambda i, j: (i, 0)),
            pl.BlockSpec((1, d), lambda i, j: (0, 0)),
            pl.BlockSpec((tn, d), lambda i, j: (j, 0)),
            pl.BlockSpec((hd, d), lambda i, j: (0, 0)),
        ],
        out_specs=[
            pl.BlockSpec((tn // LANE, tm, LANE), lambda i, j: (j, i, 0)),
            pl.BlockSpec((None, hd, tm), lambda i, j: (i // tps, 0, i % tps)),
        ],
        out_shape=[
            jax.ShapeDtypeStruct((n // LANE, m, LANE), BF16),
            jax.ShapeDtypeStruct((m // seq, hd, seq), F32),
        ],
        scratch_shapes=[pltpu.VMEM((tm, d), BF16)],
        compiler_params=_params("parallel", "arbitrary"),
        name="in_proj",
    )(x2, gain, w_all, w_dt)


def _sb_attn_kernel(q_ref, k_ref, v_ref, *rest, tq, nchain, n_cast):
    o_ref = rest[n_cast]
    for src, dst in zip(rest[:n_cast], rest[n_cast + 1:]):
        dst[...] = src[...].astype(dst.dtype)
    s, dh = q_ref.shape
    ngroups = s // (tq * nchain)
    row = lax.broadcasted_iota(jnp.int32, (tq, tq), 0)
    col = lax.broadcasted_iota(jnp.int32, (tq, tq), 1)
    causal = col < row
    suffix = jnp.where(row > col, 1.0, 0.0).astype(BF16)
    sum_rhs = jnp.concatenate([suffix, jnp.ones((tq, tq), BF16)], axis=1)

    def scores(qbs, js):
        return [lax.dot_general(qb, k_ref[pl.ds(pl.multiple_of(j * tq, tq), tq), :], (((1,), (1,)), ((), ())),
                                preferred_element_type=F32) for qb, j in zip(qbs, js)]

    def key_blocks(zs, js, carries, accs, masked):
        starts = [pl.multiple_of(j * tq, tq) for j in js]
        sps, parts = [], []
        for z in zs:
            sp = jnp.maximum(z, 0.0) + jnp.log(1.0 + jnp.exp(-jnp.abs(z)))
            if masked:
                sp = jnp.where(causal, sp, 0.0)
            sps.append(sp)
            parts.append(sp.astype(BF16))
        sums = jnp.dot(jnp.concatenate(parts, axis=0), sum_rhs, preferred_element_type=F32)
        ws = []
        for c, (z, sp) in enumerate(zip(zs, sps)):
            w = jnp.exp(z - sp - sums[c * tq:(c + 1) * tq, :tq] - carries[c])
            if masked:
                w = jnp.where(causal, w, 0.0)
            ws.append(w.astype(BF16))
        new_a = [acc + jnp.dot(w, v_ref[pl.ds(st, tq), :], preferred_element_type=F32)
                 for acc, w, st in zip(accs, ws, starts)]
        new_c = [carries[c] + sums[c * tq:(c + 1) * tq, tq:] for c in range(len(js))]
        return new_c, new_a

    def q_group(gi, _):
        qi = [gi * nchain + c for c in range(nchain)]
        qstart = [pl.multiple_of(i * tq, tq) for i in qi]
        qb = [q_ref[pl.ds(st, tq), :] for st in qstart]

        def retire(t, carries):
            carries = [jnp.where(qi[c] - t >= 0, carries[c], jnp.inf) for c in range(nchain)]
            return carries, jnp.min(functools.reduce(jnp.minimum, carries))

        def back(t):
            return [jnp.maximum(qi[c] - t, 0) for c in range(nchain)]

        zs = [scores(qb, back(t)) for t in range(SB_EAGER_STEPS + 1)]
        carries, accs = key_blocks(zs[0], qi, [jnp.zeros((tq, tq), F32)] * nchain,
                                   [jnp.zeros((tq, dh), F32)] * nchain, True)
        carries, low = retire(1, carries)
        for t in range(1, SB_EAGER_STEPS + 1):
            carries, accs = key_blocks(zs[t], back(t), carries, accs, False)
            carries, low = retire(t + 1, carries)

        def cond(st):
            return st[1] < -SB_LOG_WEIGHT_FLOOR

        def body(st):
            t, _, carries, accs = st
            carries, accs = key_blocks(scores(qb, back(t)), back(t), carries, accs, False)
            carries, low = retire(t + 1, carries)
            return t + 1, low, carries, accs

        _, _, _, accs = lax.while_loop(cond, body, (SB_EAGER_STEPS + 1, low, carries, accs))
        for c in range(nchain):
            o_ref[pl.ds(qstart[c], tq), :] = accs[c].astype(o_ref.dtype)
        return 0

    lax.fori_loop(0, ngroups, q_group, 0)


def _sb_attn(p, dims, batch, seq, tq, nchain, cast_weights):
    h, dh = dims.sb_heads, dims.sb_head_dim
    m = batch * seq
    steps = batch * h
    spec = lambda off: pl.BlockSpec((None, seq, dh), lambda b, hh: (off + hh, b, 0))
    slab = lambda w: pl.BlockSpec((w.shape[0] // steps, w.shape[1]), lambda b, hh: (b * h + hh, 0))
    outs = pl.pallas_call(
        functools.partial(_sb_attn_kernel, tq=tq, nchain=nchain, n_cast=len(cast_weights)),
        grid=(batch, h),
        in_specs=[spec(0), spec(h), spec(2 * h)] + [slab(w) for w in cast_weights],
        out_specs=[pl.BlockSpec((seq, dh), lambda b, hh: (b, hh))] + [slab(w) for w in cast_weights],
        out_shape=[jax.ShapeDtypeStruct((m, h * dh), BF16)]
                  + [jax.ShapeDtypeStruct(w.shape, BF16) for w in cast_weights],
        compiler_params=_params("arbitrary", "arbitrary"),
        name="sb_attn",
    )(p, p, p, *cast_weights)
    return outs[0], outs[1:]


def _ssd_kernel(x_ref, b_ref, c_ref, z_ref, dt_ref, wx_ref, wb_ref, wc_ref, bx_ref, bb_ref, bc_ref,
                dtb_ref, alog_ref, dskip_ref, norm_ref, shift_ref, y_ref, halo_scr, state_scr, *, hd, kconv, gps):
    nbatch, l = x_ref.shape[1], LANE
    nxb = x_ref.shape[0] // gps
    hg = dt_ref.shape[1] // gps
    inner = nxb * LANE
    n_state = b_ref.shape[-1]
    width = inner + 2 * n_state
    assert l == LANE and 2 * hd == LANE and n_state == LANE

    @pl.when(pl.program_id(1) == 0)
    def _():
        halo_scr[...] = jnp.zeros_like(halo_scr)
        state_scr[...] = jnp.zeros_like(state_scr)

    rows = functools.partial(_rows, n=l)
    lane = lax.broadcasted_iota(jnp.int32, (l, LANE), 1)
    lane_t = lax.broadcasted_iota(jnp.int32, (hg, l), 1)
    low_half = lane < hd
    tril = lane <= lax.broadcasted_iota(jnp.int32, (l, l), 0)
    pad = jnp.zeros((LANE - hg, l), F32)

    gcols = lambda ref, gi, w: ref[:, gi * w:(gi + 1) * w]
    taps8, taps, bcat, dskip, norm_g, neg_a, dtb = [], [], [], [], [], [], []
    for gi in range(gps):
        wcat = jnp.concatenate([gcols(wx_ref, gi, inner), gcols(wb_ref, gi, n_state), gcols(wc_ref, gi, n_state)],
                               axis=1)
        wcat16 = wcat.astype(BF16)
        taps8.append([jnp.broadcast_to(wcat[k:k + 1, :], (8, width)) for k in range(kconv)])
        taps.append([jnp.tile(jnp.broadcast_to(wcat16[k:k + 1, :], (16, width)), (l // 16, 1))
                     for k in range(kconv)])
        bcat.append(rows(jnp.concatenate([gcols(bx_ref, gi, inner), gcols(bb_ref, gi, n_state),
                                          gcols(bc_ref, gi, n_state)], axis=1)))
        dskip.append(rows(gcols(dskip_ref, gi, inner)))
        norm_g.append(rows(gcols(norm_ref, gi, inner)))
        neg_a.append(-jnp.exp(alog_ref[gi * hg:(gi + 1) * hg, :]))
        dtb.append(dtb_ref[gi * hg:(gi + 1) * hg, :])

    env = (x_ref, b_ref, c_ref, z_ref, dt_ref, shift_ref, y_ref, halo_scr, state_scr, kconv, gps, nbatch, l, nxb, hg,
           inner, n_state, width, rows, lane_t, low_half, tril, pad, taps8, taps, bcat, dskip, norm_g, neg_a, dtb)
    lax.fori_loop(0, x_ref.shape[2] // l, lambda ci, carry: _ssd_chunk(ci, env), 0)


def _ssd_chunk(ci, env):
    (x_ref, b_ref, c_ref, z_ref, dt_ref, shift_ref, y_ref, halo_scr, state_scr, kconv, gps, nbatch, l, nxb, hg,
     inner, n_state, width, rows, lane_t, low_half, tril, pad, taps8, taps, bcat, dskip, norm_g, neg_a, dtb) = env
    rows_t = pl.ds(pl.multiple_of(ci * l, l), l)

    chains = [(gi, b) for gi in range(gps) for b in range(nbatch)]

    acts = []
    for gi, b in chains:
        raw = jnp.concatenate([x_ref[gi * nxb + c, b, rows_t, :] for c in range(nxb)]
                              + [b_ref[gi, b, rows_t, :], c_ref[gi, b, rows_t, :]], axis=1)
        body = jnp.dot(shift_ref[...], jnp.concatenate([raw * taps[gi][k] for k in range(kconv)], axis=0),
                       preferred_element_type=F32)
        head = jnp.concatenate([halo_scr[gi, b], raw[:8].astype(F32)], axis=0)
        halo_scr[gi, b] = raw[l - 8:, :].astype(F32)
        first = jnp.zeros((8, width), F32)
        for k in range(kconv):
            off = 8 - (kconv - 1) + k
            first = first + taps8[gi][k] * head[off:off + 8, :]
        acts.append(_silu(jnp.concatenate([first, body[8:]], axis=0) + bcat[gi]))
    x16 = [a[:, :inner].astype(BF16) for a in acts]
    bms = [a[:, inner:inner + n_state] for a in acts]
    cms = [a[:, inner + n_state:] for a in acts]

    cbs = [lax.dot_general(cm.astype(BF16), bm.astype(BF16), (((1,), (1,)), ((), ())),
                           preferred_element_type=F32) for cm, bm in zip(cms, bms)]
    bm_ts = [bm.T for bm in bms]
    src_rows, coef_rows, keep_rows, cs_cols = [], [], [], []
    for gi, b in chains:
        dt = _softplus(dt_ref[b, gi * hg:(gi + 1) * hg, rows_t] + dtb[gi])
        cs = dt * neg_a[gi]
        sh = 1
        while sh < l:
            cs = cs + jnp.where(lane_t >= sh, pltpu.roll(cs, sh, axis=1), 0.0)
            sh *= 2
        cs_last = jnp.broadcast_to(cs[:, l - 1:l], (hg, l))
        src_rows.append(cs - jnp.log(dt))
        coef_rows.append(dt * jnp.exp(cs_last - cs))
        keep_rows.append(jnp.exp(cs_last))
        cs_cols.append(jnp.concatenate([cs, pad], axis=0).T)

    states = [state_scr[gi, b] for gi, b in chains]
    y_blocks = [[] for _ in chains]
    new_state = [[] for _ in chains]
    for blk in range(nxb):
        for n in range(len(chains)):
            lhs, scaled_bt = [], []
            for h in (2 * blk, 2 * blk + 1):
                ccol = jnp.broadcast_to(cs_cols[n][:, h:h + 1], (l, LANE))
                seg = ccol - rows(src_rows[n][h:h + 1, :])
                w_in = (cbs[n] * jnp.exp(jnp.where(tril, seg, -jnp.inf))).astype(BF16)
                w_prev = (cms[n] * jnp.exp(ccol)).astype(BF16)
                lhs.append(jnp.concatenate([w_in, w_prev], axis=1))
                scaled_bt.append((bm_ts[n] * rows(coef_rows[n][h:h + 1, :])).astype(BF16))
            xblk = x16[n][:, blk * LANE:(blk + 1) * LANE]
            sblk = states[n][:, blk * LANE:(blk + 1) * LANE]
            rhs = jnp.concatenate([xblk, sblk.astype(BF16)], axis=0)
            yy = jnp.dot(jnp.concatenate(lhs, axis=0), rhs, preferred_element_type=F32)
            y_blocks[n].append(jnp.where(low_half, yy[:l], yy[l:]))
            ss = jnp.dot(jnp.concatenate(scaled_bt, axis=0), xblk, preferred_element_type=F32)
            kr = keep_rows[n]
            keep = jnp.where(low_half[:1], kr[2 * blk:2 * blk + 1, :], kr[2 * blk + 1:2 * blk + 2, :])
            new_state[n].append(sblk * rows(keep) + jnp.where(low_half, ss[:n_state], ss[n_state:]))

    for n, (gi, b) in enumerate(chains):
        state_scr[gi, b] = jnp.concatenate(new_state[n], axis=1)
        y = jnp.concatenate(y_blocks[n], axis=1) + acts[n][:, :inner] * dskip[gi]
        zf = jnp.concatenate([z_ref[gi * nxb + c, b, rows_t, :] for c in range(nxb)], axis=1).astype(F32)
        y = y * _silu(zf)
        ms = jnp.mean(y * y, axis=-1, keepdims=True)
        y = y * lax.rsqrt(ms + EPS) * norm_g[gi]
        y_ref[b, rows_t, gi * inner:(gi + 1) * inner] = y.astype(y_ref.dtype)
    return 0


def _ssd(p, dt_t, conv_w, conv_b, dtb, alog, dskip_e, norm_g, dims, batch, seq, chunk, blk_off, gps, cps):
    g = dims.ssd_groups
    inner_g = dims.ssd_inner // g
    nxb = inner_g // LANE
    hg = dims.ssd_heads // g
    n_state = dims.ssd_state
    span = cps * chunk
    nt = seq // span
    kconv = conv_w.shape[0]
    z_off, x_off, b_off, c_off = blk_off
    p4 = p.reshape(p.shape[0], batch, seq, LANE)
    sb_off = dims.ssd_inner // n_state
    assert all(o % (gps * nxb) == 0 for o in (z_off, x_off)) and all(o % gps == 0 for o in (b_off, c_off, sb_off, g))
    shift = np.zeros((chunk, kconv * chunk), np.float32)
    for k in range(kconv):
        t = np.arange(kconv - 1 - k, chunk)
        shift[t, chunk * k + t - (kconv - 1) + k] = 1.0
    shift = jnp.asarray(shift, BF16)
    in_specs = [
        pl.BlockSpec((gps * nxb, batch, span, LANE), lambda gg, t: (x_off // (gps * nxb) + gg, 0, t, 0)),
        pl.BlockSpec((gps, batch, span, LANE), lambda gg, t: (b_off // gps + gg, 0, t, 0)),
        pl.BlockSpec((gps, batch, span, LANE), lambda gg, t: (c_off // gps + gg, 0, t, 0)),
        pl.BlockSpec((gps * nxb, batch, span, LANE), lambda gg, t: (z_off // (gps * nxb) + gg, 0, t, 0)),
        pl.BlockSpec((batch, gps * hg, span), lambda gg, t: (0, gg, t)),
        pl.BlockSpec((kconv, gps * inner_g), lambda gg, t: (0, gg)),
        pl.BlockSpec((kconv, gps * n_state), lambda gg, t: (0, sb_off // gps + gg)),
        pl.BlockSpec((kconv, gps * n_state), lambda gg, t: (0, (sb_off + g) // gps + gg)),
        pl.BlockSpec((1, gps * inner_g), lambda gg, t: (0, gg)),
        pl.BlockSpec((1, gps * n_state), lambda gg, t: (0, sb_off // gps + gg)),
        pl.BlockSpec((1, gps * n_state), lambda gg, t: (0, (sb_off + g) // gps + gg)),
        pl.BlockSpec((gps * hg, LANE), lambda gg, t: (gg, 0)),
        pl.BlockSpec((gps * hg, LANE), lambda gg, t: (gg, 0)),
        pl.BlockSpec((1, gps * inner_g), lambda gg, t: (0, gg)),
        pl.BlockSpec((1, gps * inner_g), lambda gg, t: (0, gg)),
        pl.BlockSpec(shift.shape, lambda gg, t: (0, 0)),
    ]
    y = pl.pallas_call(
        functools.partial(_ssd_kernel, hd=dims.ssd_head_dim, kconv=kconv, gps=gps),
        grid=(g // gps, nt),
        in_specs=in_specs,
        out_specs=pl.BlockSpec((batch, span, gps * inner_g), lambda gg, t: (0, t, gg)),
        out_shape=jax.ShapeDtypeStruct((batch, seq, dims.ssd_inner), BF16),
        scratch_shapes=[pltpu.VMEM((gps, batch, 8, inner_g + 2 * n_state), F32),
                        pltpu.VMEM((gps, batch, n_state, inner_g), F32)],
        compiler_params=_params("parallel", "arbitrary"),
        name="ssd",
    )(p4, p4, p4, p4, dt_t, conv_w, conv_w, conv_w, conv_b, conv_b, conv_b, dtb, alog, dskip_e, norm_g, shift)
    return y.reshape(batch * seq, dims.ssd_inner)


def _merge_kernel(o_ref, y_ref, gsb_ref, gssd_ref, wsb_ref, wssd_ref, out_ref):
    a = jnp.dot(o_ref[...], wsb_ref[...], preferred_element_type=F32)
    b = jnp.dot(y_ref[...], wssd_ref[...], preferred_element_type=F32)
    gsb = jnp.concatenate([gsb_ref[c] for c in range(gsb_ref.shape[0])], axis=1).astype(F32)
    gssd = jnp.concatenate([gssd_ref[c] for c in range(gssd_ref.shape[0])], axis=1).astype(F32)
    out_ref[...] = (jax.nn.sigmoid(gsb) * a + jax.nn.sigmoid(gssd) * b).astype(out_ref.dtype)


def _merge(o_sb, y, p, w_sb, w_ssd, gsb_off, gssd_off, tm, tn):
    m, ko = o_sb.shape
    ky = y.shape[1]
    d = w_sb.shape[1]
    nb = tn // LANE
    return pl.pallas_call(
        _merge_kernel,
        grid=(m // tm, d // tn),
        in_specs=[
            pl.BlockSpec((tm, ko), lambda i, j: (i, 0)),
            pl.BlockSpec((tm, ky), lambda i, j: (i, 0)),
            pl.BlockSpec((nb, tm, LANE), lambda i, j: (gsb_off // nb + j, i, 0)),
            pl.BlockSpec((nb, tm, LANE), lambda i, j: (gssd_off // nb + j, i, 0)),
            pl.BlockSpec((ko, tn), lambda i, j: (0, j), pipeline_mode=pl.Buffered(1 if tn == d else 2)),
            pl.BlockSpec((ky, tn), lambda i, j: (0, j), pipeline_mode=pl.Buffered(1 if tn == d else 2)),
        ],
        out_specs=pl.BlockSpec((tm, tn), lambda i, j: (i, j)),
        out_shape=jax.ShapeDtypeStruct((m, d), BF16),
        compiler_params=_params("parallel", "arbitrary"),
        name="merge",
    )(o_sb, y, p, p, w_sb, w_ssd)


def _proj_norm_res_kernel(a_ref, w_ref, g_ref, x_ref, out_ref, f_scr):
    nj, tm, tn = f_scr.shape
    j = pl.program_id(1)
    f_scr[j] = jnp.dot(a_ref[...], w_ref[...], preferred_element_type=F32)

    @pl.when(j == nj - 1)
    def _():
        ss = jnp.zeros((tm, 1), F32)
        for c in range(nj):
            f = f_scr[c]
            ss = ss + jnp.sum(f * f, axis=-1, keepdims=True)
        inv = lax.rsqrt(ss * (1.0 / (nj * tn)) + EPS)
        for c in range(nj):
            cols = slice(c * tn, (c + 1) * tn)
            out_ref[:, cols] = x_ref[:, cols] + f_scr[c] * inv * _rows(g_ref[:, cols], tm)


def _proj_norm_res(a, w, gain, x2, tm, tn, name):
    m, kdim = a.shape
    d = w.shape[1]
    return pl.pallas_call(
        _proj_norm_res_kernel,
        grid=(m // tm, d // tn),
        in_specs=[
            pl.BlockSpec((tm, kdim), lambda i, j: (i, 0)),
            pl.BlockSpec((kdim, tn), lambda i, j: (0, j), pipeline_mode=pl.Buffered(1 if tn == d else 2)),
            pl.BlockSpec((1, d), lambda i, j: (0, 0)),
            pl.BlockSpec((tm, d), lambda i, j: (i, 0)),
        ],
        out_specs=pl.BlockSpec((tm, d), lambda i, j: (i, 0)),
        out_shape=jax.ShapeDtypeStruct((m, d), F32),
        scratch_shapes=[pltpu.VMEM((d // tn, tm, tn), F32)],
        compiler_params=_params("parallel", "arbitrary"),
        name=name,
    )(a, w, gain, x2)


FFN_HALO = 16


def _ffn_up_kernel(x_ref, halo_ref, g_ref, wg_ref, wv_ref, cwg_ref, cwv_ref, cbg_ref, cbv_ref, out_ref, h_scr,
                   *, tiles_per_seq, kconv):
    tm = x_ref.shape[0]

    def norm(v):
        ms = jnp.mean(v * v, axis=-1, keepdims=True)
        return (v * lax.rsqrt(ms + EPS) * g_ref[...]).astype(BF16)

    @pl.when(pl.program_id(1) == 0)
    def _():
        seq_start = (pl.program_id(0) % tiles_per_seq) == 0
        h_scr[:FFN_HALO, :] = jnp.where(seq_start, jnp.zeros((), BF16), norm(halo_ref[...]))
        h_scr[FFN_HALO:, :] = norm(x_ref[...])

    h = h_scr[...]

    def conv(w_ref, cw_ref, cb_ref):
        up = jnp.dot(h, w_ref[...], preferred_element_type=F32)
        u = _rows(cb_ref[...], tm)
        for k in range(kconv):
            off = FFN_HALO - (kconv - 1) + k
            u = u + _rows(cw_ref[k:k + 1, :], tm) * up[off:off + tm, :]
        return u

    gate = conv(wg_ref, cwg_ref, cbg_ref)
    val = conv(wv_ref, cwv_ref, cbv_ref)
    c = math.sqrt(2.0 / math.pi)
    half = 0.5 * gate
    act = half + half * jnp.tanh(gate * (c + (c * 0.044715) * (gate * gate)))
    out_ref[...] = (act * val).astype(out_ref.dtype)


def _ffn_up(x1, gain, w_up, conv_w, conv_b, seq, tm, tn):
    m, d = x1.shape
    dff = w_up.shape[1] // 2
    nj = dff // tn
    kconv = conv_w.shape[0]
    hb = tm // FFN_HALO
    return pl.pallas_call(
        functools.partial(_ffn_up_kernel, tiles_per_seq=seq // tm, kconv=kconv),
        grid=(m // tm, nj),
        in_specs=[
            pl.BlockSpec((tm, d), lambda i, j: (i, 0)),
            pl.BlockSpec((FFN_HALO, d), lambda i, j: (jnp.maximum(i * hb - 1, 0), 0)),
            pl.BlockSpec((1, d), lambda i, j: (0, 0)),
            pl.BlockSpec((d, tn), lambda i, j: (0, j)),
            pl.BlockSpec((d, tn), lambda i, j: (0, nj + j)),
            pl.BlockSpec((kconv, tn), lambda i, j: (0, j)),
            pl.BlockSpec((kconv, tn), lambda i, j: (0, nj + j)),
            pl.BlockSpec((1, tn), lambda i, j: (0, j)),
            pl.BlockSpec((1, tn), lambda i, j: (0, nj + j)),
        ],
        out_specs=pl.BlockSpec((tm, tn), lambda i, j: (i, j)),
        out_shape=jax.ShapeDtypeStruct((m, dff), BF16),
        scratch_shapes=[pltpu.VMEM((FFN_HALO + tm, d), BF16)],
        compiler_params=_params("parallel", "arbitrary"),
        name="ffn_up",
    )(x1, x1, gain, w_up, w_up, conv_w, conv_w, conv_b, conv_b)


def _largest_tile(total, want, quantum=LANE):
    t = min(want, total)
    while total % t or t % quantum:
        t -= quantum
    return t


def _block(x, norm_mix_pre, w_in, ssd_conv_w, ssd_conv_b, dt_bias, a_log, d_skip, ssd_norm, w_sb_proj,
           w_ssd_proj, w_out, norm_mix_post, norm_ffn_pre, w_up, ffn_conv_w, ffn_conv_b, w_down, norm_ffn_post,
           dims, tiles):
    batch, seq, d = x.shape
    m = batch * seq
    sbw, inner, xbc, heads = dims.sb_width, dims.ssd_inner, dims.ssd_xbc, dims.ssd_heads
    x2 = x.reshape(m, d)
    row = lambda v: v.reshape(1, -1)

    dt_lo = 3 * sbw + inner + xbc
    w_in_t = w_in.T
    n_all = w_in_t.shape[0] - heads
    row_scale = jnp.where(jnp.arange(n_all) < sbw, dims.sb_head_dim ** -0.5, 1.0).astype(F32)[:, None]
    base = (w_in_t[:n_all] * row_scale).astype(BF16)
    w_all = lax.dynamic_update_slice(base, w_in_t[dt_lo + heads:].astype(BF16), (dt_lo, 0))
    w_dt = w_in_t[dt_lo:dt_lo + heads]
    z_off = 3 * sbw // LANE
    x_off = z_off + inner // LANE
    b_off = x_off + inner // LANE
    c_off = b_off + dims.ssd_groups * dims.ssd_state // LANE
    gsb_off = c_off + dims.ssd_groups * dims.ssd_state // LANE
    gssd_off = gsb_off + d // LANE

    tm = _largest_tile(seq, tiles["tm"], 16)
    p, dt_t = _in_proj(x2, row(norm_mix_pre), w_all, w_dt, heads, seq, tm,
                       _largest_tile(n_all, tiles["tn_in"]))

    o_sb, (w_sb16, w_ssd16, w_out16, w_up16, w_down16) = _sb_attn(
        p, dims, batch, seq, min(tiles["tq"], seq), tiles["sb_chains"],
        [w_sb_proj, w_ssd_proj, w_out, w_up, w_down])

    chunk = min(tiles["chunk"], seq)
    bcast = lambda v: jnp.broadcast_to(v.reshape(-1, 1), (heads, LANE))
    y = _ssd(p, dt_t, ssd_conv_w, row(ssd_conv_b), bcast(dt_bias), bcast(a_log),
             row(jnp.repeat(d_skip, dims.ssd_head_dim)), row(ssd_norm), dims, batch, seq, chunk,
             (z_off, x_off, b_off, c_off), tiles["ssd_groups_per_step"], tiles["ssd_chunks_per_step"])

    merged = _merge(o_sb, y, p, w_sb16, w_ssd16, gsb_off, gssd_off,
                    _largest_tile(m, tiles["tm_merge"], 16), _largest_tile(d, tiles["tn_merge"]))
    tm2 = _largest_tile(m, tiles["tm_res"], 16)
    x1 = _proj_norm_res(merged, w_out16, row(norm_mix_post), x2, tm2, d, "out_proj")

    tm_up = _largest_tile(seq, tiles["tm"], 16)
    act = _ffn_up(x1, row(norm_ffn_pre), w_up16, ffn_conv_w, row(ffn_conv_b), seq, tm_up,
                  _largest_tile(dims.d_ff, tiles["tn_up"]))
    out = _proj_norm_res(act, w_down16, row(norm_ffn_post), x1, tm2,
                         _largest_tile(d, tiles["tn_down"]), "ffn_down")
    return out.reshape(batch, seq, d)


TILES = dict(tm=1024, tn_in=2048, tq=128, sb_chains=16, chunk=128, ssd_groups_per_step=4, ssd_chunks_per_step=4, tm_merge=512, tn_merge=2048, tm_res=512, tn_up=512, tn_down=2048)


def kernel(x, norm_mix_pre, w_in, ssd_conv_w, ssd_conv_b, dt_bias, a_log, d_skip, ssd_norm, w_sb_proj, w_ssd_proj,
           w_out, norm_mix_post, norm_ffn_pre, w_up, ffn_conv_w, ffn_conv_b, w_down, norm_ffn_post):
    dims = Dims()
    args = (norm_mix_pre, w_in, ssd_conv_w, ssd_conv_b, dt_bias, a_log, d_skip, ssd_norm, w_sb_proj, w_ssd_proj,
            w_out, norm_mix_post, norm_ffn_pre, w_up, ffn_conv_w, ffn_conv_b, w_down, norm_ffn_post)
    for layer in range(w_in.shape[0]):
        x = _block(x, *(a[layer] for a in args), dims, TILES)
    return x
```

```python
import dataclasses
import functools
import math

import numpy as np

import jax
import jax.numpy as jnp
from jax import lax
from jax.experimental import pallas as pl
from jax.experimental.pallas import tpu as pltpu

F32 = jnp.float32
BF16 = jnp.bfloat16
LANE = 128
EPS = 1e-6


@dataclasses.dataclass(frozen=True)
class Dims:
    d_model: int = 2048
    sb_heads: int = 16
    sb_head_dim: int = 128
    ssd_inner: int = 4096
    ssd_head_dim: int = 64
    ssd_groups: int = 8
    ssd_state: int = 128
    ssd_conv: int = 4
    d_ff: int = 5632
    ffn_conv: int = 3

    @property
    def sb_width(self):
        return self.sb_heads * self.sb_head_dim

    @property
    def ssd_heads(self):
        return self.ssd_inner // self.ssd_head_dim

    @property
    def ssd_xbc(self):
        return self.ssd_inner + 2 * self.ssd_groups * self.ssd_state


SB_LOG_WEIGHT_FLOOR = -110.0
SB_EAGER_STEPS = 2
VMEM_LIMIT = 60 * 1024 * 1024


def _softplus(x):
    return jnp.maximum(x, 0.0) + jnp.log1p(jnp.exp(-jnp.abs(x)))


def _rows(v, n):
    return jnp.tile(jnp.broadcast_to(v, (8, v.shape[1])), (n // 8, 1))


def _silu(x):
    h = 0.5 * x
    return h + h * jnp.tanh(h)


def _params(*sem):
    return pltpu.CompilerParams(dimension_semantics=sem, vmem_limit_bytes=VMEM_LIMIT)


def _in_proj_kernel(x_ref, g_ref, w_ref, wdt_ref, p_ref, dt_ref, h_scr):
    j = pl.program_id(1)

    @pl.when(j == 0)
    def _():
        x = x_ref[...]
        ms = jnp.mean(x * x, axis=-1, keepdims=True)
        h = (x * lax.rsqrt(ms + EPS) * g_ref[...]).astype(BF16)
        h_scr[...] = h
        dt_ref[...] = lax.dot_general(wdt_ref[...].astype(BF16), h, (((1,), (1,)), ((), ())),
                                      preferred_element_type=F32)

    acc = lax.dot_general(h_scr[...], w_ref[...], (((1,), (1,)), ((), ())), preferred_element_type=F32)
    for c in range(p_ref.shape[0]):
        p_ref[c] = acc[:, c * LANE:(c + 1) * LANE].astype(BF16)


def _in_proj(x2, gain, w_all, w_dt, hd, seq, tm, tn, skip_at, skip):
    m, d = x2.shape
    n = w_all.shape[0] - skip
    w_row = lambda j: pl.multiple_of(j * tn + jnp.where(j * tn >= skip_at, skip, 0), 16)
    tps = seq // tm
    return pl.pallas_call(
        _in_proj_kernel,
        grid=(m // tm, n // tn),
        in_specs=[
            pl.BlockSpec((tm, d), lambda i, j: (i, 0)),
            pl.BlockSpec((1, d), lambda i, j: (0, 0)),
            pl.BlockSpec((pl.Element(tn), pl.Element(d)), lambda i, j: (w_row(j), 0)),
            pl.BlockSpec((hd, d), lambda i, j: (0, 0)),
        ],
        out_specs=[
            pl.BlockSpec((tn // LANE, tm, LANE), lambda i, j: (j, i, 0)),
            pl.BlockSpec((None, hd, tm), lambda i, j: (i // tps, 0, i % tps)),
        ],
        out_shape=[
            jax.ShapeDtypeStruct((n // LANE, m, LANE), BF16),
            jax.ShapeDtypeStruct((m // seq, hd, seq), F32),
        ],
        scratch_shapes=[pltpu.VMEM((tm, d), BF16)],
        compiler_params=_params("parallel", "arbitrary"),
        name="in_proj",
    )(x2, gain, w_all, w_dt)


def _sb_attn_kernel(q_ref, k_ref, v_ref, *rest, tq, nchain, n_cast):
    o_ref = rest[n_cast]
    for src, dst in zip(rest[:n_cast], rest[n_cast + 1:]):
        dst[...] = src[...].astype(dst.dtype)
    s, dh = q_ref.shape
    ngroups = s // (tq * nchain)
    row = lax.broadcasted_iota(jnp.int32, (tq, tq), 0)
    col = lax.broadcasted_iota(jnp.int32, (tq, tq), 1)
    causal = col < row
    suffix = jnp.where(row > col, 1.0, 0.0).astype(BF16)
    sum_rhs = jnp.concatenate([suffix, jnp.ones((tq, tq), BF16)], axis=1)

    def scores(qbs, js):
        return [lax.dot_general(qb, k_ref[pl.ds(pl.multiple_of(j * tq, tq), tq), :], (((1,), (1,)), ((), ())),
                                preferred_element_type=F32) for qb, j in zip(qbs, js)]

    def key_blocks(zs, js, carries, accs, masked):
        starts = [pl.multiple_of(j * tq, tq) for j in js]
        sps, parts = [], []
        for z in zs:
            sp = jnp.maximum(z, 0.0) + jnp.log(1.0 + jnp.exp(-jnp.abs(z)))
            if masked:
                sp = jnp.where(causal, sp, 0.0)
            sps.append(sp)
            parts.append(sp.astype(BF16))
        sums = jnp.dot(jnp.concatenate(parts, axis=0), sum_rhs, preferred_element_type=F32)
        ws = []
        for c, (z, sp) in enumerate(zip(zs, sps)):
            w = jnp.exp(z - sp - sums[c * tq:(c + 1) * tq, :tq] - carries[c])
            if masked:
                w = jnp.where(causal, w, 0.0)
            ws.append(w.astype(BF16))
        new_a = [acc + jnp.dot(w, v_ref[pl.ds(st, tq), :], preferred_element_type=F32)
                 for acc, w, st in zip(accs, ws, starts)]
        new_c = [carries[c] + sums[c * tq:(c + 1) * tq, tq:] for c in range(len(js))]
        return new_c, new_a

    def q_group(gi, _):
        qi = [gi * nchain + c for c in range(nchain)]
        qstart = [pl.multiple_of(i * tq, tq) for i in qi]
        qb = [q_ref[pl.ds(st, tq), :] for st in qstart]

        def retire(t, carries):
            carries = [jnp.where(qi[c] - t >= 0, carries[c], jnp.inf) for c in range(nchain)]
            return carries, jnp.min(functools.reduce(jnp.minimum, carries))

        def back(t):
            return [jnp.maximum(qi[c] - t, 0) for c in range(nchain)]

        zs = [scores(qb, back(t)) for t in range(SB_EAGER_STEPS + 1)]
        carries, accs = key_blocks(zs[0], qi, [jnp.zeros((tq, tq), F32)] * nchain,
                                   [jnp.zeros((tq, dh), F32)] * nchain, True)
        carries, low = retire(1, carries)
        for t in range(1, SB_EAGER_STEPS + 1):
            carries, accs = key_blocks(zs[t], back(t), carries, accs, False)
            carries, low = retire(t + 1, carries)

        def cond(st):
            return st[1] < -SB_LOG_WEIGHT_FLOOR

        def body(st):
            t, _, carries, accs = st
            carries, accs = key_blocks(scores(qb, back(t)), back(t), carries, accs, False)
            carries, low = retire(t + 1, carries)
            return t + 1, low, carries, accs

        _, _, _, accs = lax.while_loop(cond, body, (SB_EAGER_STEPS + 1, low, carries, accs))
        for c in range(nchain):
            o_ref[pl.ds(qstart[c], tq), :] = accs[c].astype(o_ref.dtype)
        return 0

    lax.fori_loop(0, ngroups, q_group, 0)


def _sb_attn(p, dims, batch, seq, tq, nchain, cast_weights):
    h, dh = dims.sb_heads, dims.sb_head_dim
    m = batch * seq
    steps = batch * h
    spec = lambda off: pl.BlockSpec((None, seq, dh), lambda b, hh: (off + hh, b, 0))
    slab = lambda w: pl.BlockSpec((w.shape[0] // steps, w.shape[1]), lambda b, hh: (b * h + hh, 0))
    outs = pl.pallas_call(
        functools.partial(_sb_attn_kernel, tq=tq, nchain=nchain, n_cast=len(cast_weights)),
        grid=(batch, h),
        in_specs=[spec(0), spec(h), spec(2 * h)] + [slab(w) for w in cast_weights],
        out_specs=[pl.BlockSpec((seq, dh), lambda b, hh: (b, hh))] + [slab(w) for w in cast_weights],
        out_shape=[jax.ShapeDtypeStruct((m, h * dh), BF16)]
                  + [jax.ShapeDtypeStruct(w.shape, BF16) for w in cast_weights],
        compiler_params=_params("arbitrary", "arbitrary"),
        name="sb_attn",
    )(p, p, p, *cast_weights)
    return outs[0], outs[1:]


def _ssd_kernel(x_ref, b_ref, c_ref, z_ref, dt_ref, wx_ref, wb_ref, wc_ref, bx_ref, bb_ref, bc_ref,
                dtb_ref, alog_ref, dskip_ref, norm_ref, shift_ref, y_ref, halo_scr, state_scr, *, hd, kconv, gps):
    nbatch, l = x_ref.shape[1], LANE
    nxb = x_ref.shape[0] // gps
    hg = dt_ref.shape[1] // gps
    inner = nxb * LANE
    n_state = b_ref.shape[-1]
    width = inner + 2 * n_state
    assert l == LANE and 2 * hd == LANE and n_state == LANE

    @pl.when(pl.program_id(1) == 0)
    def _():
        halo_scr[...] = jnp.zeros_like(halo_scr)
        state_scr[...] = jnp.zeros_like(state_scr)

    rows = functools.partial(_rows, n=l)
    lane = lax.broadcasted_iota(jnp.int32, (l, LANE), 1)
    lane_t = lax.broadcasted_iota(jnp.int32, (hg, l), 1)
    low_half = lane < hd
    tril = lane <= lax.broadcasted_iota(jnp.int32, (l, l), 0)
    pad = jnp.zeros((LANE - hg, l), F32)

    gcols = lambda ref, gi, w: ref[:, gi * w:(gi + 1) * w]
    taps8, taps, bcat, dskip, norm_g, neg_a, dtb = [], [], [], [], [], [], []
    for gi in range(gps):
        wcat = jnp.concatenate([gcols(wx_ref, gi, inner), gcols(wb_ref, gi, n_state), gcols(wc_ref, gi, n_state)],
                               axis=1)
        wcat16 = wcat.astype(BF16)
        taps8.append([jnp.broadcast_to(wcat[k:k + 1, :], (8, width)) for k in range(kconv)])
        taps.append([jnp.tile(jnp.broadcast_to(wcat16[k:k + 1, :], (16, width)), (l // 16, 1))
                     for k in range(kconv)])
        bcat.append(rows(jnp.concatenate([gcols(bx_ref, gi, inner), gcols(bb_ref, gi, n_state),
                                          gcols(bc_ref, gi, n_state)], axis=1)))
        dskip.append(rows(gcols(dskip_ref, gi, inner)))
        norm_g.append(rows(gcols(norm_ref, gi, inner)))
        neg_a.append(-jnp.exp(alog_ref[gi * hg:(gi + 1) * hg, :]))
        dtb.append(dtb_ref[gi * hg:(gi + 1) * hg, :])

    env = (x_ref, b_ref, c_ref, z_ref, dt_ref, shift_ref, y_ref, halo_scr, state_scr, kconv, gps, nbatch, l, nxb, hg,
           inner, n_state, width, rows, lane_t, low_half, tril, pad, taps8, taps, bcat, dskip, norm_g, neg_a, dtb)
    lax.fori_loop(0, x_ref.shape[2] // l, lambda ci, carry: _ssd_chunk(ci, env), 0)


def _ssd_chunk(ci, env):
    (x_ref, b_ref, c_ref, z_ref, dt_ref, shift_ref, y_ref, halo_scr, state_scr, kconv, gps, nbatch, l, nxb, hg,
     inner, n_state, width, rows, lane_t, low_half, tril, pad, taps8, taps, bcat, dskip, norm_g, neg_a, dtb) = env
    rows_t = pl.ds(pl.multiple_of(ci * l, l), l)

    chains = [(gi, b) for gi in range(gps) for b in range(nbatch)]

    acts = []
    for gi, b in chains:
        raw = jnp.concatenate([x_ref[gi * nxb + c, b, rows_t, :] for c in range(nxb)]
                              + [b_ref[gi, b, rows_t, :], c_ref[gi, b, rows_t, :]], axis=1)
        body = jnp.dot(shift_ref[...], jnp.concatenate([raw * taps[gi][k] for k in range(kconv)], axis=0),
                       preferred_element_type=F32)
        head = jnp.concatenate([halo_scr[gi, b], raw[:8].astype(F32)], axis=0)
        halo_scr[gi, b] = raw[l - 8:, :].astype(F32)
        first = jnp.zeros((8, width), F32)
        for k in range(kconv):
            off = 8 - (kconv - 1) + k
            first = first + taps8[gi][k] * head[off:off + 8, :]
        acts.append(_silu(jnp.concatenate([first, body[8:]], axis=0) + bcat[gi]))
    x16 = [a[:, :inner].astype(BF16) for a in acts]
    bms = [a[:, inner:inner + n_state] for a in acts]
    cms = [a[:, inner + n_state:] for a in acts]

    cbs = [lax.dot_general(cm.astype(BF16), bm.astype(BF16), (((1,), (1,)), ((), ())),
                           preferred_element_type=F32) for cm, bm in zip(cms, bms)]
    bm_ts = [bm.T for bm in bms]
    src_rows, coef_rows, keep_rows, cs_cols = [], [], [], []
    for gi, b in chains:
        dt = _softplus(dt_ref[b, gi * hg:(gi + 1) * hg, rows_t] + dtb[gi])
        cs = dt * neg_a[gi]
        sh = 1
        while sh < l:
            cs = cs + jnp.where(lane_t >= sh, pltpu.roll(cs, sh, axis=1), 0.0)
            sh *= 2
        cs_last = jnp.broadcast_to(cs[:, l - 1:l], (hg, l))
        src_rows.append(cs - jnp.log(dt))
        coef_rows.append(dt * jnp.exp(cs_last - cs))
        keep_rows.append(jnp.exp(cs_last))
        cs_cols.append(jnp.concatenate([cs, pad], axis=0).T)

    states = [state_scr[gi, b] for gi, b in chains]
    y_blocks = [[] for _ in chains]
    new_state = [[] for _ in chains]
    for blk in range(nxb):
        for n in range(len(chains)):
            lhs, scaled_bt = [], []
            for h in (2 * blk, 2 * blk + 1):
                ccol = jnp.broadcast_to(cs_cols[n][:, h:h + 1], (l, LANE))
                seg = ccol - rows(src_rows[n][h:h + 1, :])
                w_in = (cbs[n] * jnp.exp(jnp.where(tril, seg, -jnp.inf))).astype(BF16)
                w_prev = (cms[n] * jnp.exp(ccol)).astype(BF16)
                lhs.append(jnp.concatenate([w_in, w_prev], axis=1))
                scaled_bt.append((bm_ts[n] * rows(coef_rows[n][h:h + 1, :])).astype(BF16))
            xblk = x16[n][:, blk * LANE:(blk + 1) * LANE]
            sblk = states[n][:, blk * LANE:(blk + 1) * LANE]
            rhs = jnp.concatenate([xblk, sblk.astype(BF16)], axis=0)
            yy = jnp.dot(jnp.concatenate(lhs, axis=0), rhs, preferred_element_type=F32)
            y_blocks[n].append(jnp.where(low_half, yy[:l], yy[l:]))
            ss = jnp.dot(jnp.concatenate(scaled_bt, axis=0), xblk, preferred_element_type=F32)
            kr = keep_rows[n]
            keep = jnp.where(low_half[:1], kr[2 * blk:2 * blk + 1, :], kr[2 * blk + 1:2 * blk + 2, :])
            new_state[n].append(sblk * rows(keep) + jnp.where(low_half, ss[:n_state], ss[n_state:]))

    for n, (gi, b) in enumerate(chains):
        state_scr[gi, b] = jnp.concatenate(new_state[n], axis=1)
        y = jnp.concatenate(y_blocks[n], axis=1) + acts[n][:, :inner] * dskip[gi]
        zf = jnp.concatenate([z_ref[gi * nxb + c, b, rows_t, :] for c in range(nxb)], axis=1).astype(F32)
        y = y * _silu(zf)
        ms = jnp.mean(y * y, axis=-1, keepdims=True)
        y = y * lax.rsqrt(ms + EPS) * norm_g[gi]
        y_ref[b, rows_t, gi * inner:(gi + 1) * inner] = y.astype(y_ref.dtype)
    return 0


def _ssd(p, dt_t, conv_w, conv_b, dtb, alog, dskip_e, norm_g, dims, batch, seq, chunk, blk_off, gps, cps):
    g = dims.ssd_groups
    inner_g = dims.ssd_inner // g
    nxb = inner_g // LANE
    hg = dims.ssd_heads // g
    n_state = dims.ssd_state
    span = cps * chunk
    nt = seq // span
    kconv = conv_w.shape[0]
    z_off, x_off, b_off, c_off = blk_off
    p4 = p.reshape(p.shape[0], batch, seq, LANE)
    sb_off = dims.ssd_inner // n_state
    assert all(o % (gps * nxb) == 0 for o in (z_off, x_off)) and all(o % gps == 0 for o in (b_off, c_off, sb_off, g))
    shift = np.zeros((chunk, kconv * chunk), np.float32)
    for k in range(kconv):
        t = np.arange(kconv - 1 - k, chunk)
        shift[t, chunk * k + t - (kconv - 1) + k] = 1.0
    shift = jnp.asarray(shift, BF16)
    in_specs = [
        pl.BlockSpec((gps * nxb, batch, span, LANE), lambda gg, t: (x_off // (gps * nxb) + gg, 0, t, 0)),
        pl.BlockSpec((gps, batch, span, LANE), lambda gg, t: (b_off // gps + gg, 0, t, 0)),
        pl.BlockSpec((gps, batch, span, LANE), lambda gg, t: (c_off // gps + gg, 0, t, 0)),
        pl.BlockSpec((gps * nxb, batch, span, LANE), lambda gg, t: (z_off // (gps * nxb) + gg, 0, t, 0)),
        pl.BlockSpec((batch, gps * hg, span), lambda gg, t: (0, gg, t)),
        pl.BlockSpec((kconv, gps * inner_g), lambda gg, t: (0, gg)),
        pl.BlockSpec((kconv, gps * n_state), lambda gg, t: (0, sb_off // gps + gg)),
        pl.BlockSpec((kconv, gps * n_state), lambda gg, t: (0, (sb_off + g) // gps + gg)),
        pl.BlockSpec((1, gps * inner_g), lambda gg, t: (0, gg)),
        pl.BlockSpec((1, gps * n_state), lambda gg, t: (0, sb_off // gps + gg)),
        pl.BlockSpec((1, gps * n_state), lambda gg, t: (0, (sb_off + g) // gps + gg)),
        pl.BlockSpec((gps * hg, LANE), lambda gg, t: (gg, 0)),
        pl.BlockSpec((gps * hg, LANE), lambda gg, t: (gg, 0)),
        pl.BlockSpec((1, gps * inner_g), lambda gg, t: (0, gg)),
        pl.BlockSpec((1, gps * inner_g), lambda gg, t: (0, gg)),
        pl.BlockSpec(shift.shape, lambda gg, t: (0, 0)),
    ]
    y = pl.pallas_call(
        functools.partial(_ssd_kernel, hd=dims.ssd_head_dim, kconv=kconv, gps=gps),
        grid=(g // gps, nt),
        in_specs=in_specs,
        out_specs=pl.BlockSpec((batch, span, gps * inner_g), lambda gg, t: (0, t, gg)),
        out_shape=jax.ShapeDtypeStruct((batch, seq, dims.ssd_inner), BF16),
        scratch_shapes=[pltpu.VMEM((gps, batch, 8, inner_g + 2 * n_state), F32),
                        pltpu.VMEM((gps, batch, n_state, inner_g), F32)],
        compiler_params=_params("parallel", "arbitrary"),
        name="ssd",
    )(p4, p4, p4, p4, dt_t, conv_w, conv_w, conv_w, conv_b, conv_b, conv_b, dtb, alog, dskip_e, norm_g, shift)
    return y.reshape(batch * seq, dims.ssd_inner)


def _merge_kernel(o_ref, y_ref, gsb_ref, gssd_ref, wsb_ref, wssd_ref, out_ref):
    a = jnp.dot(o_ref[...], wsb_ref[...], preferred_element_type=F32)
    b = jnp.dot(y_ref[...], wssd_ref[...], preferred_element_type=F32)
    gsb = jnp.concatenate([gsb_ref[c] for c in range(gsb_ref.shape[0])], axis=1).astype(F32)
    gssd = jnp.concatenate([gssd_ref[c] for c in range(gssd_ref.shape[0])], axis=1).astype(F32)
    out_ref[...] = (jax.nn.sigmoid(gsb) * a + jax.nn.sigmoid(gssd) * b).astype(out_ref.dtype)


def _merge(o_sb, y, p, w_sb, w_ssd, gsb_off, gssd_off, tm, tn):
    m, ko = o_sb.shape
    ky = y.shape[1]
    d = w_sb.shape[1]
    nb = tn // LANE
    return pl.pallas_call(
        _merge_kernel,
        grid=(m // tm, d // tn),
        in_specs=[
            pl.BlockSpec((tm, ko), lambda i, j: (i, 0)),
            pl.BlockSpec((tm, ky), lambda i, j: (i, 0)),
            pl.BlockSpec((nb, tm, LANE), lambda i, j: (gsb_off // nb + j, i, 0)),
            pl.BlockSpec((nb, tm, LANE), lambda i, j: (gssd_off // nb + j, i, 0)),
            pl.BlockSpec((ko, tn), lambda i, j: (0, j), pipeline_mode=pl.Buffered(1 if tn == d else 2)),
            pl.BlockSpec((ky, tn), lambda i, j: (0, j), pipeline_mode=pl.Buffered(1 if tn == d else 2)),
        ],
        out_specs=pl.BlockSpec((tm, tn), lambda i, j: (i, j)),
        out_shape=jax.ShapeDtypeStruct((m, d), BF16),
        compiler_params=_params("parallel", "arbitrary"),
        name="merge",
    )(o_sb, y, p, p, w_sb, w_ssd)


def _proj_norm_res_kernel(a_ref, w_ref, g_ref, x_ref, out_ref, f_scr):
    nj, tm, tn = f_scr.shape
    j = pl.program_id(1)
    f_scr[j] = jnp.dot(a_ref[...], w_ref[...], preferred_element_type=F32)

    @pl.when(j == nj - 1)
    def _():
        ss = jnp.zeros((tm, 1), F32)
        for c in range(nj):
            f = f_scr[c]
            ss = ss + jnp.sum(f * f, axis=-1, keepdims=True)
        inv = lax.rsqrt(ss * (1.0 / (nj * tn)) + EPS)
        for c in range(nj):
            cols = slice(c * tn, (c + 1) * tn)
            out_ref[:, cols] = x_ref[:, cols] + f_scr[c] * inv * _rows(g_ref[:, cols], tm)


def _proj_norm_res(a, w, gain, x2, tm, tn, name):
    m, kdim = a.shape
    d = w.shape[1]
    return pl.pallas_call(
        _proj_norm_res_kernel,
        grid=(m // tm, d // tn),
        in_specs=[
            pl.BlockSpec((tm, kdim), lambda i, j: (i, 0)),
            pl.BlockSpec((kdim, tn), lambda i, j: (0, j), pipeline_mode=pl.Buffered(1 if tn == d else 2)),
            pl.BlockSpec((1, d), lambda i, j: (0, 0)),
            pl.BlockSpec((tm, d), lambda i, j: (i, 0)),
        ],
        out_specs=pl.BlockSpec((tm, d), lambda i, j: (i, 0)),
        out_shape=jax.ShapeDtypeStruct((m, d), F32),
        scratch_shapes=[pltpu.VMEM((d // tn, tm, tn), F32)],
        compiler_params=_params("parallel", "arbitrary"),
        name=name,
    )(a, w, gain, x2)


FFN_HALO = 16


def _ffn_up_kernel(x_ref, halo_ref, g_ref, wg_ref, wv_ref, cwg_ref, cwv_ref, cbg_ref, cbv_ref, out_ref, h_scr,
                   *, tiles_per_seq, kconv):
    tm = x_ref.shape[0]

    def norm(v):
        ms = jnp.mean(v * v, axis=-1, keepdims=True)
        return (v * lax.rsqrt(ms + EPS) * g_ref[...]).astype(BF16)

    @pl.when(pl.program_id(1) == 0)
    def _():
        seq_start = (pl.program_id(0) % tiles_per_seq) == 0
        h_scr[:FFN_HALO, :] = jnp.where(seq_start, jnp.zeros((), BF16), norm(halo_ref[...]))
        h_scr[FFN_HALO:, :] = norm(x_ref[...])

    h = h_scr[...]

    def conv(w_ref, cw_ref, cb_ref):
        up = jnp.dot(h, w_ref[...], preferred_element_type=F32)
        u = _rows(cb_ref[...], tm)
        for k in range(kconv):
            off = FFN_HALO - (kconv - 1) + k
            u = u + _rows(cw_ref[k:k + 1, :], tm) * up[off:off + tm, :]
        return u

    gate = conv(wg_ref, cwg_ref, cbg_ref)
    val = conv(wv_ref, cwv_ref, cbv_ref)
    c = math.sqrt(2.0 / math.pi)
    half = 0.5 * gate
    act = half + half * jnp.tanh(gate * (c + (c * 0.044715) * (gate * gate)))
    out_ref[...] = (act * val).astype(out_ref.dtype)


def _ffn_up(x1, gain, w_up, conv_w, conv_b, seq, tm, tn):
    m, d = x1.shape
    dff = w_up.shape[1] // 2
    nj = dff // tn
    kconv = conv_w.shape[0]
    hb = tm // FFN_HALO
    return pl.pallas_call(
        functools.partial(_ffn_up_kernel, tiles_per_seq=seq // tm, kconv=kconv),
        grid=(m // tm, nj),
        in_specs=[
            pl.BlockSpec((tm, d), lambda i, j: (i, 0)),
            pl.BlockSpec((FFN_HALO, d), lambda i, j: (jnp.maximum(i * hb - 1, 0), 0)),
            pl.BlockSpec((1, d), lambda i, j: (0, 0)),
            pl.BlockSpec((d, tn), lambda i, j: (0, j)),
            pl.BlockSpec((d, tn), lambda i, j: (0, nj + j)),
            pl.BlockSpec((kconv, tn), lambda i, j: (0, j)),
            pl.BlockSpec((kconv, tn), lambda i, j: (0, nj + j)),
            pl.BlockSpec((1, tn), lambda i, j: (0, j)),
            pl.BlockSpec((1, tn), lambda i, j: (0, nj + j)),
        ],
        out_specs=pl.BlockSpec((tm, tn), lambda i, j: (i, j)),
        out_shape=jax.ShapeDtypeStruct((m, dff), BF16),
        scratch_shapes=[pltpu.VMEM((FFN_HALO + tm, d), BF16)],
        compiler_params=_params("parallel", "arbitrary"),
        name="ffn_up",
    )(x1, x1, gain, w_up, w_up, conv_w, conv_w, conv_b, conv_b)


def _largest_tile(total, want, quantum=LANE):
    t = min(want, total)
    while total % t or t % quantum:
        t -= quantum
    return t


def _block(x, norm_mix_pre, w_in, ssd_conv_w, ssd_conv_b, dt_bias, a_log, d_skip, ssd_norm, w_sb_proj,
           w_ssd_proj, w_out, norm_mix_post, norm_ffn_pre, w_up, ffn_conv_w, ffn_conv_b, w_down, norm_ffn_post,
           dims, tiles):
    batch, seq, d = x.shape
    m = batch * seq
    sbw, inner, xbc, heads = dims.sb_width, dims.ssd_inner, dims.ssd_xbc, dims.ssd_heads
    x2 = x.reshape(m, d)
    row = lambda v: v.reshape(1, -1)

    dt_lo = 3 * sbw + inner + xbc
    w_in_t = w_in.T
    n_all = w_in_t.shape[0] - heads
    row_scale = jnp.where(jnp.arange(w_in_t.shape[0]) < sbw, dims.sb_head_dim ** -0.5, 1.0).astype(F32)[:, None]
    w_all = (w_in_t * row_scale).astype(BF16)
    w_dt = w_in_t[dt_lo:dt_lo + heads]
    z_off = 3 * sbw // LANE
    x_off = z_off + inner // LANE
    b_off = x_off + inner // LANE
    c_off = b_off + dims.ssd_groups * dims.ssd_state // LANE
    gsb_off = c_off + dims.ssd_groups * dims.ssd_state // LANE
    gssd_off = gsb_off + d // LANE

    tm = _largest_tile(seq, tiles["tm"], 16)
    p, dt_t = _in_proj(x2, row(norm_mix_pre), w_all, w_dt, heads, seq, tm,
                       _largest_tile(math.gcd(n_all, dt_lo), tiles["tn_in"]), dt_lo, heads)

    o_sb, (w_sb16, w_ssd16, w_out16, w_up16, w_down16) = _sb_attn(
        p, dims, batch, seq, min(tiles["tq"], seq), tiles["sb_chains"],
        [w_sb_proj, w_ssd_proj, w_out, w_up, w_down])

    chunk = min(tiles["chunk"], seq)
    bcast = lambda v: jnp.broadcast_to(v.reshape(-1, 1), (heads, LANE))
    y = _ssd(p, dt_t, ssd_conv_w, row(ssd_conv_b), bcast(dt_bias), bcast(a_log),
             row(jnp.repeat(d_skip, dims.ssd_head_dim)), row(ssd_norm), dims, batch, seq, chunk,
             (z_off, x_off, b_off, c_off), tiles["ssd_groups_per_step"], tiles["ssd_chunks_per_step"])

    merged = _merge(o_sb, y, p, w_sb16, w_ssd16, gsb_off, gssd_off,
                    _largest_tile(m, tiles["tm_merge"], 16), _largest_tile(d, tiles["tn_merge"]))
    tm2 = _largest_tile(m, tiles["tm_res"], 16)
    x1 = _proj_norm_res(merged, w_out16, row(norm_mix_post), x2, tm2, d, "out_proj")

    tm_up = _largest_tile(seq, tiles["tm"], 16)
    act = _ffn_up(x1, row(norm_ffn_pre), w_up16, ffn_conv_w, row(ffn_conv_b), seq, tm_up,
                  _largest_tile(dims.d_ff, tiles["tn_up"]))
    out = _proj_norm_res(act, w_down16, row(norm_ffn_post), x1, tm2,
                         _largest_tile(d, tiles["tn_down"]), "ffn_down")
    return out.reshape(batch, seq, d)


TILES = dict(tm=1024, tn_in=2048, tq=128, sb_chains=16, chunk=128, ssd_groups_per_step=4, ssd_chunks_per_step=4, tm_merge=512, tn_merge=2048, tm_res=512, tn_up=512, tn_down=2048)


def kernel(x, norm_mix_pre, w_in, ssd_conv_w, ssd_conv_b, dt_bias, a_log, d_skip, ssd_norm, w_sb_proj, w_ssd_proj,
           w_out, norm_mix_post, norm_ffn_pre, w_up, ffn_conv_w, ffn_conv_b, w_down, norm_ffn_post):
    dims = Dims()
    args = (norm_mix_pre, w_in, ssd_conv_w, ssd_conv_b, dt_bias, a_log, d_skip, ssd_norm, w_sb_proj, w_ssd_proj,
            w_out, norm_mix_post, norm_ffn_pre, w_up, ffn_conv_w, ffn_conv_b, w_down, norm_ffn_post)
    for layer in range(w_in.shape[0]):
        x = _block(x, *(a[layer] for a in args), dims, TILES)
    return x
```

```python
import dataclasses
import functools
import math

import numpy as np

import jax
import jax.numpy as jnp
from jax import lax
from jax.experimental import pallas as pl
from jax.experimental.pallas import tpu as pltpu

F32 = jnp.float32
BF16 = jnp.bfloat16
LANE = 128
EPS = 1e-6


@dataclasses.dataclass(frozen=True)
class Dims:
    d_model: int = 2048
    sb_heads: int = 16
    sb_head_dim: int = 128
    ssd_inner: int = 4096
    ssd_head_dim: int = 64
    ssd_groups: int = 8
    ssd_state: int = 128
    d_ff: int = 5632

    @property
    def sb_width(self):
        return self.sb_heads * self.sb_head_dim

    @property
    def ssd_heads(self):
        return self.ssd_inner // self.ssd_head_dim

    @property
    def ssd_xbc(self):
        return self.ssd_inner + 2 * self.ssd_groups * self.ssd_state


SB_LOG_WEIGHT_FLOOR = -110.0
SB_EAGER_STEPS = 2
VMEM_LIMIT = 60 * 1024 * 1024


def _softplus(x):
    return jnp.maximum(x, 0.0) + jnp.log1p(jnp.exp(-jnp.abs(x)))


def _rows(v, n):
    return jnp.tile(jnp.broadcast_to(v, (8, v.shape[1])), (n // 8, 1))


def _silu(x):
    h = 0.5 * x
    return h + h * jnp.tanh(h)


def _params(*sem):
    return pltpu.CompilerParams(dimension_semantics=sem, vmem_limit_bytes=VMEM_LIMIT)


def _in_proj_kernel(x_ref, g_ref, w_ref, wdt_ref, p_ref, dt_ref, h_scr):
    j = pl.program_id(1)

    @pl.when(j == 0)
    def _():
        x = x_ref[...]
        ms = jnp.mean(x * x, axis=-1, keepdims=True)
        h = (x * lax.rsqrt(ms + EPS) * g_ref[...]).astype(BF16)
        h_scr[...] = h
        dt_ref[...] = lax.dot_general(wdt_ref[...].astype(BF16), h, (((1,), (1,)), ((), ())),
                                      preferred_element_type=F32)

    acc = lax.dot_general(h_scr[...], w_ref[...], (((1,), (1,)), ((), ())), preferred_element_type=F32)
    for c in range(p_ref.shape[0]):
        p_ref[c] = acc[:, c * LANE:(c + 1) * LANE].astype(BF16)


def _in_proj(x2, gain, w_all, w_dt, hd, seq, tm, tn, skip_at, skip):
    m, d = x2.shape
    n = w_all.shape[0] - skip
    w_row = lambda j: pl.multiple_of(j * tn + jnp.where(j * tn >= skip_at, skip, 0), 16)
    tps = seq // tm
    return pl.pallas_call(
        _in_proj_kernel,
        grid=(m // tm, n // tn),
        in_specs=[
            pl.BlockSpec((tm, d), lambda i, j: (i, 0)),
            pl.BlockSpec((1, d), lambda i, j: (0, 0)),
            pl.BlockSpec((pl.Element(tn), pl.Element(d)), lambda i, j: (w_row(j), 0)),
            pl.BlockSpec((hd, d), lambda i, j: (0, 0)),
        ],
        out_specs=[
            pl.BlockSpec((tn // LANE, tm, LANE), lambda i, j: (j, i, 0)),
            pl.BlockSpec((None, hd, tm), lambda i, j: (i // tps, 0, i % tps)),
        ],
        out_shape=[
            jax.ShapeDtypeStruct((n // LANE, m, LANE), BF16),
            jax.ShapeDtypeStruct((m // seq, hd, seq), F32),
        ],
        scratch_shapes=[pltpu.VMEM((tm, d), BF16)],
        compiler_params=_params("parallel", "arbitrary"),
        name="in_proj",
    )(x2, gain, w_all, w_dt)


def _sb_attn_kernel(q_ref, k_ref, v_ref, *rest, tq, nchain, n_cast):
    o_ref = rest[n_cast]
    for src, dst in zip(rest[:n_cast], rest[n_cast + 1:]):
        dst[...] = src[...].astype(dst.dtype)
    s, dh = q_ref.shape
    ngroups = s // (tq * nchain)
    row = lax.broadcasted_iota(jnp.int32, (tq, tq), 0)
    col = lax.broadcasted_iota(jnp.int32, (tq, tq), 1)
    causal = col < row
    suffix = jnp.where(row > col, 1.0, 0.0).astype(BF16)
    sum_rhs = jnp.concatenate([suffix, jnp.ones((tq, tq), BF16)], axis=1)

    def scores(qbs, js):
        return [lax.dot_general(qb, k_ref[pl.ds(pl.multiple_of(j * tq, tq), tq), :], (((1,), (1,)), ((), ())),
                                preferred_element_type=F32) for qb, j in zip(qbs, js)]

    def key_blocks(zs, js, carries, accs, masked):
        starts = [pl.multiple_of(j * tq, tq) for j in js]
        sps, parts = [], []
        for z in zs:
            sp = jnp.maximum(z, 0.0) + jnp.log(1.0 + jnp.exp(-jnp.abs(z)))
            if masked:
                sp = jnp.where(causal, sp, 0.0)
            sps.append(sp)
            parts.append(sp.astype(BF16))
        sums = jnp.dot(jnp.concatenate(parts, axis=0), sum_rhs, preferred_element_type=F32)
        ws = []
        for c, (z, sp) in enumerate(zip(zs, sps)):
            w = jnp.exp(z - sp - sums[c * tq:(c + 1) * tq, :tq] - carries[c])
            if masked:
                w = jnp.where(causal, w, 0.0)
            ws.append(w.astype(BF16))
        new_a = [acc + jnp.dot(w, v_ref[pl.ds(st, tq), :], preferred_element_type=F32)
                 for acc, w, st in zip(accs, ws, starts)]
        new_c = [carries[c] + sums[c * tq:(c + 1) * tq, tq:] for c in range(len(js))]
        return new_c, new_a

    def q_group(gi, _):
        qi = [gi * nchain + c for c in range(nchain)]
        qstart = [pl.multiple_of(i * tq, tq) for i in qi]
        qb = [q_ref[pl.ds(st, tq), :] for st in qstart]

        def retire(t, carries):
            carries = [jnp.where(qi[c] - t >= 0, carries[c], jnp.inf) for c in range(nchain)]
            return carries, jnp.min(functools.reduce(jnp.minimum, carries))

        def back(t):
            return [jnp.maximum(qi[c] - t, 0) for c in range(nchain)]

        zs = [scores(qb, back(t)) for t in range(SB_EAGER_STEPS + 1)]
        carries, accs = key_blocks(zs[0], qi, [jnp.zeros((tq, tq), F32)] * nchain,
                                   [jnp.zeros((tq, dh), F32)] * nchain, True)
        carries, low = retire(1, carries)
        for t in range(1, SB_EAGER_STEPS + 1):
            carries, accs = key_blocks(zs[t], back(t), carries, accs, False)
            carries, low = retire(t + 1, carries)

        def cond(st):
            return st[1] < -SB_LOG_WEIGHT_FLOOR

        def body(st):
            t, _, carries, accs = st
            carries, accs = key_blocks(scores(qb, back(t)), back(t), carries, accs, False)
            carries, low = retire(t + 1, carries)
            return t + 1, low, carries, accs

        _, _, _, accs = lax.while_loop(cond, body, (SB_EAGER_STEPS + 1, low, carries, accs))
        for c in range(nchain):
            o_ref[pl.ds(qstart[c], tq), :] = accs[c].astype(o_ref.dtype)
        return 0

    lax.fori_loop(0, ngroups, q_group, 0)


def _sb_attn(p, dims, batch, seq, tq, nchain, cast_weights):
    h, dh = dims.sb_heads, dims.sb_head_dim
    m = batch * seq
    steps = batch * h
    spec = lambda off: pl.BlockSpec((None, seq, dh), lambda b, hh: (off + hh, b, 0))
    slab = lambda w: pl.BlockSpec((w.shape[0] // steps, w.shape[1]), lambda b, hh: (b * h + hh, 0))
    outs = pl.pallas_call(
        functools.partial(_sb_attn_kernel, tq=tq, nchain=nchain, n_cast=len(cast_weights)),
        grid=(batch, h),
        in_specs=[spec(0), spec(h), spec(2 * h)] + [slab(w) for w in cast_weights],
        out_specs=[pl.BlockSpec((seq, dh), lambda b, hh: (b, hh))] + [slab(w) for w in cast_weights],
        out_shape=[jax.ShapeDtypeStruct((m, h * dh), BF16)]
                  + [jax.ShapeDtypeStruct(w.shape, BF16) for w in cast_weights],
        compiler_params=_params("arbitrary", "arbitrary"),
        name="sb_attn",
    )(p, p, p, *cast_weights)
    return outs[0], outs[1:]


def _ssd_kernel(x_ref, b_ref, c_ref, z_ref, dt_ref, wx_ref, wb_ref, wc_ref, bx_ref, bb_ref, bc_ref,
                dtb_ref, alog_ref, dskip_ref, norm_ref, shift_ref, y_ref, halo_scr, state_scr, *, hd, kconv, gps):
    nbatch, l = x_ref.shape[1], LANE
    nxb = x_ref.shape[0] // gps
    hg = dt_ref.shape[1] // gps
    inner = nxb * LANE
    n_state = b_ref.shape[-1]
    width = inner + 2 * n_state
    assert l == LANE and 2 * hd == LANE and n_state == LANE

    @pl.when(pl.program_id(1) == 0)
    def _():
        halo_scr[...] = jnp.zeros_like(halo_scr)
        state_scr[...] = jnp.zeros_like(state_scr)

    rows = functools.partial(_rows, n=l)
    lane = lax.broadcasted_iota(jnp.int32, (l, LANE), 1)
    lane_t = lax.broadcasted_iota(jnp.int32, (hg, l), 1)
    low_half = lane < hd
    tril = lane <= lax.broadcasted_iota(jnp.int32, (l, l), 0)
    pad = jnp.zeros((LANE - hg, l), F32)

    gcols = lambda ref, gi, w: ref[:, gi * w:(gi + 1) * w]
    taps8, taps, bcat, dskip, norm_g, neg_a, dtb = [], [], [], [], [], [], []
    for gi in range(gps):
        wcat = jnp.concatenate([gcols(wx_ref, gi, inner), gcols(wb_ref, gi, n_state), gcols(wc_ref, gi, n_state)],
                               axis=1)
        wcat16 = wcat.astype(BF16)
        taps8.append([jnp.broadcast_to(wcat[k:k + 1, :], (8, width)) for k in range(kconv)])
        taps.append([jnp.tile(jnp.broadcast_to(wcat16[k:k + 1, :], (16, width)), (l // 16, 1))
                     for k in range(kconv)])
        bcat.append(rows(jnp.concatenate([gcols(bx_ref, gi, inner), gcols(bb_ref, gi, n_state),
                                          gcols(bc_ref, gi, n_state)], axis=1)))
        dskip.append(rows(gcols(dskip_ref, gi, inner)))
        norm_g.append(rows(gcols(norm_ref, gi, inner)))
        neg_a.append(-jnp.exp(alog_ref[gi * hg:(gi + 1) * hg, :]))
        dtb.append(dtb_ref[gi * hg:(gi + 1) * hg, :])

    env = (x_ref, b_ref, c_ref, z_ref, dt_ref, shift_ref, y_ref, halo_scr, state_scr, kconv, gps, nbatch, l, nxb, hg,
           inner, n_state, width, rows, lane_t, low_half, tril, pad, taps8, taps, bcat, dskip, norm_g, neg_a, dtb)
    lax.fori_loop(0, x_ref.shape[2] // l, lambda ci, carry: _ssd_chunk(ci, env), 0)


def _ssd_chunk(ci, env):
    (x_ref, b_ref, c_ref, z_ref, dt_ref, shift_ref, y_ref, halo_scr, state_scr, kconv, gps, nbatch, l, nxb, hg,
     inner, n_state, width, rows, lane_t, low_half, tril, pad, taps8, taps, bcat, dskip, norm_g, neg_a, dtb) = env
    rows_t = pl.ds(pl.multiple_of(ci * l, l), l)

    chains = [(gi, b) for gi in range(gps) for b in range(nbatch)]

    acts = []
    for gi, b in chains:
        raw = jnp.concatenate([x_ref[gi * nxb + c, b, rows_t, :] for c in range(nxb)]
                              + [b_ref[gi, b, rows_t, :], c_ref[gi, b, rows_t, :]], axis=1)
        body = jnp.dot(shift_ref[...], jnp.concatenate([raw * taps[gi][k] for k in range(kconv)], axis=0),
                       preferred_element_type=F32)
        head = jnp.concatenate([halo_scr[gi, b], raw[:8].astype(F32)], axis=0)
        halo_scr[gi, b] = raw[l - 8:, :].astype(F32)
        first = jnp.zeros((8, width), F32)
        for k in range(kconv):
            off = 8 - (kconv - 1) + k
            first = first + taps8[gi][k] * head[off:off + 8, :]
        acts.append(_silu(jnp.concatenate([first, body[8:]], axis=0) + bcat[gi]))
    x16 = [a[:, :inner].astype(BF16) for a in acts]
    bms = [a[:, inner:inner + n_state] for a in acts]
    cms = [a[:, inner + n_state:] for a in acts]

    cbs = [lax.dot_general(cm.astype(BF16), bm.astype(BF16), (((1,), (1,)), ((), ())),
                           preferred_element_type=F32) for cm, bm in zip(cms, bms)]
    bm_ts = [bm.T for bm in bms]
    src_rows, coef_rows, keep_rows, cs_cols = [], [], [], []
    for gi, b in chains:
        dt = _softplus(dt_ref[b, gi * hg:(gi + 1) * hg, rows_t] + dtb[gi])
        cs = dt * neg_a[gi]
        sh = 1
        while sh < l:
            cs = cs + jnp.where(lane_t >= sh, pltpu.roll(cs, sh, axis=1), 0.0)
            sh *= 2
        cs_last = jnp.broadcast_to(cs[:, l - 1:l], (hg, l))
        src_rows.append(cs - jnp.log(dt))
        coef_rows.append(dt * jnp.exp(cs_last - cs))
        keep_rows.append(jnp.exp(cs_last))
        cs_cols.append(jnp.concatenate([cs, pad], axis=0).T)

    states = [state_scr[gi, b] for gi, b in chains]
    y_blocks = [[] for _ in chains]
    new_state = [[] for _ in chains]
    for blk in range(nxb):
        for n in range(len(chains)):
            lhs, scaled_bt = [], []
            for h in (2 * blk, 2 * blk + 1):
                ccol = jnp.broadcast_to(cs_cols[n][:, h:h + 1], (l, LANE))
                seg = ccol - rows(src_rows[n][h:h + 1, :])
                w_in = (cbs[n] * jnp.exp(jnp.where(tril, seg, -jnp.inf))).astype(BF16)
                w_prev = (cms[n] * jnp.exp(ccol)).astype(BF16)
                lhs.append(jnp.concatenate([w_in, w_prev], axis=1))
                scaled_bt.append((bm_ts[n] * rows(coef_rows[n][h:h + 1, :])).astype(BF16))
            xblk = x16[n][:, blk * LANE:(blk + 1) * LANE]
            sblk = states[n][:, blk * LANE:(blk + 1) * LANE]
            rhs = jnp.concatenate([xblk, sblk.astype(BF16)], axis=0)
            yy = jnp.dot(jnp.concatenate(lhs, axis=0), rhs, preferred_element_type=F32)
            y_blocks[n].append(jnp.where(low_half, yy[:l], yy[l:]))
            ss = jnp.dot(jnp.concatenate(scaled_bt, axis=0), xblk, preferred_element_type=F32)
            kr = keep_rows[n]
            keep = jnp.where(low_half[:1], kr[2 * blk:2 * blk + 1, :], kr[2 * blk + 1:2 * blk + 2, :])
            new_state[n].append(sblk * rows(keep) + jnp.where(low_half, ss[:n_state], ss[n_state:]))

    for n, (gi, b) in enumerate(chains):
        state_scr[gi, b] = jnp.concatenate(new_state[n], axis=1)
        y = jnp.concatenate(y_blocks[n], axis=1) + acts[n][:, :inner] * dskip[gi]
        zf = jnp.concatenate([z_ref[gi * nxb + c, b, rows_t, :] for c in range(nxb)], axis=1).astype(F32)
        y = y * _silu(zf)
        ms = jnp.mean(y * y, axis=-1, keepdims=True)
        y = y * lax.rsqrt(ms + EPS) * norm_g[gi]
        y_ref[b, rows_t, gi * inner:(gi + 1) * inner] = y.astype(y_ref.dtype)
    return 0


def _ssd(p, dt_t, conv_w, conv_b, dtb, alog, dskip_e, norm_g, dims, batch, seq, chunk, blk_off, gps, cps):
    g = dims.ssd_groups
    inner_g = dims.ssd_inner // g
    nxb = inner_g // LANE
    hg = dims.ssd_heads // g
    n_state = dims.ssd_state
    span = cps * chunk
    nt = seq // span
    kconv = conv_w.shape[0]
    z_off, x_off, b_off, c_off = blk_off
    p4 = p.reshape(p.shape[0], batch, seq, LANE)
    sb_off = dims.ssd_inner // n_state
    assert all(o % (gps * nxb) == 0 for o in (z_off, x_off)) and all(o % gps == 0 for o in (b_off, c_off, sb_off, g))
    shift = np.zeros((chunk, kconv * chunk), np.float32)
    for k in range(kconv):
        t = np.arange(kconv - 1 - k, chunk)
        shift[t, chunk * k + t - (kconv - 1) + k] = 1.0
    shift = jnp.asarray(shift, BF16)
    in_specs = [
        pl.BlockSpec((gps * nxb, batch, span, LANE), lambda gg, t: (x_off // (gps * nxb) + gg, 0, t, 0)),
        pl.BlockSpec((gps, batch, span, LANE), lambda gg, t: (b_off // gps + gg, 0, t, 0)),
        pl.BlockSpec((gps, batch, span, LANE), lambda gg, t: (c_off // gps + gg, 0, t, 0)),
        pl.BlockSpec((gps * nxb, batch, span, LANE), lambda gg, t: (z_off // (gps * nxb) + gg, 0, t, 0)),
        pl.BlockSpec((batch, gps * hg, span), lambda gg, t: (0, gg, t)),
        pl.BlockSpec((kconv, gps * inner_g), lambda gg, t: (0, gg)),
        pl.BlockSpec((kconv, gps * n_state), lambda gg, t: (0, sb_off // gps + gg)),
        pl.BlockSpec((kconv, gps * n_state), lambda gg, t: (0, (sb_off + g) // gps + gg)),
        pl.BlockSpec((1, gps * inner_g), lambda gg, t: (0, gg)),
        pl.BlockSpec((1, gps * n_state), lambda gg, t: (0, sb_off // gps + gg)),
        pl.BlockSpec((1, gps * n_state), lambda gg, t: (0, (sb_off + g) // gps + gg)),
        pl.BlockSpec((gps * hg, LANE), lambda gg, t: (gg, 0)),
        pl.BlockSpec((gps * hg, LANE), lambda gg, t: (gg, 0)),
        pl.BlockSpec((1, gps * inner_g), lambda gg, t: (0, gg)),
        pl.BlockSpec((1, gps * inner_g), lambda gg, t: (0, gg)),
        pl.BlockSpec(shift.shape, lambda gg, t: (0, 0)),
    ]
    y = pl.pallas_call(
        functools.partial(_ssd_kernel, hd=dims.ssd_head_dim, kconv=kconv, gps=gps),
        grid=(g // gps, nt),
        in_specs=in_specs,
        out_specs=pl.BlockSpec((batch, span, gps * inner_g), lambda gg, t: (0, t, gg)),
        out_shape=jax.ShapeDtypeStruct((batch, seq, dims.ssd_inner), BF16),
        scratch_shapes=[pltpu.VMEM((gps, batch, 8, inner_g + 2 * n_state), F32),
                        pltpu.VMEM((gps, batch, n_state, inner_g), F32)],
        compiler_params=_params("parallel", "arbitrary"),
        name="ssd",
    )(p4, p4, p4, p4, dt_t, conv_w, conv_w, conv_w, conv_b, conv_b, conv_b, dtb, alog, dskip_e, norm_g, shift)
    return y.reshape(batch * seq, dims.ssd_inner)


def _merge_kernel(o_ref, y_ref, gsb_ref, gssd_ref, wsb_ref, wssd_ref, out_ref):
    a = jnp.dot(o_ref[...], wsb_ref[...], preferred_element_type=F32)
    b = jnp.dot(y_ref[...], wssd_ref[...], preferred_element_type=F32)
    gsb = jnp.concatenate([gsb_ref[c] for c in range(gsb_ref.shape[0])], axis=1).astype(F32)
    gssd = jnp.concatenate([gssd_ref[c] for c in range(gssd_ref.shape[0])], axis=1).astype(F32)
    out_ref[...] = (jax.nn.sigmoid(gsb) * a + jax.nn.sigmoid(gssd) * b).astype(out_ref.dtype)


def _merge(o_sb, y, p, w_sb, w_ssd, gsb_off, gssd_off, tm, tn):
    m, ko = o_sb.shape
    ky = y.shape[1]
    d = w_sb.shape[1]
    nb = tn // LANE
    return pl.pallas_call(
        _merge_kernel,
        grid=(m // tm, d // tn),
        in_specs=[
            pl.BlockSpec((tm, ko), lambda i, j: (i, 0)),
            pl.BlockSpec((tm, ky), lambda i, j: (i, 0)),
            pl.BlockSpec((nb, tm, LANE), lambda i, j: (gsb_off // nb + j, i, 0)),
            pl.BlockSpec((nb, tm, LANE), lambda i, j: (gssd_off // nb + j, i, 0)),
            pl.BlockSpec((ko, tn), lambda i, j: (0, j), pipeline_mode=pl.Buffered(1 if tn == d else 2)),
            pl.BlockSpec((ky, tn), lambda i, j: (0, j), pipeline_mode=pl.Buffered(1 if tn == d else 2)),
        ],
        out_specs=pl.BlockSpec((tm, tn), lambda i, j: (i, j)),
        out_shape=jax.ShapeDtypeStruct((m, d), BF16),
        compiler_params=_params("parallel", "arbitrary"),
        name="merge",
    )(o_sb, y, p, p, w_sb, w_ssd)


def _proj_norm_res_kernel(a_ref, w_ref, g_ref, x_ref, out_ref, f_scr):
    nj, tm, tn = f_scr.shape
    j = pl.program_id(1)
    f_scr[j] = jnp.dot(a_ref[...], w_ref[...], preferred_element_type=F32)

    @pl.when(j == nj - 1)
    def _():
        ss = jnp.zeros((tm, 1), F32)
        for c in range(nj):
            f = f_scr[c]
            ss = ss + jnp.sum(f * f, axis=-1, keepdims=True)
        inv = lax.rsqrt(ss * (1.0 / (nj * tn)) + EPS)
        for c in range(nj):
            cols = slice(c * tn, (c + 1) * tn)
            out_ref[:, cols] = x_ref[:, cols] + f_scr[c] * inv * _rows(g_ref[:, cols], tm)


def _proj_norm_res(a, w, gain, x2, tm, tn, name):
    m, kdim = a.shape
    d = w.shape[1]
    return pl.pallas_call(
        _proj_norm_res_kernel,
        grid=(m // tm, d // tn),
        in_specs=[
            pl.BlockSpec((tm, kdim), lambda i, j: (i, 0)),
            pl.BlockSpec((kdim, tn), lambda i, j: (0, j), pipeline_mode=pl.Buffered(1 if tn == d else 2)),
            pl.BlockSpec((1, d), lambda i, j: (0, 0)),
            pl.BlockSpec((tm, d), lambda i, j: (i, 0)),
        ],
        out_specs=pl.BlockSpec((tm, d), lambda i, j: (i, 0)),
        out_shape=jax.ShapeDtypeStruct((m, d), F32),
        scratch_shapes=[pltpu.VMEM((d // tn, tm, tn), F32)],
        compiler_params=_params("parallel", "arbitrary"),
        name=name,
    )(a, w, gain, x2)


FFN_HALO = 16


def _ffn_up_kernel(x_ref, halo_ref, g_ref, wg_ref, wv_ref, cwg_ref, cwv_ref, cbg_ref, cbv_ref, out_ref, h_scr,
                   *, tiles_per_seq, kconv):
    tm = x_ref.shape[0]

    def norm(v):
        ms = jnp.mean(v * v, axis=-1, keepdims=True)
        return (v * lax.rsqrt(ms + EPS) * g_ref[...]).astype(BF16)

    @pl.when(pl.program_id(1) == 0)
    def _():
        seq_start = (pl.program_id(0) % tiles_per_seq) == 0
        h_scr[:FFN_HALO, :] = jnp.where(seq_start, jnp.zeros((), BF16), norm(halo_ref[...]))
        h_scr[FFN_HALO:, :] = norm(x_ref[...])

    h = h_scr[...]

    def conv(w_ref, cw_ref, cb_ref):
        up = jnp.dot(h, w_ref[...], preferred_element_type=F32)
        u = _rows(cb_ref[...], tm)
        for k in range(kconv):
            off = FFN_HALO - (kconv - 1) + k
            u = u + _rows(cw_ref[k:k + 1, :], tm) * up[off:off + tm, :]
        return u

    gate = conv(wg_ref, cwg_ref, cbg_ref)
    val = conv(wv_ref, cwv_ref, cbv_ref)
    c = math.sqrt(2.0 / math.pi)
    half = 0.5 * gate
    act = half + half * jnp.tanh(gate * (c + (c * 0.044715) * (gate * gate)))
    out_ref[...] = (act * val).astype(out_ref.dtype)


def _ffn_up(x1, gain, w_up, conv_w, conv_b, seq, tm, tn):
    m, d = x1.shape
    dff = w_up.shape[1] // 2
    nj = dff // tn
    kconv = conv_w.shape[0]
    hb = tm // FFN_HALO
    return pl.pallas_call(
        functools.partial(_ffn_up_kernel, tiles_per_seq=seq // tm, kconv=kconv),
        grid=(m // tm, nj),
        in_specs=[
            pl.BlockSpec((tm, d), lambda i, j: (i, 0)),
            pl.BlockSpec((FFN_HALO, d), lambda i, j: (jnp.maximum(i * hb - 1, 0), 0)),
            pl.BlockSpec((1, d), lambda i, j: (0, 0)),
            pl.BlockSpec((d, tn), lambda i, j: (0, j)),
            pl.BlockSpec((d, tn), lambda i, j: (0, nj + j)),
            pl.BlockSpec((kconv, tn), lambda i, j: (0, j)),
            pl.BlockSpec((kconv, tn), lambda i, j: (0, nj + j)),
            pl.BlockSpec((1, tn), lambda i, j: (0, j)),
            pl.BlockSpec((1, tn), lambda i, j: (0, nj + j)),
        ],
        out_specs=pl.BlockSpec((tm, tn), lambda i, j: (i, j)),
        out_shape=jax.ShapeDtypeStruct((m, dff), BF16),
        scratch_shapes=[pltpu.VMEM((FFN_HALO + tm, d), BF16)],
        compiler_params=_params("parallel", "arbitrary"),
        name="ffn_up",
    )(x1, x1, gain, w_up, w_up, conv_w, conv_w, conv_b, conv_b)


def _largest_tile(total, want, quantum=LANE):
    t = min(want, total)
    while total % t or t % quantum:
        t -= quantum
    return t


def _block(x, norm_mix_pre, w_in, ssd_conv_w, ssd_conv_b, dt_bias, a_log, d_skip, ssd_norm, w_sb_proj,
           w_ssd_proj, w_out, norm_mix_post, norm_ffn_pre, w_up, ffn_conv_w, ffn_conv_b, w_down, norm_ffn_post,
           dims, tiles):
    batch, seq, d = x.shape
    m = batch * seq
    sbw, inner, xbc, heads = dims.sb_width, dims.ssd_inner, dims.ssd_xbc, dims.ssd_heads
    x2 = x.reshape(m, d)
    row = lambda v: v.reshape(1, -1)

    dt_lo = 3 * sbw + inner + xbc
    w_in_t = w_in.T
    n_all = w_in_t.shape[0] - heads
    row_scale = jnp.where(jnp.arange(w_in_t.shape[0]) < sbw, dims.sb_head_dim ** -0.5, 1.0).astype(F32)[:, None]
    w_all = (w_in_t * row_scale).astype(BF16)
    w_dt = w_in_t[dt_lo:dt_lo + heads]
    z_off = 3 * sbw // LANE
    x_off = z_off + inner // LANE
    b_off = x_off + inner // LANE
    c_off = b_off + dims.ssd_groups * dims.ssd_state // LANE
    gsb_off = c_off + dims.ssd_groups * dims.ssd_state // LANE
    gssd_off = gsb_off + d // LANE

    tm = _largest_tile(seq, tiles["tm"], 16)
    p, dt_t = _in_proj(x2, row(norm_mix_pre), w_all, w_dt, heads, seq, tm,
                       _largest_tile(math.gcd(n_all, dt_lo), tiles["tn_in"]), dt_lo, heads)

    o_sb, (w_sb16, w_ssd16, w_out16, w_up16, w_down16) = _sb_attn(
        p, dims, batch, seq, min(tiles["tq"], seq), tiles["sb_chains"],
        [w_sb_proj, w_ssd_proj, w_out, w_up, w_down])

    chunk = min(tiles["chunk"], seq)
    bcast = lambda v: jnp.broadcast_to(v.reshape(-1, 1), (heads, LANE))
    y = _ssd(p, dt_t, ssd_conv_w, row(ssd_conv_b), bcast(dt_bias), bcast(a_log),
             row(jnp.repeat(d_skip, dims.ssd_head_dim)), row(ssd_norm), dims, batch, seq, chunk,
             (z_off, x_off, b_off, c_off), tiles["ssd_groups_per_step"], tiles["ssd_chunks_per_step"])

    merged = _merge(o_sb, y, p, w_sb16, w_ssd16, gsb_off, gssd_off,
                    _largest_tile(m, tiles["tm_merge"], 16), _largest_tile(d, tiles["tn_merge"]))
    tm2 = _largest_tile(m, tiles["tm_res"], 16)
    x1 = _proj_norm_res(merged, w_out16, row(norm_mix_post), x2, tm2, d, "out_proj")

    tm_up = _largest_tile(seq, tiles["tm"], 16)
    act = _ffn_up(x1, row(norm_ffn_pre), w_up16, ffn_conv_w, row(ffn_conv_b), seq, tm_up,
                  _largest_tile(dims.d_ff, tiles["tn_up"]))
    out = _proj_norm_res(act, w_down16, row(norm_ffn_post), x1, tm2,
                         _largest_tile(d, tiles["tn_down"]), "ffn_down")
    return out.reshape(batch, seq, d)


TILES = dict(tm=1024, tn_in=2048, tq=128, sb_chains=16, chunk=128, ssd_groups_per_step=4, ssd_chunks_per_step=4, tm_merge=512, tn_merge=2048, tm_res=512, tn_up=512, tn_down=2048)


def kernel(x, norm_mix_pre, w_in, ssd_conv_w, ssd_conv_b, dt_bias, a_log, d_skip, ssd_norm, w_sb_proj, w_ssd_proj,
           w_out, norm_mix_post, norm_ffn_pre, w_up, ffn_conv_w, ffn_conv_b, w_down, norm_ffn_post):
    dims = Dims()
    args = (norm_mix_pre, w_in, ssd_conv_w, ssd_conv_b, dt_bias, a_log, d_skip, ssd_norm, w_sb_proj, w_ssd_proj,
            w_out, norm_mix_post, norm_ffn_pre, w_up, ffn_conv_w, ffn_conv_b, w_down, norm_ffn_post)
    for layer in range(w_in.shape[0]):
        x = _block(x, *(a[layer] for a in args), dims, TILES)
    return x
```

```python
import dataclasses
import functools
import math

import numpy as np

import jax
import jax.numpy as jnp
from jax import lax
from jax.experimental import pallas as pl
from jax.experimental.pallas import tpu as pltpu

F32 = jnp.float32
BF16 = jnp.bfloat16
LANE = 128
EPS = 1e-6


@dataclasses.dataclass(frozen=True)
class Dims:
    d_model: int = 2048
    sb_heads: int = 16
    sb_head_dim: int = 128
    ssd_inner: int = 4096
    ssd_head_dim: int = 64
    ssd_groups: int = 8
    ssd_state: int = 128
    d_ff: int = 5632

    @property
    def sb_width(self):
        return self.sb_heads * self.sb_head_dim

    @property
    def ssd_heads(self):
        return self.ssd_inner // self.ssd_head_dim

    @property
    def ssd_xbc(self):
        return self.ssd_inner + 2 * self.ssd_groups * self.ssd_state


SB_LOG_WEIGHT_FLOOR = -110.0
SB_EAGER_STEPS = 2
VMEM_LIMIT = 60 * 1024 * 1024


def _softplus(x):
    return jnp.maximum(x, 0.0) + jnp.log1p(jnp.exp(-jnp.abs(x)))


def _rows(v, n):
    return jnp.tile(jnp.broadcast_to(v, (8, v.shape[1])), (n // 8, 1))


def _silu(x):
    h = 0.5 * x
    return h + h * jnp.tanh(h)


def _params(*sem):
    return pltpu.CompilerParams(dimension_semantics=sem, vmem_limit_bytes=VMEM_LIMIT)


def _in_proj_kernel(x_ref, g_ref, w_ref, wdt_ref, p_ref, dt_ref, h_scr):
    j = pl.program_id(1)

    @pl.when(j == 0)
    def _():
        x = x_ref[...]
        ms = jnp.mean(x * x, axis=-1, keepdims=True)
        h = (x * lax.rsqrt(ms + EPS) * g_ref[...]).astype(BF16)
        h_scr[...] = h
        dt_ref[...] = lax.dot_general(wdt_ref[...].astype(BF16), h, (((1,), (1,)), ((), ())),
                                      preferred_element_type=F32)

    acc = lax.dot_general(h_scr[...], w_ref[...], (((1,), (1,)), ((), ())), preferred_element_type=F32)
    for c in range(p_ref.shape[0]):
        p_ref[c] = acc[:, c * LANE:(c + 1) * LANE].astype(BF16)


def _in_proj(x2, gain, w_all, w_dt, hd, seq, tm, tn, skip_at, skip):
    m, d = x2.shape
    n = w_all.shape[0] - skip
    w_row = lambda j: pl.multiple_of(j * tn + jnp.where(j * tn >= skip_at, skip, 0), 16)
    tps = seq // tm
    return pl.pallas_call(
        _in_proj_kernel,
        grid=(m // tm, n // tn),
        in_specs=[
            pl.BlockSpec((tm, d), lambda i, j: (i, 0)),
            pl.BlockSpec((1, d), lambda i, j: (0, 0)),
            pl.BlockSpec((pl.Element(tn), pl.Element(d)), lambda i, j: (w_row(j), 0)),
            pl.BlockSpec((hd, d), lambda i, j: (0, 0)),
        ],
        out_specs=[
            pl.BlockSpec((tn // LANE, tm, LANE), lambda i, j: (j, i, 0)),
            pl.BlockSpec((None, hd, tm), lambda i, j: (i // tps, 0, i % tps)),
        ],
        out_shape=[
            jax.ShapeDtypeStruct((n // LANE, m, LANE), BF16),
            jax.ShapeDtypeStruct((m // seq, hd, seq), F32),
        ],
        scratch_shapes=[pltpu.VMEM((tm, d), BF16)],
        compiler_params=_params("parallel", "arbitrary"),
        name="in_proj",
    )(x2, gain, w_all, w_dt)


def _sb_attn_kernel(q_ref, k_ref, v_ref, *rest, tq, nchain, n_cast):
    o_ref = rest[n_cast]
    for src, dst in zip(rest[:n_cast], rest[n_cast + 1:]):
        dst[...] = src[...].astype(dst.dtype)
    s, dh = q_ref.shape
    ngroups = s // (tq * nchain)
    row = lax.broadcasted_iota(jnp.int32, (tq, tq), 0)
    col = lax.broadcasted_iota(jnp.int32, (tq, tq), 1)
    causal = col < row
    suffix = jnp.where(row > col, 1.0, 0.0).astype(BF16)
    sum_rhs = jnp.concatenate([suffix, jnp.ones((tq, tq), BF16)], axis=1)

    def scores(qbs, js):
        return [lax.dot_general(qb, k_ref[pl.ds(pl.multiple_of(j * tq, tq), tq), :], (((1,), (1,)), ((), ())),
                                preferred_element_type=F32) for qb, j in zip(qbs, js)]

    def key_blocks(zs, js, carries, accs, masked):
        starts = [pl.multiple_of(j * tq, tq) for j in js]
        sps, parts = [], []
        for z in zs:
            sp = jnp.maximum(z, 0.0) + jnp.log(1.0 + jnp.exp(-jnp.abs(z)))
            if masked:
                sp = jnp.where(causal, sp, 0.0)
            sps.append(sp)
            parts.append(sp.astype(BF16))
        sums = jnp.dot(jnp.concatenate(parts, axis=0), sum_rhs, preferred_element_type=F32)
        ws = []
        for c, (z, sp) in enumerate(zip(zs, sps)):
            w = jnp.exp(z - sp - sums[c * tq:(c + 1) * tq, :tq] - carries[c])
            if masked:
                w = jnp.where(causal, w, 0.0)
            ws.append(w.astype(BF16))
        new_a = [acc + jnp.dot(w, v_ref[pl.ds(st, tq), :], preferred_element_type=F32)
                 for acc, w, st in zip(accs, ws, starts)]
        new_c = [carries[c] + sums[c * tq:(c + 1) * tq, tq:] for c in range(len(js))]
        return new_c, new_a

    def q_group(gi, _):
        qi = [gi * nchain + c for c in range(nchain)]
        qstart = [pl.multiple_of(i * tq, tq) for i in qi]
        qb = [q_ref[pl.ds(st, tq), :] for st in qstart]

        def retire(t, carries):
            carries = [jnp.where(qi[c] - t >= 0, carries[c], jnp.inf) for c in range(nchain)]
            return carries, jnp.min(functools.reduce(jnp.minimum, carries))

        def back(t):
            return [jnp.maximum(qi[c] - t, 0) for c in range(nchain)]

        z_now = scores(qb, back(0))
        z_next = scores(qb, back(1))
        carries, accs = key_blocks(z_now, qi, [jnp.zeros((tq, tq), F32)] * nchain,
                                   [jnp.zeros((tq, dh), F32)] * nchain, True)
        carries, low = retire(1, carries)
        for t in range(1, SB_EAGER_STEPS + 1):
            z_now = z_next
            if t < SB_EAGER_STEPS:
                z_next = scores(qb, back(t + 1))
            carries, accs = key_blocks(z_now, back(t), carries, accs, False)
            carries, low = retire(t + 1, carries)

        def cond(st):
            return st[1] < -SB_LOG_WEIGHT_FLOOR

        def body(st):
            t, _, carries, accs = st
            carries, accs = key_blocks(scores(qb, back(t)), back(t), carries, accs, False)
            carries, low = retire(t + 1, carries)
            return t + 1, low, carries, accs

        _, _, _, accs = lax.while_loop(cond, body, (SB_EAGER_STEPS + 1, low, carries, accs))
        for c in range(nchain):
            o_ref[pl.ds(qstart[c], tq), :] = accs[c].astype(o_ref.dtype)
        return 0

    lax.fori_loop(0, ngroups, q_group, 0)


def _sb_attn(p, dims, batch, seq, tq, nchain, cast_weights):
    h, dh = dims.sb_heads, dims.sb_head_dim
    m = batch * seq
    steps = batch * h
    spec = lambda off: pl.BlockSpec((None, seq, dh), lambda b, hh: (off + hh, b, 0))
    slab = lambda w: pl.BlockSpec((w.shape[0] // steps, w.shape[1]), lambda b, hh: (b * h + hh, 0))
    outs = pl.pallas_call(
        functools.partial(_sb_attn_kernel, tq=tq, nchain=nchain, n_cast=len(cast_weights)),
        grid=(batch, h),
        in_specs=[spec(0), spec(h), spec(2 * h)] + [slab(w) for w in cast_weights],
        out_specs=[pl.BlockSpec((seq, dh), lambda b, hh: (b, hh))] + [slab(w) for w in cast_weights],
        out_shape=[jax.ShapeDtypeStruct((m, h * dh), BF16)]
                  + [jax.ShapeDtypeStruct(w.shape, BF16) for w in cast_weights],
        compiler_params=_params("arbitrary", "arbitrary"),
        name="sb_attn",
    )(p, p, p, *cast_weights)
    return outs[0], outs[1:]


def _ssd_kernel(x_ref, b_ref, c_ref, z_ref, dt_ref, wx_ref, wb_ref, wc_ref, bx_ref, bb_ref, bc_ref,
                dtb_ref, alog_ref, dskip_ref, norm_ref, shift_ref, y_ref, halo_scr, state_scr, *, hd, kconv, gps):
    nbatch, l = x_ref.shape[1], LANE
    nxb = x_ref.shape[0] // gps
    hg = dt_ref.shape[1] // gps
    inner = nxb * LANE
    n_state = b_ref.shape[-1]
    width = inner + 2 * n_state
    assert l == LANE and 2 * hd == LANE and n_state == LANE

    @pl.when(pl.program_id(1) == 0)
    def _():
        halo_scr[...] = jnp.zeros_like(halo_scr)
        state_scr[...] = jnp.zeros_like(state_scr)

    rows = functools.partial(_rows, n=l)
    lane = lax.broadcasted_iota(jnp.int32, (l, LANE), 1)
    lane_t = lax.broadcasted_iota(jnp.int32, (hg, l), 1)
    low_half = lane < hd
    tril = lane <= lax.broadcasted_iota(jnp.int32, (l, l), 0)
    pad = jnp.zeros((LANE - hg, l), F32)

    gcols = lambda ref, gi, w: ref[:, gi * w:(gi + 1) * w]
    taps8, taps, bcat, dskip, norm_g, neg_a, dtb = [], [], [], [], [], [], []
    for gi in range(gps):
        wcat = jnp.concatenate([gcols(wx_ref, gi, inner), gcols(wb_ref, gi, n_state), gcols(wc_ref, gi, n_state)],
                               axis=1)
        wcat16 = wcat.astype(BF16)
        taps8.append([jnp.broadcast_to(wcat[k:k + 1, :], (8, width)) for k in range(kconv)])
        taps.append([jnp.tile(jnp.broadcast_to(wcat16[k:k + 1, :], (16, width)), (l // 16, 1))
                     for k in range(kconv)])
        bcat.append(rows(jnp.concatenate([gcols(bx_ref, gi, inner), gcols(bb_ref, gi, n_state),
                                          gcols(bc_ref, gi, n_state)], axis=1)))
        dskip.append(rows(gcols(dskip_ref, gi, inner)))
        norm_g.append(rows(gcols(norm_ref, gi, inner)))
        neg_a.append(-jnp.exp(alog_ref[gi * hg:(gi + 1) * hg, :]))
        dtb.append(dtb_ref[gi * hg:(gi + 1) * hg, :])

    env = (x_ref, b_ref, c_ref, z_ref, dt_ref, shift_ref, y_ref, halo_scr, state_scr, kconv, gps, nbatch, l, nxb, hg,
           inner, n_state, width, rows, lane_t, low_half, tril, pad, taps8, taps, bcat, dskip, norm_g, neg_a, dtb)
    lax.fori_loop(0, x_ref.shape[2] // l, lambda ci, carry: _ssd_chunk(ci, env), 0)


def _ssd_chunk(ci, env):
    (x_ref, b_ref, c_ref, z_ref, dt_ref, shift_ref, y_ref, halo_scr, state_scr, kconv, gps, nbatch, l, nxb, hg,
     inner, n_state, width, rows, lane_t, low_half, tril, pad, taps8, taps, bcat, dskip, norm_g, neg_a, dtb) = env
    rows_t = pl.ds(pl.multiple_of(ci * l, l), l)

    chains = [(gi, b) for gi in range(gps) for b in range(nbatch)]

    acts = []
    for gi, b in chains:
        raw = jnp.concatenate([x_ref[gi * nxb + c, b, rows_t, :] for c in range(nxb)]
                              + [b_ref[gi, b, rows_t, :], c_ref[gi, b, rows_t, :]], axis=1)
        body = jnp.dot(shift_ref[...], jnp.concatenate([raw * taps[gi][k] for k in range(kconv)], axis=0),
                       preferred_element_type=F32)
        head = jnp.concatenate([halo_scr[gi, b], raw[:8].astype(F32)], axis=0)
        halo_scr[gi, b] = raw[l - 8:, :].astype(F32)
        first = jnp.zeros((8, width), F32)
        for k in range(kconv):
            off = 8 - (kconv - 1) + k
            first = first + taps8[gi][k] * head[off:off + 8, :]
        acts.append(_silu(jnp.concatenate([first, body[8:]], axis=0) + bcat[gi]))
    x16 = [a[:, :inner].astype(BF16) for a in acts]
    bms = [a[:, inner:inner + n_state] for a in acts]
    cms = [a[:, inner + n_state:] for a in acts]

    cbs = [lax.dot_general(cm.astype(BF16), bm.astype(BF16), (((1,), (1,)), ((), ())),
                           preferred_element_type=F32) for cm, bm in zip(cms, bms)]
    bm_ts = [bm.T for bm in bms]
    src_rows, coef_rows, keep_rows, cs_cols = [], [], [], []
    for gi, b in chains:
        dt = _softplus(dt_ref[b, gi * hg:(gi + 1) * hg, rows_t] + dtb[gi])
        cs = dt * neg_a[gi]
        sh = 1
        while sh < l:
            cs = cs + jnp.where(lane_t >= sh, pltpu.roll(cs, sh, axis=1), 0.0)
            sh *= 2
        cs_last = jnp.broadcast_to(cs[:, l - 1:l], (hg, l))
        src_rows.append(cs - jnp.log(dt))
        coef_rows.append(dt * jnp.exp(cs_last - cs))
        keep_rows.append(jnp.exp(cs_last))
        cs_cols.append(jnp.concatenate([cs, pad], axis=0).T)

    states = [state_scr[gi, b] for gi, b in chains]
    y_blocks = [[] for _ in chains]
    new_state = [[] for _ in chains]
    for blk in range(nxb):
        for n in range(len(chains)):
            lhs, scaled_bt = [], []
            for h in (2 * blk, 2 * blk + 1):
                ccol = jnp.broadcast_to(cs_cols[n][:, h:h + 1], (l, LANE))
                seg = ccol - rows(src_rows[n][h:h + 1, :])
                w_in = (cbs[n] * jnp.exp(jnp.where(tril, seg, -jnp.inf))).astype(BF16)
                w_prev = (cms[n] * jnp.exp(ccol)).astype(BF16)
                lhs.append(jnp.concatenate([w_in, w_prev], axis=1))
                scaled_bt.append((bm_ts[n] * rows(coef_rows[n][h:h + 1, :])).astype(BF16))
            xblk = x16[n][:, blk * LANE:(blk + 1) * LANE]
            sblk = states[n][:, blk * LANE:(blk + 1) * LANE]
            rhs = jnp.concatenate([xblk, sblk.astype(BF16)], axis=0)
            yy = jnp.dot(jnp.concatenate(lhs, axis=0), rhs, preferred_element_type=F32)
            y_blocks[n].append(jnp.where(low_half, yy[:l], yy[l:]))
            ss = jnp.dot(jnp.concatenate(scaled_bt, axis=0), xblk, preferred_element_type=F32)
            kr = keep_rows[n]
            keep = jnp.where(low_half[:1], kr[2 * blk:2 * blk + 1, :], kr[2 * blk + 1:2 * blk + 2, :])
            new_state[n].append(sblk * rows(keep) + jnp.where(low_half, ss[:n_state], ss[n_state:]))

    for n, (gi, b) in enumerate(chains):
        state_scr[gi, b] = jnp.concatenate(new_state[n], axis=1)
        y = jnp.concatenate(y_blocks[n], axis=1) + acts[n][:, :inner] * dskip[gi]
        zf = jnp.concatenate([z_ref[gi * nxb + c, b, rows_t, :] for c in range(nxb)], axis=1).astype(F32)
        y = y * _silu(zf)
        ms = jnp.mean(y * y, axis=-1, keepdims=True)
        y = y * lax.rsqrt(ms + EPS) * norm_g[gi]
        y_ref[b, rows_t, gi * inner:(gi + 1) * inner] = y.astype(y_ref.dtype)
    return 0


def _ssd(p, dt_t, conv_w, conv_b, dtb, alog, dskip_e, norm_g, dims, batch, seq, chunk, blk_off, gps, cps):
    g = dims.ssd_groups
    inner_g = dims.ssd_inner // g
    nxb = inner_g // LANE
    hg = dims.ssd_heads // g
    n_state = dims.ssd_state
    span = cps * chunk
    nt = seq // span
    kconv = conv_w.shape[0]
    z_off, x_off, b_off, c_off = blk_off
    p4 = p.reshape(p.shape[0], batch, seq, LANE)
    sb_off = dims.ssd_inner // n_state
    assert all(o % (gps * nxb) == 0 for o in (z_off, x_off)) and all(o % gps == 0 for o in (b_off, c_off, sb_off, g))
    shift = np.zeros((chunk, kconv * chunk), np.float32)
    for k in range(kconv):
        t = np.arange(kconv - 1 - k, chunk)
        shift[t, chunk * k + t - (kconv - 1) + k] = 1.0
    shift = jnp.asarray(shift, BF16)
    in_specs = [
        pl.BlockSpec((gps * nxb, batch, span, LANE), lambda gg, t: (x_off // (gps * nxb) + gg, 0, t, 0)),
        pl.BlockSpec((gps, batch, span, LANE), lambda gg, t: (b_off // gps + gg, 0, t, 0)),
        pl.BlockSpec((gps, batch, span, LANE), lambda gg, t: (c_off // gps + gg, 0, t, 0)),
        pl.BlockSpec((gps * nxb, batch, span, LANE), lambda gg, t: (z_off // (gps * nxb) + gg, 0, t, 0)),
        pl.BlockSpec((batch, gps * hg, span), lambda gg, t: (0, gg, t)),
        pl.BlockSpec((kconv, gps * inner_g), lambda gg, t: (0, gg)),
        pl.BlockSpec((kconv, gps * n_state), lambda gg, t: (0, sb_off // gps + gg)),
        pl.BlockSpec((kconv, gps * n_state), lambda gg, t: (0, (sb_off + g) // gps + gg)),
        pl.BlockSpec((1, gps * inner_g), lambda gg, t: (0, gg)),
        pl.BlockSpec((1, gps * n_state), lambda gg, t: (0, sb_off // gps + gg)),
        pl.BlockSpec((1, gps * n_state), lambda gg, t: (0, (sb_off + g) // gps + gg)),
        pl.BlockSpec((gps * hg, LANE), lambda gg, t: (gg, 0)),
        pl.BlockSpec((gps * hg, LANE), lambda gg, t: (gg, 0)),
        pl.BlockSpec((1, gps * inner_g), lambda gg, t: (0, gg)),
        pl.BlockSpec((1, gps * inner_g), lambda gg, t: (0, gg)),
        pl.BlockSpec(shift.shape, lambda gg, t: (0, 0)),
    ]
    y = pl.pallas_call(
        functools.partial(_ssd_kernel, hd=dims.ssd_head_dim, kconv=kconv, gps=gps),
        grid=(g // gps, nt),
        in_specs=in_specs,
        out_specs=pl.BlockSpec((batch, span, gps * inner_g), lambda gg, t: (0, t, gg)),
        out_shape=jax.ShapeDtypeStruct((batch, seq, dims.ssd_inner), BF16),
        scratch_shapes=[pltpu.VMEM((gps, batch, 8, inner_g + 2 * n_state), F32),
                        pltpu.VMEM((gps, batch, n_state, inner_g), F32)],
        compiler_params=_params("parallel", "arbitrary"),
        name="ssd",
    )(p4, p4, p4, p4, dt_t, conv_w, conv_w, conv_w, conv_b, conv_b, conv_b, dtb, alog, dskip_e, norm_g, shift)
    return y.reshape(batch * seq, dims.ssd_inner)


def _merge_kernel(o_ref, y_ref, gsb_ref, gssd_ref, wsb_ref, wssd_ref, out_ref):
    a = jnp.dot(o_ref[...], wsb_ref[...], preferred_element_type=F32)
    b = jnp.dot(y_ref[...], wssd_ref[...], preferred_element_type=F32)
    gsb = jnp.concatenate([gsb_ref[c] for c in range(gsb_ref.shape[0])], axis=1).astype(F32)
    gssd = jnp.concatenate([gssd_ref[c] for c in range(gssd_ref.shape[0])], axis=1).astype(F32)
    out_ref[...] = (jax.nn.sigmoid(gsb) * a + jax.nn.sigmoid(gssd) * b).astype(out_ref.dtype)


def _merge(o_sb, y, p, w_sb, w_ssd, gsb_off, gssd_off, tm, tn):
    m, ko = o_sb.shape
    ky = y.shape[1]
    d = w_sb.shape[1]
    nb = tn // LANE
    return pl.pallas_call(
        _merge_kernel,
        grid=(m // tm, d // tn),
        in_specs=[
            pl.BlockSpec((tm, ko), lambda i, j: (i, 0)),
            pl.BlockSpec((tm, ky), lambda i, j: (i, 0)),
            pl.BlockSpec((nb, tm, LANE), lambda i, j: (gsb_off // nb + j, i, 0)),
            pl.BlockSpec((nb, tm, LANE), lambda i, j: (gssd_off // nb + j, i, 0)),
            pl.BlockSpec((ko, tn), lambda i, j: (0, j), pipeline_mode=pl.Buffered(1 if tn == d else 2)),
            pl.BlockSpec((ky, tn), lambda i, j: (0, j), pipeline_mode=pl.Buffered(1 if tn == d else 2)),
        ],
        out_specs=pl.BlockSpec((tm, tn), lambda i, j: (i, j)),
        out_shape=jax.ShapeDtypeStruct((m, d), BF16),
        compiler_params=_params("parallel", "arbitrary"),
        name="merge",
    )(o_sb, y, p, p, w_sb, w_ssd)


def _proj_norm_res_kernel(a_ref, w_ref, g_ref, x_ref, out_ref, f_scr):
    nj, tm, tn = f_scr.shape
    j = pl.program_id(1)
    f_scr[j] = jnp.dot(a_ref[...], w_ref[...], preferred_element_type=F32)

    @pl.when(j == nj - 1)
    def _():
        ss = jnp.zeros((tm, 1), F32)
        for c in range(nj):
            f = f_scr[c]
            ss = ss + jnp.sum(f * f, axis=-1, keepdims=True)
        inv = lax.rsqrt(ss * (1.0 / (nj * tn)) + EPS)
        for c in range(nj):
            cols = slice(c * tn, (c + 1) * tn)
            out_ref[:, cols] = x_ref[:, cols] + f_scr[c] * inv * _rows(g_ref[:, cols], tm)


def _proj_norm_res(a, w, gain, x2, tm, tn, name):
    m, kdim = a.shape
    d = w.shape[1]
    return pl.pallas_call(
        _proj_norm_res_kernel,
        grid=(m // tm, d // tn),
        in_specs=[
            pl.BlockSpec((tm, kdim), lambda i, j: (i, 0)),
            pl.BlockSpec((kdim, tn), lambda i, j: (0, j), pipeline_mode=pl.Buffered(1 if tn == d else 2)),
            pl.BlockSpec((1, d), lambda i, j: (0, 0)),
            pl.BlockSpec((tm, d), lambda i, j: (i, 0)),
        ],
        out_specs=pl.BlockSpec((tm, d), lambda i, j: (i, 0)),
        out_shape=jax.ShapeDtypeStruct((m, d), F32),
        scratch_shapes=[pltpu.VMEM((d // tn, tm, tn), F32)],
        compiler_params=_params("parallel", "arbitrary"),
        name=name,
    )(a, w, gain, x2)


FFN_HALO = 16


def _ffn_up_kernel(x_ref, halo_ref, g_ref, wg_ref, wv_ref, cwg_ref, cwv_ref, cbg_ref, cbv_ref, out_ref, h_scr,
                   *, tiles_per_seq, kconv):
    tm = x_ref.shape[0]

    def norm(v):
        ms = jnp.mean(v * v, axis=-1, keepdims=True)
        return (v * lax.rsqrt(ms + EPS) * g_ref[...]).astype(BF16)

    @pl.when(pl.program_id(1) == 0)
    def _():
        seq_start = (pl.program_id(0) % tiles_per_seq) == 0
        h_scr[:FFN_HALO, :] = jnp.where(seq_start, jnp.zeros((), BF16), norm(halo_ref[...]))
        h_scr[FFN_HALO:, :] = norm(x_ref[...])

    h = h_scr[...]

    def conv(w_ref, cw_ref, cb_ref):
        up = jnp.dot(h, w_ref[...], preferred_element_type=F32)
        u = _rows(cb_ref[...], tm)
        for k in range(kconv):
            off = FFN_HALO - (kconv - 1) + k
            u = u + _rows(cw_ref[k:k + 1, :], tm) * up[off:off + tm, :]
        return u

    gate = conv(wg_ref, cwg_ref, cbg_ref)
    val = conv(wv_ref, cwv_ref, cbv_ref)
    c = math.sqrt(2.0 / math.pi)
    half = 0.5 * gate
    act = half + half * jnp.tanh(gate * (c + (c * 0.044715) * (gate * gate)))
    out_ref[...] = (act * val).astype(out_ref.dtype)


def _ffn_up(x1, gain, w_up, conv_w, conv_b, seq, tm, tn):
    m, d = x1.shape
    dff = w_up.shape[1] // 2
    nj = dff // tn
    kconv = conv_w.shape[0]
    hb = tm // FFN_HALO
    return pl.pallas_call(
        functools.partial(_ffn_up_kernel, tiles_per_seq=seq // tm, kconv=kconv),
        grid=(m // tm, nj),
        in_specs=[
            pl.BlockSpec((tm, d), lambda i, j: (i, 0)),
            pl.BlockSpec((FFN_HALO, d), lambda i, j: (jnp.maximum(i * hb - 1, 0), 0)),
            pl.BlockSpec((1, d), lambda i, j: (0, 0)),
            pl.BlockSpec((d, tn), lambda i, j: (0, j)),
            pl.BlockSpec((d, tn), lambda i, j: (0, nj + j)),
            pl.BlockSpec((kconv, tn), lambda i, j: (0, j)),
            pl.BlockSpec((kconv, tn), lambda i, j: (0, nj + j)),
            pl.BlockSpec((1, tn), lambda i, j: (0, j)),
            pl.BlockSpec((1, tn), lambda i, j: (0, nj + j)),
        ],
        out_specs=pl.BlockSpec((tm, tn), lambda i, j: (i, j)),
        out_shape=jax.ShapeDtypeStruct((m, dff), BF16),
        scratch_shapes=[pltpu.VMEM((FFN_HALO + tm, d), BF16)],
        compiler_params=_params("parallel", "arbitrary"),
        name="ffn_up",
    )(x1, x1, gain, w_up, w_up, conv_w, conv_w, conv_b, conv_b)


def _largest_tile(total, want, quantum=LANE):
    t = min(want, total)
    while total % t or t % quantum:
        t -= quantum
    return t


def _block(x, norm_mix_pre, w_in, ssd_conv_w, ssd_conv_b, dt_bias, a_log, d_skip, ssd_norm, w_sb_proj,
           w_ssd_proj, w_out, norm_mix_post, norm_ffn_pre, w_up, ffn_conv_w, ffn_conv_b, w_down, norm_ffn_post,
           dims, tiles):
    batch, seq, d = x.shape
    m = batch * seq
    sbw, inner, xbc, heads = dims.sb_width, dims.ssd_inner, dims.ssd_xbc, dims.ssd_heads
    x2 = x.reshape(m, d)
    row = lambda v: v.reshape(1, -1)

    dt_lo = 3 * sbw + inner + xbc
    w_in_t = w_in.T
    n_all = w_in_t.shape[0] - heads
    row_scale = jnp.where(jnp.arange(w_in_t.shape[0]) < sbw, dims.sb_head_dim ** -0.5, 1.0).astype(F32)[:, None]
    w_all = (w_in_t * row_scale).astype(BF16)
    w_dt = w_in_t[dt_lo:dt_lo + heads]
    z_off = 3 * sbw // LANE
    x_off = z_off + inner // LANE
    b_off = x_off + inner // LANE
    c_off = b_off + dims.ssd_groups * dims.ssd_state // LANE
    gsb_off = c_off + dims.ssd_groups * dims.ssd_state // LANE
    gssd_off = gsb_off + d // LANE

    tm = _largest_tile(seq, tiles["tm"], 16)
    p, dt_t = _in_proj(x2, row(norm_mix_pre), w_all, w_dt, heads, seq, tm,
                       _largest_tile(math.gcd(n_all, dt_lo), tiles["tn_in"]), dt_lo, heads)

    o_sb, (w_sb16, w_ssd16, w_out16, w_up16, w_down16) = _sb_attn(
        p, dims, batch, seq, min(tiles["tq"], seq), tiles["sb_chains"],
        [w_sb_proj, w_ssd_proj, w_out, w_up, w_down])

    chunk = min(tiles["chunk"], seq)
    bcast = lambda v: jnp.broadcast_to(v.reshape(-1, 1), (heads, LANE))
    y = _ssd(p, dt_t, ssd_conv_w, row(ssd_conv_b), bcast(dt_bias), bcast(a_log),
             row(jnp.repeat(d_skip, dims.ssd_head_dim)), row(ssd_norm), dims, batch, seq, chunk,
             (z_off, x_off, b_off, c_off), tiles["ssd_groups_per_step"], tiles["ssd_chunks_per_step"])

    merged = _merge(o_sb, y, p, w_sb16, w_ssd16, gsb_off, gssd_off,
                    _largest_tile(m, tiles["tm_merge"], 16), _largest_tile(d, tiles["tn_merge"]))
    tm2 = _largest_tile(m, tiles["tm_res"], 16)
    x1 = _proj_norm_res(merged, w_out16, row(norm_mix_post), x2, tm2, d, "out_proj")

    tm_up = _largest_tile(seq, tiles["tm"], 16)
    act = _ffn_up(x1, row(norm_ffn_pre), w_up16, ffn_conv_w, row(ffn_conv_b), seq, tm_up,
                  _largest_tile(dims.d_ff, tiles["tn_up"]))
    out = _proj_norm_res(act, w_down16, row(norm_ffn_post), x1, tm2,
                         _largest_tile(d, tiles["tn_down"]), "ffn_down")
    return out.reshape(batch, seq, d)


TILES = dict(tm=1024, tn_in=2048, tq=128, sb_chains=16, chunk=128, ssd_groups_per_step=4, ssd_chunks_per_step=4, tm_merge=512, tn_merge=2048, tm_res=512, tn_up=512, tn_down=2048)


def kernel(x, norm_mix_pre, w_in, ssd_conv_w, ssd_conv_b, dt_bias, a_log, d_skip, ssd_norm, w_sb_proj, w_ssd_proj,
           w_out, norm_mix_post, norm_ffn_pre, w_up, ffn_conv_w, ffn_conv_b, w_down, norm_ffn_post):
    dims = Dims()
    args = (norm_mix_pre, w_in, ssd_conv_w, ssd_conv_b, dt_bias, a_log, d_skip, ssd_norm, w_sb_proj, w_ssd_proj,
            w_out, norm_mix_post, norm_ffn_pre, w_up, ffn_conv_w, ffn_conv_b, w_down, norm_ffn_post)
    for layer in range(w_in.shape[0]):
        x = _block(x, *(a[layer] for a in args), dims, TILES)
    return x
```
